```python
import numpy as np
import jax
import jax.numpy as jnp
from jax import lax

D_MODEL = 1024
BATCH = 32
SEQ = 256
DEPTH = 4
DEC_BATCH = 4
DEC_SEQ = 4096
PAST_LEN = 512

GRID_W = 64
NA_HEADS = 8
NA_DH = 64
NA_W = NA_HEADS * NA_DH
WIN_R = 8
WIN_C = 16
QC = 16
BAND_W = QC + WIN_C
HG_HEADS = 4
HG_DK = 128
HG_DV = 128
HG_W = HG_HEADS * HG_DK
HG_CHUNK = 32
IN_SPLITS = (NA_W, NA_W, NA_W, HG_W, HG_HEADS * HG_DV, HG_W, HG_W, HG_HEADS * HG_DV)
IN_COLS = sum(IN_SPLITS)
MIX_W = NA_W + HG_HEADS * HG_DV
D_FF = 2816
N_EXPERTS = 8
TOP_K = 2
MOE_FF = 1408
N_DENSE = (DEPTH + 1) // 2
N_MOE = DEPTH // 2
CTX_QBLOCK = 128
EPS = 1e-6

kernel_name = "hybrid_na_hgrn2_dit_step"


def _rms_norm(x, w):
    xf = x.astype(jnp.float32)
    y = xf * lax.rsqrt(jnp.mean(xf * xf, axis=-1, keepdims=True) + EPS)
    return (y * w.astype(jnp.float32)).astype(x.dtype)


def _modulation(cond, w_ada, b_ada):
    m = jax.nn.silu(cond) @ w_ada + b_ada
    return jnp.split(m[:, None, :], 6, axis=-1)


def _modulate(h, shift, scale):
    return h * (1 + scale) + shift


def _split_proj(h, w_in):
    p = h @ w_in
    cuts = [int(v) for v in np.cumsum(IN_SPLITS)[:-1]]
    return jnp.split(p, cuts, axis=-1)


def _heads(t, n, d):
    return t.reshape(t.shape[0], t.shape[1], n, d)


def _ctx_attention(q, k, v):
    b, l, h, dh = q.shape
    scale = dh ** -0.5
    qb = jnp.moveaxis(q.reshape(b, l // CTX_QBLOCK, CTX_QBLOCK, h, dh), 1, 0)

    def blk(qi):
        s = jnp.einsum('bqhd,bkhd->bhqk', qi, k).astype(jnp.float32) * scale
        p = jax.nn.softmax(s, axis=-1).astype(v.dtype)
        return jnp.einsum('bhqk,bkhd->bqhd', p, v)

    o = lax.map(blk, qb)
    return jnp.moveaxis(o, 0, 1).reshape(b, l, h * dh)


def _na_indices(rows):
    kr = min(WIN_R, rows)
    r = np.arange(rows)
    row_start = np.clip(r - kr // 2, 0, rows - kr)
    dr_idx = row_start[:, None] + np.arange(kr)[None, :] - r[:, None] + (WIN_R - 1)
    ncb = GRID_W // QC
    band_start = np.clip(np.arange(ncb) * QC - WIN_C // 2, 0, GRID_W - BAND_W)
    band_cols = band_start[:, None] + np.arange(BAND_W)[None, :]
    j = np.arange(ncb)[:, None] * QC + np.arange(QC)[None, :]
    col_start = np.clip(j - WIN_C // 2, 0, GRID_W - WIN_C)
    kc = band_cols[:, None, :]
    col_mask = (kc >= col_start[..., None]) & (kc < col_start[..., None] + WIN_C)
    dc_idx = np.clip(kc - j[..., None] + (WIN_C - 1), 0, 2 * WIN_C - 2)
    return kr, row_start, dr_idx, band_cols, col_mask, dc_idx


def _na_latent(q, k, v, ck, cv, rpb):
    b, l, h, dh = q.shape
    rows = l // GRID_W
    kr, row_start, dr_idx, band_cols, col_mask, dc_idx = _na_indices(rows)
    ncb = GRID_W // QC
    scale = dh ** -0.5
    kg = k.reshape(b, rows, GRID_W, h, dh)
    vg = v.reshape(b, rows, GRID_W, h, dh)
    qr = jnp.moveaxis(q.reshape(b, rows, GRID_W, h, dh), 1, 0)
    bias_tab = rpb[:, :, dc_idx]
    mask = jnp.asarray(col_mask)[None, None, :, :, None, :]
    n_loc = kr * BAND_W

    def row_block(inp):
        q_row, rs, dr = inp
        kb = lax.dynamic_slice_in_dim(kg, rs, kr, axis=1)[:, :, band_cols]
        vb = lax.dynamic_slice_in_dim(vg, rs, kr, axis=1)[:, :, band_cols]
        qb = q_row.reshape(b, ncb, QC, h, dh)
        bias = jnp.transpose(bias_tab[:, dr], (0, 2, 3, 1, 4))
        s_loc = jnp.einsum('bnqhd,brnkhd->bhnqrk', qb, kb).astype(jnp.float32) * scale
        s_loc = jnp.where(mask, s_loc + bias[None].astype(jnp.float32), -jnp.inf)
        s_ctx = jnp.einsum('bnqhd,bchd->bhnqc', qb, ck).astype(jnp.float32) * scale
        s = jnp.concatenate([s_loc.reshape(b, h, ncb, QC, n_loc), s_ctx], axis=-1)
        p = jax.nn.softmax(s, axis=-1).astype(v.dtype)
        p_loc = p[..., :n_loc].reshape(b, h, ncb, QC, kr, BAND_W)
        p_ctx = p[..., n_loc:]
        o = (jnp.einsum('bhnqrk,brnkhd->bnqhd', p_loc, vb)
             + jnp.einsum('bhnqc,bchd->bnqhd', p_ctx, cv))
        return o.reshape(b, GRID_W, h * dh)

    o = lax.map(row_block, (qr, jnp.asarray(row_start, jnp.int32), jnp.asarray(dr_idx, jnp.int32)))
    return jnp.moveaxis(o, 0, 1).reshape(b, l, h * dh)


def _hgrn_chunk_scan(q, k, logf, v, s0):
    b, l, h, dk = q.shape
    dv = v.shape[-1]
    nc = l // HG_CHUNK

    def to_chunks(t):
        return jnp.transpose(t.reshape(b, nc, HG_CHUNK, h, t.shape[-1]), (1, 0, 3, 2, 4))

    causal = jnp.tril(jnp.ones((HG_CHUNK, HG_CHUNK), dtype=bool))[:, :, None]

    def step(state, inp):
        qc, kc, gc, vc = inp
        cum = jnp.cumsum(gc, axis=2)
        diff = cum[:, :, :, None, :] - cum[:, :, None, :, :]
        decay = jnp.exp(jnp.where(causal, diff, -jnp.inf))
        a = jnp.einsum('bhtd,bhtsd,bhsd->bhts', qc, decay, kc)
        o = (jnp.einsum('bhts,bhse->bhte', a, vc)
             + jnp.einsum('bhtd,bhde->bhte', qc * jnp.exp(cum), state))
        last = cum[:, :, -1:, :]
        new_state = (jnp.exp(last[:, :, 0, :])[..., None] * state
                     + jnp.einsum('bhsd,bhse->bhde', kc * jnp.exp(last - cum), vc))
        return new_state, o

    s_fin, o = lax.scan(step, s0, (to_chunks(q), to_chunks(k), to_chunks(logf), to_chunks(v)))
    o = jnp.transpose(o, (1, 0, 3, 2, 4)).reshape(b, l, h, dv)
    return o, s_fin


def _hgrn_mixer(q_raw, i_raw, zf, zb, g_raw, lb, norm_w, s0f, s0b):
    b, l, _ = q_raw.shape
    sh = (b, l, HG_HEADS, HG_DK)
    q = jax.nn.silu(q_raw.astype(jnp.float32)).reshape(sh)
    v = i_raw.astype(jnp.float32).reshape(b, l, HG_HEADS, HG_DV)
    lbh = lb.reshape(HG_HEADS, HG_DK)

    def gates(z):
        z = z.astype(jnp.float32).reshape(sh)
        logf = jnp.logaddexp(jnp.log(lbh), jnp.log1p(-lbh) + jax.nn.log_sigmoid(z))
        kk = (1.0 - lbh) * jax.nn.sigmoid(-z)
        return logf, kk

    lf_f, k_f = gates(zf)
    lf_b, k_b = gates(zb)
    o_f, s_f = _hgrn_chunk_scan(q, k_f, lf_f, v, s0f)
    o_b, s_b = _hgrn_chunk_scan(jnp.flip(q, axis=1), jnp.flip(k_b, axis=1),
                                jnp.flip(lf_b, axis=1), jnp.flip(v, axis=1), s0b)
    o = o_f + jnp.flip(o_b, axis=1)
    o = o * lax.rsqrt(jnp.mean(o * o, axis=-1, keepdims=True) + EPS) * norm_w.astype(jnp.float32)
    o = o.reshape(b, l, HG_HEADS * HG_DV) * jax.nn.silu(g_raw.astype(jnp.float32))
    return o.astype(q_raw.dtype), s_f, s_b


def _swiglu(h, w1, w3, w2):
    return (jax.nn.silu(h @ w1) * (h @ w3)) @ w2


def _moe_swiglu(h, router, w1, w3, w2):
    logits = (h @ router).astype(jnp.float32)
    top_v, top_i = lax.top_k(logits, TOP_K)
    wts = jax.nn.softmax(top_v, axis=-1)
    gates = jnp.sum(jax.nn.one_hot(top_i, N_EXPERTS, dtype=jnp.float32) * wts[..., None],
                    axis=-2).astype(h.dtype)
    out = jnp.zeros_like(h)
    for e in range(N_EXPERTS):
        out = out + gates[..., e:e + 1] * _swiglu(h, w1[e], w3[e], w2[e])
    return out


def _channel_mixer(h, l, ffn_w1, ffn_w3, ffn_w2, moe_router, moe_w1, moe_w3, moe_w2):
    i = l // 2
    if l % 2 == 0:
        return _swiglu(h, ffn_w1[i], ffn_w3[i], ffn_w2[i])
    return _moe_swiglu(h, moe_router[i], moe_w1[i], moe_w3[i], moe_w2[i])


def setup_inputs(seed: int = 0) -> dict:
    key = jax.random.key(seed)
    ks = jax.random.split(key, 24)
    f32 = jnp.float32

    def nrm(k, shape, s=1.0):
        return s * jax.random.normal(k, shape, f32)

    ds = D_MODEL ** -0.5
    return {
        "x_prompt": nrm(ks[0], (BATCH, SEQ, D_MODEL)),
        "x_sample": nrm(ks[1], (DEC_BATCH, DEC_SEQ, D_MODEL)),
        "cache_k": nrm(ks[2], (DEC_BATCH, DEPTH, PAST_LEN, NA_HEADS, NA_DH)),
        "cache_v": nrm(ks[3], (DEC_BATCH, DEPTH, PAST_LEN, NA_HEADS, NA_DH)),
        "state_hgrn": nrm(ks[4], (DEC_BATCH, DEPTH, 2, HG_HEADS, HG_DK, HG_DV), 0.5),
        "c": nrm(ks[5], (DEC_BATCH, D_MODEL)),
        "c_ctx": nrm(ks[6], (D_MODEL,)),
        "norm1_w": 1.0 + nrm(ks[7], (DEPTH, D_MODEL), 0.05),
        "norm2_w": 1.0 + nrm(ks[8], (DEPTH, D_MODEL), 0.05),
        "w_ada": nrm(ks[9], (DEPTH, D_MODEL, 6 * D_MODEL), 0.5 * ds),
        "b_ada": nrm(ks[10], (DEPTH, 6 * D_MODEL), 0.02),
        "w_in": nrm(ks[11], (DEPTH, D_MODEL, IN_COLS), ds),
        "rpb": nrm(ks[12], (DEPTH, NA_HEADS, 2 * WIN_R - 1, 2 * WIN_C - 1), 0.1),
        "hg_lower": nrm(ks[13], (DEPTH, HG_W), 0.5),
        "hg_norm_w": 1.0 + nrm(ks[14], (DEPTH, HG_DV), 0.05),
        "w_out": nrm(ks[15], (DEPTH, MIX_W, D_MODEL), MIX_W ** -0.5),
        "ffn_w1": nrm(ks[16], (N_DENSE, D_MODEL, D_FF), ds),
        "ffn_w3": nrm(ks[17], (N_DENSE, D_MODEL, D_FF), ds),
        "ffn_w2": nrm(ks[18], (N_DENSE, D_FF, D_MODEL), D_FF ** -0.5),
        "moe_router": nrm(ks[19], (N_MOE, D_MODEL, N_EXPERTS), ds),
        "moe_w1": nrm(ks[20], (N_MOE, N_EXPERTS, D_MODEL, MOE_FF), ds),
        "moe_w3": nrm(ks[21], (N_MOE, N_EXPERTS, D_MODEL, MOE_FF), ds),
        "moe_w2": nrm(ks[22], (N_MOE, N_EXPERTS, MOE_FF, D_MODEL), MOE_FF ** -0.5),
        "final_norm_w": 1.0 + nrm(ks[23], (D_MODEL,), 0.05),
    }


def reference(x_prompt, x_sample, cache_k, cache_v, state_hgrn, c, c_ctx,
              norm1_w, norm2_w, w_ada, b_ada, w_in, rpb, hg_lower, hg_norm_w, w_out,
              ffn_w1, ffn_w3, ffn_w2, moe_router, moe_w1, moe_w3, moe_w2, final_norm_w):
    lbs = jnp.cumsum(jax.nn.softmax(hg_lower.astype(jnp.float32), axis=0), axis=0)
    lbs = lbs - lbs[0:1]

    def layer(x, mod, l, attend, s0f, s0b):
        h = _modulate(_rms_norm(x, norm1_w[l]), mod[0], mod[1])
        qa, ka, va, qg, ig, zf, zb, gg = _split_proj(h, w_in[l])
        qa = _heads(qa, NA_HEADS, NA_DH)
        ka = _heads(ka, NA_HEADS, NA_DH)
        va = _heads(va, NA_HEADS, NA_DH)
        att = attend(qa, ka, va)
        hg, sf, sb = _hgrn_mixer(qg, ig, zf, zb, gg, lbs[l], hg_norm_w[l], s0f, s0b)
        x = x + mod[2] * (jnp.concatenate([att, hg], axis=-1) @ w_out[l])
        h = _modulate(_rms_norm(x, norm2_w[l]), mod[3], mod[4])
        x = x + mod[5] * _channel_mixer(h, l, ffn_w1, ffn_w3, ffn_w2,
                                        moe_router, moe_w1, moe_w3, moe_w2)
        return x, ka, va, sf, sb

    xp, xs = x_prompt, x_sample
    zero_state = jnp.zeros((x_prompt.shape[0], HG_HEADS, HG_DK, HG_DV), jnp.float32)
    ks_out, vs_out, ss_out = [], [], []
    for l in range(DEPTH):
        mod_p = _modulation(c_ctx[None, :], w_ada[l], b_ada[l])
        mod_s = _modulation(c, w_ada[l], b_ada[l])
        xp, ka, va, sf, sb = layer(xp, mod_p, l, _ctx_attention, zero_state, zero_state)
        ks_out.append(ka)
        vs_out.append(va)
        ss_out.append(jnp.stack([sf, sb], axis=1).astype(x_prompt.dtype))
        attend_lat = (lambda q, k, v, l=l: _na_latent(q, k, v, cache_k[:, l], cache_v[:, l], rpb[l]))
        xs, _, _, _, _ = layer(xs, mod_s, l, attend_lat,
                               state_hgrn[:, l, 0].astype(jnp.float32),
                               state_hgrn[:, l, 1].astype(jnp.float32))

    y_prompt = _rms_norm(xp, final_norm_w)
    y_sample = _rms_norm(xs, final_norm_w)
    new_cache_k = jnp.stack(ks_out, axis=1)
    new_cache_v = jnp.stack(vs_out, axis=1)
    new_state_hgrn = jnp.stack(ss_out, axis=1)
    return (y_prompt, y_sample, new_cache_k, new_cache_v, new_state_hgrn)
```

```python
import functools

import numpy as np
import jax
import jax.numpy as jnp
from jax import lax
from jax.experimental import pallas as pl
from jax.experimental.pallas import tpu as pltpu

F32 = jnp.float32
BF16 = jnp.bfloat16

EPS = 1e-6
NA_HEADS = 8
NA_DH = 64
NA_W = NA_HEADS * NA_DH
GRID_W = 64
WIN_R = 8
WIN_C = 16
HG_HEADS = 4
HG_D = 128
HG_W = HG_HEADS * HG_D
N_EXPERTS = 8
LANES = 128
SUBLANES = 8
VMEM_LIMIT = 56 * 1024 * 1024

HG_CHUNK = 128
HG_SUB = 16


def _cparams(sem):
    return pltpu.CompilerParams(dimension_semantics=sem, vmem_limit_bytes=VMEM_LIMIT)


def _dot(a, b):
    return jnp.dot(a, b, preferred_element_type=F32)


def _dot_nt(a, b):
    return lax.dot_general(a, b, (((1,), (1,)), ((), ())), preferred_element_type=F32)


def _dot_tn(a, b):
    return lax.dot_general(a, b, (((0,), (0,)), ((), ())), preferred_element_type=F32)


def _norm_mod(x, nw, shift, scale):
    ms = jnp.mean(x * x, axis=-1, keepdims=True)
    y = x * lax.rsqrt(ms + EPS) * nw
    return y * (1.0 + scale) + shift


def _mod_kernel(cond_ref, w_ref, b_ref, o_ref):
    c = cond_ref[...]
    s = c * jax.nn.sigmoid(c)
    o_ref[0] = _dot(s.astype(BF16), w_ref[0].astype(BF16)) + b_ref[0]


def _modulation(cond, w_ada, b_ada, tn=1536):
    depth, d, n = w_ada.shape
    rows = cond.shape[0]
    return pl.pallas_call(
        _mod_kernel,
        grid=(depth, n // tn),
        in_specs=[
            pl.BlockSpec((rows, d), lambda l, j: (0, 0)),
            pl.BlockSpec((1, d, tn), lambda l, j: (l, 0, j)),
            pl.BlockSpec((1, 1, tn), lambda l, j: (l, 0, j)),
        ],
        out_specs=pl.BlockSpec((1, rows, tn), lambda l, j: (l, 0, j)),
        out_shape=jax.ShapeDtypeStruct((depth, rows, n), F32),
        compiler_params=_cparams(("parallel", "parallel")),
        name="modulation",
    )(cond, w_ada, b_ada.reshape(depth, 1, n))


def _mod_index(mod):
    if mod.shape[0] == 1:
        return lambda b, *_: (0, 0, 0)
    return lambda b, *_: (b, 0, 0)


def _inproj_kernel(x_ref, nw_ref, mod_ref, w_ref, qkv_ref, hg_ref, *kv_ref):
    h = _norm_mod(x_ref[0], nw_ref[...], mod_ref[0, 0:1, :], mod_ref[0, 1:2, :])
    p = _dot(h.astype(BF16), w_ref[...])
    qkv_ref[0] = p[:, :3 * NA_W].astype(BF16)
    hg_ref[0] = p[:, 3 * NA_W:]
    if kv_ref:
        kv_ref[0][0] = p[:, NA_W:3 * NA_W]


def _inproj(x, nw, mod, w, want_kv, tm=256):
    b, l, d = x.shape
    n = w.shape[1]
    out_shape = [jax.ShapeDtypeStruct((b, l, 3 * NA_W), BF16),
                 jax.ShapeDtypeStruct((b, l, n - 3 * NA_W), F32)]
    out_specs = [pl.BlockSpec((1, tm, 3 * NA_W), lambda i, j: (i, j, 0)),
                 pl.BlockSpec((1, tm, n - 3 * NA_W), lambda i, j: (i, j, 0))]
    if want_kv:
        out_shape.append(jax.ShapeDtypeStruct((b, l, 2 * NA_W), F32))
        out_specs.append(pl.BlockSpec((1, tm, 2 * NA_W), lambda i, j: (i, j, 0)))
    return pl.pallas_call(
        _inproj_kernel,
        grid=(b, l // tm),
        in_specs=[
            pl.BlockSpec((1, tm, d), lambda i, j: (i, j, 0)),
            pl.BlockSpec((1, d), lambda i, j: (0, 0)),
            pl.BlockSpec((1, SUBLANES, d), _mod_index(mod)),
            pl.BlockSpec((d, n), lambda i, j: (0, 0)),
        ],
        out_specs=out_specs,
        out_shape=out_shape,
        compiler_params=_cparams(("parallel", "parallel")),
        name="inproj",
    )(x, nw.reshape(1, d), mod, w)


def _ctx_attn_kernel(q_ref, k_ref, v_ref, o_ref):
    scale = NA_DH ** -0.5
    for h in range(NA_HEADS):
        sl = slice(h * NA_DH, (h + 1) * NA_DH)
        q = q_ref[0, :, sl] * scale
        s = _dot_nt(q, k_ref[0, :, sl])
        m = jnp.max(s, axis=-1, keepdims=True)
        p = jnp.exp(s - m)
        den = jnp.sum(p, axis=-1, keepdims=True)
        o = _dot(p.astype(BF16), v_ref[0, :, sl]) / den
        o_ref[0, :, sl] = o.astype(BF16)


def _ctx_attention(qkv):
    b, l, _ = qkv.shape
    spec = lambda c: pl.BlockSpec((1, l, NA_W), lambda i, c=c: (i, 0, c))
    return pl.pallas_call(
        _ctx_attn_kernel,
        grid=(b,),
        in_specs=[spec(0), spec(1), spec(2)],
        out_specs=pl.BlockSpec((1, l, NA_W), lambda i: (i, 0, 0)),
        out_shape=jax.ShapeDtypeStruct((b, l, NA_W), BF16),
        compiler_params=_cparams(("parallel",)),
        name="ctx_attention",
    )(qkv, qkv, qkv)


def _na_row_start(r, rows):
    return jnp.clip(r - WIN_R // 2, 0, rows - WIN_R)


def _na_kernel(q_ref, k_ref, v_ref, ck_ref, cv_ref, bias_ref, o_ref, *, rows):
    r = pl.program_id(1)
    rs = _na_row_start(r, rows)
    start = pl.multiple_of(rs * GRID_W, GRID_W)
    n_loc = WIN_R * GRID_W
    scale = NA_DH ** -0.5
    for h in range(NA_HEADS):
        sl = slice(h * NA_DH, (h + 1) * NA_DH)
        q = q_ref[0, :, sl] * scale
        s_loc = _dot_nt(q, k_ref[0, pl.ds(start, n_loc), sl]) + bias_ref[0, h]
        s_ctx = _dot_nt(q, ck_ref[0, :, sl])
        m = jnp.maximum(jnp.max(s_loc, axis=-1, keepdims=True),
                        jnp.max(s_ctx, axis=-1, keepdims=True))
        p_loc = jnp.exp(s_loc - m)
        p_ctx = jnp.exp(s_ctx - m)
        den = jnp.sum(p_loc, axis=-1, keepdims=True) + jnp.sum(p_ctx, axis=-1, keepdims=True)
        o = (_dot(p_loc.astype(BF16), v_ref[0, pl.ds(start, n_loc), sl])
             + _dot(p_ctx.astype(BF16), cv_ref[0, :, sl])) / den
        o_ref[0, :, sl] = o.astype(BF16)


def _na_bias_slabs(rpb, rows):
    j = np.arange(GRID_W)[:, None]
    kc = np.arange(GRID_W)[None, :]
    cs = np.clip(j - WIN_C // 2, 0, GRID_W - WIN_C)
    valid = (kc >= cs) & (kc < cs + WIN_C)
    dc = np.clip(kc - j + (WIN_C - 1), 0, 2 * WIN_C - 2)
    tab = jnp.where(jnp.asarray(valid)[None, None], rpb[:, :, dc].astype(F32), -jnp.inf)
    dr = np.arange(WIN_R)[:, None] + np.arange(WIN_R)[None, :]
    slabs = tab[:, dr]
    slabs = jnp.transpose(slabs, (1, 0, 3, 2, 4))
    return slabs.reshape(WIN_R, NA_HEADS, GRID_W, WIN_R * GRID_W)


def _na_attention(qkv, ck, cv, bias):
    b, l, _ = qkv.shape
    rows = l // GRID_W
    lc = ck.shape[1]

    def bias_index(i, r):
        return (_na_row_start(r, rows) - r + (WIN_R - 1), 0, 0, 0)

    return pl.pallas_call(
        functools.partial(_na_kernel, rows=rows),
        grid=(b, rows),
        in_specs=[
            pl.BlockSpec((1, GRID_W, NA_W), lambda i, r: (i, r, 0)),
            pl.BlockSpec((1, l, NA_W), lambda i, r: (i, 0, 1)),
            pl.BlockSpec((1, l, NA_W), lambda i, r: (i, 0, 2)),
            pl.BlockSpec((1, lc, NA_W), lambda i, r: (i, 0, 0)),
            pl.BlockSpec((1, lc, NA_W), lambda i, r: (i, 0, 0)),
            pl.BlockSpec((1, NA_HEADS, GRID_W, WIN_R * GRID_W), bias_index),
        ],
        out_specs=pl.BlockSpec((1, GRID_W, NA_W), lambda i, r: (i, r, 0)),
        out_shape=jax.ShapeDtypeStruct((b, l, NA_W), BF16),
        compiler_params=_cparams(("parallel", "arbitrary")),
        name="na_attention",
    )(qkv, qkv, qkv, ck, cv, bias)


def _split3(x):
    hi = x.astype(BF16)
    r1 = x - hi.astype(F32)
    mid = r1.astype(BF16)
    lo = (r1 - mid.astype(F32)).astype(BF16)
    return hi, mid, lo


def _hgrn_chunk(z, qs, v, st, consts, reverse):
    c = z.shape[0]
    loglb, log1mlb, oml = consts
    e = jnp.exp(-jnp.abs(z))
    logsig = jnp.minimum(z, 0.0) - jnp.log1p(e)
    bb = log1mlb + logsig
    logf = jnp.maximum(loglb, bb) + jnp.log1p(jnp.exp(-jnp.abs(loglb - bb)))
    k = oml * (jnp.where(z >= 0.0, e, 1.0) / (1.0 + e))

    ti = lax.broadcasted_iota(jnp.int32, (c, c), 0)
    si = lax.broadcasted_iota(jnp.int32, (c, c), 1)
    tri = jnp.where((si >= ti) if reverse else (si <= ti), 1.0, 0.0).astype(BF16)
    hi, mid, lo = _split3(logf)
    cum = _dot(tri, hi) + _dot(tri, mid) + _dot(tri, lo)

    row = lax.broadcasted_iota(jnp.int32, (c, 1), 0)
    a_off = jnp.zeros((c, c), F32)
    m = c // 2
    while m >= HG_SUB:
        groups = c // (2 * m)
        ref_row = m if reverse else m - 1
        ref = cum.reshape(groups, 2 * m, HG_D)[:, ref_row:ref_row + 1, :]
        ref = jnp.broadcast_to(ref, (groups, 2 * m, HG_D)).reshape(c, HG_D)
        upper = jnp.bitwise_and(row, m) != 0
        q_role = jnp.logical_not(upper) if reverse else upper
        dec = jnp.exp(jnp.where(q_role, cum - ref, ref - cum))
        qm = jnp.where(q_role, qs * dec, 0.0).astype(BF16)
        km = jnp.where(q_role, 0.0, k * dec).astype(BF16)
        am = _dot_nt(qm, km)
        if groups > 1:
            am = jnp.where(jnp.bitwise_xor(ti, si) < 2 * m, am, 0.0)
        a_off = a_off + am
        m //= 2

    sub_row = lax.broadcasted_iota(jnp.int32, (SUBLANES, 1), 0)
    tiles_per_sub = HG_SUB // SUBLANES
    o_tiles = []
    for blk in range(c // HG_SUB):
        accs = [jnp.zeros((SUBLANES, HG_D), F32) for _ in range(tiles_per_sub)]
        for j in range(HG_SUB):
            s = blk * HG_SUB + j
            cs, ks, vs = cum[s:s + 1, :], k[s:s + 1, :], v[s:s + 1, :]
            for t in range(tiles_per_sub):
                lo_r, hi_r = t * SUBLANES, t * SUBLANES + SUBLANES - 1
                if (lo_r > j) if reverse else (hi_r < j):
                    continue
                base = blk * HG_SUB + lo_r
                x = qs[base:base + SUBLANES, :] * ks * jnp.exp(
                    jnp.minimum(cum[base:base + SUBLANES, :] - cs, 0.0))
                a = jnp.sum(x, axis=-1, keepdims=True)
                if reverse and hi_r > j:
                    a = jnp.where(sub_row + lo_r <= j, a, 0.0)
                elif (not reverse) and lo_r < j:
                    a = jnp.where(sub_row + lo_r >= j, a, 0.0)
                accs[t] = accs[t] + a * vs
        o_tiles.extend(accs)
    o = jnp.concatenate(o_tiles, axis=0)

    edge = cum[0:1, :] if reverse else cum[c - 1:c, :]
    o = o + _dot(a_off.astype(BF16), v.astype(BF16))
    o = o + _dot_nt((qs * jnp.exp(cum)).astype(BF16), st.astype(BF16))
    kd = (k * jnp.exp(edge - cum)).astype(BF16)
    st_new = st * jnp.exp(edge) + _dot_tn(v.astype(BF16), kd)
    return o, st_new


def _hgrn_kernel(q_ref, i_ref, zf_ref, zb_ref, g_ref, lbc_ref, nw_ref, s0_ref,
                 o_ref, s_ref, acc_ref, stf_ref, stb_ref):
    l = q_ref.shape[1]
    c = HG_CHUNK
    nc = l // c
    consts = (lbc_ref[0:1, :], lbc_ref[1:2, :], lbc_ref[2:3, :])
    stf_ref[...] = s0_ref[0, 0, 0].T
    stb_ref[...] = s0_ref[0, 1, 0].T
    acc_ref[...] = jnp.zeros_like(acc_ref)

    def body(i, carry):
        for reverse, z_ref, st_ref in ((False, zf_ref, stf_ref), (True, zb_ref, stb_ref)):
            ci = (nc - 1 - i) if reverse else i
            rows = pl.ds(pl.multiple_of(ci * c, c), c)
            q = q_ref[0, rows, :]
            qs = q * jax.nn.sigmoid(q)
            o, st_new = _hgrn_chunk(z_ref[0, rows, :], qs, i_ref[0, rows, :], st_ref[...],
                                    consts, reverse)
            st_ref[...] = st_new
            acc_ref[rows, :] += o
        return carry

    lax.fori_loop(0, nc, body, 0)
    s_ref[0, 0, 0] = stf_ref[...].T
    s_ref[0, 1, 0] = stb_ref[...].T

    def finish(i, carry):
        rows = pl.ds(pl.multiple_of(i * c, c), c)
        o = acc_ref[rows, :]
        o = o * lax.rsqrt(jnp.mean(o * o, axis=-1, keepdims=True) + EPS) * nw_ref[...]
        g = g_ref[0, rows, :]
        o_ref[0, rows, :] = (o * (g * jax.nn.sigmoid(g))).astype(BF16)
        return carry

    lax.fori_loop(0, nc, finish, 0)


def _hgrn(hgp, lbc, norm_w, s0):
    b, l, _ = hgp.shape
    col = lambda sec: pl.BlockSpec((1, l, HG_D), lambda i, h, sec=sec: (i, 0, sec * HG_HEADS + h))
    state_spec = pl.BlockSpec((1, 2, 1, HG_D, HG_D), lambda i, h: (i, 0, h, 0, 0))
    return pl.pallas_call(
        _hgrn_kernel,
        grid=(b, HG_HEADS),
        in_specs=[col(0), col(1), col(2), col(3), col(4),
                  pl.BlockSpec((SUBLANES, HG_D), lambda i, h: (0, h)),
                  pl.BlockSpec((1, HG_D), lambda i, h: (0, 0)),
                  state_spec],
        out_specs=[pl.BlockSpec((1, l, HG_D), lambda i, h: (i, 0, h)), state_spec],
        out_shape=[jax.ShapeDtypeStruct((b, l, HG_W), BF16),
                   jax.ShapeDtypeStruct((b, 2, HG_HEADS, HG_D, HG_D), F32)],
        scratch_shapes=[pltpu.VMEM((l, HG_D), F32),
                        pltpu.VMEM((HG_D, HG_D), F32),
                        pltpu.VMEM((HG_D, HG_D), F32)],
        compiler_params=_cparams(("parallel", "parallel")),
        name="hgrn",
    )(hgp, hgp, hgp, hgp, hgp, lbc, norm_w.reshape(1, HG_D), s0)


def _outproj_kernel(x_ref, att_ref, hg_ref, mod_ref, w_ref, o_ref):
    y = _dot(att_ref[0], w_ref[:NA_W, :]) + _dot(hg_ref[0], w_ref[NA_W:, :])
    o_ref[0] = x_ref[0] + mod_ref[0, 2:3, :] * y


def _outproj(x, att, hg, mod, w, tm=512):
    b, l, d = x.shape
    tm = min(tm, l)
    return pl.pallas_call(
        _outproj_kernel,
        grid=(b, l // tm),
        in_specs=[
            pl.BlockSpec((1, tm, d), lambda i, j: (i, j, 0)),
            pl.BlockSpec((1, tm, NA_W), lambda i, j: (i, j, 0)),
            pl.BlockSpec((1, tm, HG_W), lambda i, j: (i, j, 0)),
            pl.BlockSpec((1, SUBLANES, d), _mod_index(mod)),
            pl.BlockSpec(w.shape, lambda i, j: (0, 0)),
        ],
        out_specs=pl.BlockSpec((1, tm, d), lambda i, j: (i, j, 0)),
        out_shape=jax.ShapeDtypeStruct(x.shape, F32),
        compiler_params=_cparams(("parallel", "parallel")),
        name="outproj",
    )(x, att, hg, mod, w)


def _ffn_kernel(x_ref, nw_ref, mod_ref, w1_ref, w3_ref, w2_ref, o_ref, h_ref, acc_ref):
    f = pl.program_id(2)

    @pl.when(f == 0)
    def _():
        h = _norm_mod(x_ref[0], nw_ref[...], mod_ref[0, 3:4, :], mod_ref[0, 4:5, :])
        h_ref[...] = h.astype(BF16)
        acc_ref[...] = jnp.zeros_like(acc_ref)

    h = h_ref[...]
    a = _dot(h, w1_ref[...])
    g = (a * jax.nn.sigmoid(a)) * _dot(h, w3_ref[...])
    acc_ref[...] += _dot(g.astype(BF16), w2_ref[...])

    @pl.when(f == pl.num_programs(2) - 1)
    def _():
        o_ref[0] = x_ref[0] + mod_ref[0, 5:6, :] * acc_ref[...]


def _ffn(x, nw, mod, w1, w3, w2, tm=512, tf=1408):
    b, l, d = x.shape
    tm = min(tm, l)
    ff = w1.shape[1]
    tf = min(tf, ff)
    return pl.pallas_call(
        _ffn_kernel,
        grid=(b, l // tm, ff // tf),
        in_specs=[
            pl.BlockSpec((1, tm, d), lambda i, j, f: (i, j, 0)),
            pl.BlockSpec((1, d), lambda i, j, f: (0, 0)),
            pl.BlockSpec((1, SUBLANES, d), _mod_index(mod)),
            pl.BlockSpec((d, tf), lambda i, j, f: (0, f)),
            pl.BlockSpec((d, tf), lambda i, j, f: (0, f)),
            pl.BlockSpec((tf, d), lambda i, j, f: (f, 0)),
        ],
        out_specs=pl.BlockSpec((1, tm, d), lambda i, j, f: (i, j, 0)),
        out_shape=jax.ShapeDtypeStruct(x.shape, F32),
        scratch_shapes=[pltpu.VMEM((tm, d), BF16), pltpu.VMEM((tm, d), F32)],
        compiler_params=_cparams(("parallel", "parallel", "arbitrary")),
        name="ffn",
    )(x, nw.reshape(1, d), mod, w1, w3, w2)


def _router_kernel(x_ref, nw_ref, mod_ref, r_ref, g_ref):
    h = _norm_mod(x_ref[0], nw_ref[...], mod_ref[0, 3:4, :], mod_ref[0, 4:5, :])
    hh = h.astype(BF16)
    hl = (h - hh.astype(F32)).astype(BF16)
    r = r_ref[...]
    rh = r.astype(BF16)
    rl = (r - rh.astype(F32)).astype(BF16)
    logits = _dot(hh, rh) + _dot(hh, rl) + _dot(hl, rh)
    lane = lax.broadcasted_iota(jnp.int32, logits.shape, 1)
    logits = jnp.where(lane < N_EXPERTS, logits, -jnp.inf)
    m1 = jnp.max(logits, axis=-1, keepdims=True)
    i1 = jnp.min(jnp.where(logits == m1, lane, LANES), axis=-1, keepdims=True)
    rest = jnp.where(lane == i1, -jnp.inf, logits)
    m2 = jnp.max(rest, axis=-1, keepdims=True)
    i2 = jnp.min(jnp.where(rest == m2, lane, LANES), axis=-1, keepdims=True)
    e2 = jnp.exp(m2 - m1)
    den = 1.0 + e2
    g_ref[0] = jnp.where(lane == i1, 1.0 / den, 0.0) + jnp.where(lane == i2, e2 / den, 0.0)


def _router(x, nw, mod, router, tm=512):
    b, l, d = x.shape
    tm = min(tm, l)
    rp = jnp.zeros((d, LANES), F32).at[:, :router.shape[1]].set(router)
    return pl.pallas_call(
        _router_kernel,
        grid=(b, l // tm),
        in_specs=[
            pl.BlockSpec((1, tm, d), lambda i, j: (i, j, 0)),
            pl.BlockSpec((1, d), lambda i, j: (0, 0)),
            pl.BlockSpec((1, SUBLANES, d), _mod_index(mod)),
            pl.BlockSpec((d, LANES), lambda i, j: (0, 0)),
        ],
        out_specs=pl.BlockSpec((1, tm, LANES), lambda i, j: (i, j, 0)),
        out_shape=jax.ShapeDtypeStruct((b, l, LANES), F32),
        compiler_params=_cparams(("parallel", "parallel")),
        name="router",
    )(x, nw.reshape(1, d), mod, rp)


def _moe_kernel(x_ref, nw_ref, mod_ref, gate_ref, w1_ref, w3_ref, w2_ref, o_ref,
                h_ref, acc_ref, acc_e_ref):
    e = pl.program_id(2)
    f = pl.program_id(3)
    nf = pl.num_programs(3)

    @pl.when((e == 0) & (f == 0))
    def _():
        h = _norm_mod(x_ref[0], nw_ref[...], mod_ref[0, 3:4, :], mod_ref[0, 4:5, :])
        h_ref[...] = h.astype(BF16)
        acc_ref[...] = jnp.zeros_like(acc_ref)

    h = h_ref[...]
    a = _dot(h, w1_ref[0])
    g = (a * jax.nn.sigmoid(a)) * _dot(h, w3_ref[0])
    y = _dot(g.astype(BF16), w2_ref[0])

    @pl.when(f == 0)
    def _():
        acc_e_ref[...] = y

    @pl.when(f > 0)
    def _():
        acc_e_ref[...] += y

    @pl.when(f == nf - 1)
    def _():
        gates = gate_ref[0]
        lane = lax.broadcasted_iota(jnp.int32, gates.shape, 1)
        ge = jnp.sum(jnp.where(lane == e, gates, 0.0), axis=-1, keepdims=True)
        acc_ref[...] += ge * acc_e_ref[...]

    @pl.when((e == pl.num_programs(2) - 1) & (f == nf - 1))
    def _():
        o_ref[0] = x_ref[0] + mod_ref[0, 5:6, :] * acc_ref[...]


def _moe(x, nw, mod, gates, w1, w3, w2, tm=512, tf=1408):
    b, l, d = x.shape
    tm = min(tm, l)
    ne, _, ff = w1.shape
    tf = min(tf, ff)
    return pl.pallas_call(
        _moe_kernel,
        grid=(b, l // tm, ne, ff // tf),
        in_specs=[
            pl.BlockSpec((1, tm, d), lambda i, j, e, f: (i, j, 0)),
            pl.BlockSpec((1, d), lambda i, j, e, f: (0, 0)),
            pl.BlockSpec((1, SUBLANES, d), _mod_index(mod)),
            pl.BlockSpec((1, tm, LANES), lambda i, j, e, f: (i, j, 0)),
            pl.BlockSpec((1, d, tf), lambda i, j, e, f: (e, 0, f)),
            pl.BlockSpec((1, d, tf), lambda i, j, e, f: (e, 0, f)),
            pl.BlockSpec((1, tf, d), lambda i, j, e, f: (e, f, 0)),
        ],
        out_specs=pl.BlockSpec((1, tm, d), lambda i, j, e, f: (i, j, 0)),
        out_shape=jax.ShapeDtypeStruct(x.shape, F32),
        scratch_shapes=[pltpu.VMEM((tm, d), BF16), pltpu.VMEM((tm, d), F32),
                        pltpu.VMEM((tm, d), F32)],
        compiler_params=_cparams(("parallel", "parallel", "arbitrary", "arbitrary")),
        name="moe",
    )(x, nw.reshape(1, d), mod, gates, w1, w3, w2)


def _final_norm_kernel(x_ref, nw_ref, o_ref):
    x = x_ref[0]
    o_ref[0] = x * lax.rsqrt(jnp.mean(x * x, axis=-1, keepdims=True) + EPS) * nw_ref[...]


def _final_norm(x, nw, tm=512):
    b, l, d = x.shape
    tm = min(tm, l)
    return pl.pallas_call(
        _final_norm_kernel,
        grid=(b, l // tm),
        in_specs=[pl.BlockSpec((1, tm, d), lambda i, j: (i, j, 0)),
                  pl.BlockSpec((1, d), lambda i, j: (0, 0))],
        out_specs=pl.BlockSpec((1, tm, d), lambda i, j: (i, j, 0)),
        out_shape=jax.ShapeDtypeStruct(x.shape, F32),
        compiler_params=_cparams(("parallel", "parallel")),
        name="final_norm",
    )(x, nw.reshape(1, d))


def _pad_rows(a, rows):
    return jnp.zeros((rows,) + a.shape[1:], a.dtype).at[:a.shape[0]].set(a)


def kernel(x_prompt, x_sample, cache_k, cache_v, state_hgrn, c, c_ctx, norm1_w, norm2_w, w_ada, b_ada,
           w_in, rpb, hg_lower, hg_norm_w, w_out, ffn_w1, ffn_w3, ffn_w2, moe_router, moe_w1, moe_w3,
           moe_w2, final_norm_w):
    depth = w_in.shape[0]
    d = x_prompt.shape[-1]
    nb = x_prompt.shape[0]
    nd = x_sample.shape[0]
    rows = x_sample.shape[1] // GRID_W

    lbs = jnp.cumsum(jax.nn.softmax(hg_lower.astype(F32), axis=0), axis=0)
    lbs = lbs - lbs[0:1]
    lbc = _pad_rows(jnp.stack([jnp.log(lbs), jnp.log1p(-lbs), 1.0 - lbs], axis=1).transpose(1, 0, 2),
                    SUBLANES).transpose(1, 0, 2)

    cond = _pad_rows(jnp.concatenate([c, c_ctx[None, :]], axis=0), SUBLANES)
    mods = _modulation(cond, w_ada, b_ada)
    mods = _pad_rows(mods.reshape(depth, SUBLANES, 6, d).transpose(2, 0, 1, 3),
                     SUBLANES).transpose(1, 2, 0, 3)

    w_in_b = w_in.astype(BF16)
    w_out_b = w_out.astype(BF16)
    ffn_w1_b, ffn_w3_b, ffn_w2_b = ffn_w1.astype(BF16), ffn_w3.astype(BF16), ffn_w2.astype(BF16)
    moe_w1_b, moe_w3_b, moe_w2_b = moe_w1.astype(BF16), moe_w3.astype(BF16), moe_w2.astype(BF16)
    lc = cache_k.shape[2]
    ck_b = cache_k.astype(BF16).reshape(nd, depth, lc, NA_W)
    cv_b = cache_v.astype(BF16).reshape(nd, depth, lc, NA_W)
    zero_state = jnp.zeros((nb, 2, HG_HEADS, HG_D, HG_D), F32)

    def layer(x, seq_shape, mod, l, attend, s0, want_kv):
        tok = lambda a: a.reshape(x.shape[:2] + a.shape[-1:])
        seq = lambda a: a.reshape(seq_shape + a.shape[-1:])
        outs = _inproj(x, norm1_w[l], mod, w_in_b[l], want_kv)
        att = attend(seq(outs[0]))
        hg, s_fin = _hgrn(seq(outs[1]), lbc[l], hg_norm_w[l], s0)
        x = _outproj(x, tok(att), tok(hg), mod, w_out_b[l])
        i = l // 2
        if l % 2 == 0:
            x = _ffn(x, norm2_w[l], mod, ffn_w1_b[i], ffn_w3_b[i], ffn_w2_b[i])
        else:
            gates = _router(x, norm2_w[l], mod, moe_router[i])
            x = _moe(x, norm2_w[l], mod, gates, moe_w1_b[i], moe_w3_b[i], moe_w2_b[i])
        return x, (seq(outs[2]) if want_kv else None), s_fin

    seq = x_prompt.shape[1]
    xp, xs = x_prompt.reshape(1, nb * seq, d), x_sample
    ks_out, vs_out, ss_out = [], [], []
    for l in range(depth):
        mod_p = mods[l, nd:nd + 1]
        mod_s = mods[l, :nd]
        xp, kv, s_fin = layer(xp, (nb, seq), mod_p, l, _ctx_attention, zero_state, True)
        ks_out.append(kv[..., :NA_W])
        vs_out.append(kv[..., NA_W:])
        ss_out.append(s_fin)
        bias = _na_bias_slabs(rpb[l], rows)
        attend_lat = lambda qkv, l=l, bias=bias: _na_attention(qkv, ck_b[:, l], cv_b[:, l], bias)
        xs, _, _ = layer(xs, x_sample.shape[:2], mod_s, l, attend_lat,
                         state_hgrn[:, l].astype(F32), False)

    y_prompt = _final_norm(xp, final_norm_w).reshape(x_prompt.shape)
    y_sample = _final_norm(xs, final_norm_w)
    new_cache_k = jnp.stack(ks_out, axis=1).reshape(nb, depth, seq, NA_HEADS, NA_DH)
    new_cache_v = jnp.stack(vs_out, axis=1).reshape(nb, depth, seq, NA_HEADS, NA_DH)
    new_state = jnp.stack(ss_out, axis=1).astype(x_prompt.dtype)
    return (y_prompt, y_sample, new_cache_k, new_cache_v, new_state)
```

```python
import functools

import numpy as np
import jax
import jax.numpy as jnp
from jax import lax
from jax.experimental import pallas as pl
from jax.experimental.pallas import tpu as pltpu

F32 = jnp.float32
BF16 = jnp.bfloat16

EPS = 1e-6
NA_HEADS = 8
NA_DH = 64
NA_W = NA_HEADS * NA_DH
GRID_W = 64
WIN_R = 8
WIN_C = 16
HG_HEADS = 4
HG_D = 128
HG_W = HG_HEADS * HG_D
N_EXPERTS = 8
LANES = 128
SUBLANES = 8
VMEM_LIMIT = 56 * 1024 * 1024

HG_CHUNK = 128
HG_SUB = 16


def _cparams(sem):
    return pltpu.CompilerParams(dimension_semantics=sem, vmem_limit_bytes=VMEM_LIMIT)


def _dot(a, b):
    return jnp.dot(a, b, preferred_element_type=F32)


def _dot_nt(a, b):
    return lax.dot_general(a, b, (((1,), (1,)), ((), ())), preferred_element_type=F32)


def _dot_tn(a, b):
    return lax.dot_general(a, b, (((0,), (0,)), ((), ())), preferred_element_type=F32)


def _norm_mod(x, nw, shift, scale):
    ms = jnp.mean(x * x, axis=-1, keepdims=True)
    y = x * lax.rsqrt(ms + EPS) * nw
    return y * (1.0 + scale) + shift


def _mod_kernel(cond_ref, w_ref, b_ref, o_ref):
    c = cond_ref[...]
    s = c * jax.nn.sigmoid(c)
    o_ref[0] = _dot(s.astype(BF16), w_ref[0].astype(BF16)) + b_ref[0]


def _modulation(cond, w_ada, b_ada, tn=1536):
    depth, d, n = w_ada.shape
    rows = cond.shape[0]
    return pl.pallas_call(
        _mod_kernel,
        grid=(depth, n // tn),
        in_specs=[
            pl.BlockSpec((rows, d), lambda l, j: (0, 0)),
            pl.BlockSpec((1, d, tn), lambda l, j: (l, 0, j)),
            pl.BlockSpec((1, 1, tn), lambda l, j: (l, 0, j)),
        ],
        out_specs=pl.BlockSpec((1, rows, tn), lambda l, j: (l, 0, j)),
        out_shape=jax.ShapeDtypeStruct((depth, rows, n), F32),
        compiler_params=_cparams(("parallel", "parallel")),
        name="modulation",
    )(cond, w_ada, b_ada.reshape(depth, 1, n))


def _mod_index(mod):
    if mod.shape[0] == 1:
        return lambda b, *_: (0, 0, 0)
    return lambda b, *_: (b, 0, 0)


def _inproj_kernel(x_ref, nw_ref, mod_ref, w_ref, *refs, want_kt):
    if want_kt:
        wkt_ref, qkv_ref, hg_ref, extra_ref = refs
    else:
        qkv_ref, hg_ref, extra_ref = refs
    h = _norm_mod(x_ref[0], nw_ref[...], mod_ref[0, 0:1, :], mod_ref[0, 1:2, :]).astype(BF16)
    p = _dot(h, w_ref[...])
    qkv_ref[0] = p[:, :3 * NA_W].astype(BF16)
    hg_ref[0] = p[:, 3 * NA_W:]
    if want_kt:
        extra_ref[0] = _dot_nt(wkt_ref[...], h).astype(BF16)
    else:
        extra_ref[0] = p[:, NA_W:3 * NA_W]


def _inproj(x, nw, mod, w, wkt=None, tm=256):
    b, l, d = x.shape
    n = w.shape[1]
    want_kt = wkt is not None
    out_shape = [jax.ShapeDtypeStruct((b, l, 3 * NA_W), BF16),
                 jax.ShapeDtypeStruct((b, l, n - 3 * NA_W), F32)]
    out_specs = [pl.BlockSpec((1, tm, 3 * NA_W), lambda i, j: (i, j, 0)),
                 pl.BlockSpec((1, tm, n - 3 * NA_W), lambda i, j: (i, j, 0))]
    in_specs = [pl.BlockSpec((1, tm, d), lambda i, j: (i, j, 0)),
                pl.BlockSpec((1, d), lambda i, j: (0, 0)),
                pl.BlockSpec((1, SUBLANES, d), _mod_index(mod)),
                pl.BlockSpec((d, n), lambda i, j: (0, 0))]
    args = [x, nw.reshape(1, d), mod, w]
    if want_kt:
        in_specs.append(pl.BlockSpec((NA_W, d), lambda i, j: (0, 0)))
        args.append(wkt)
        out_shape.append(jax.ShapeDtypeStruct((b, NA_W, l), BF16))
        out_specs.append(pl.BlockSpec((1, NA_W, tm), lambda i, j: (i, 0, j)))
    else:
        out_shape.append(jax.ShapeDtypeStruct((b, l, 2 * NA_W), F32))
        out_specs.append(pl.BlockSpec((1, tm, 2 * NA_W), lambda i, j: (i, j, 0)))
    return pl.pallas_call(
        functools.partial(_inproj_kernel, want_kt=want_kt),
        grid=(b, l // tm),
        in_specs=in_specs,
        out_specs=out_specs,
        out_shape=out_shape,
        compiler_params=_cparams(("parallel", "parallel")),
        name="inproj",
    )(*args)


def _ctx_attn_kernel(q_ref, k_ref, v_ref, o_ref):
    scale = NA_DH ** -0.5
    for h in range(NA_HEADS):
        sl = slice(h * NA_DH, (h + 1) * NA_DH)
        q = q_ref[0, :, sl] * scale
        s = _dot_nt(q, k_ref[0, :, sl])
        m = jnp.max(s, axis=-1, keepdims=True)
        p = jnp.exp(s - m)
        den = jnp.sum(p, axis=-1, keepdims=True)
        o = _dot(p.astype(BF16), v_ref[0, :, sl]) / den
        o_ref[0, :, sl] = o.astype(BF16)


def _ctx_attention(qkv):
    b, l, _ = qkv.shape
    spec = lambda c: pl.BlockSpec((1, l, NA_W), lambda i, c=c: (i, 0, c))
    return pl.pallas_call(
        _ctx_attn_kernel,
        grid=(b,),
        in_specs=[spec(0), spec(1), spec(2)],
        out_specs=pl.BlockSpec((1, l, NA_W), lambda i: (i, 0, 0)),
        out_shape=jax.ShapeDtypeStruct((b, l, NA_W), BF16),
        compiler_params=_cparams(("parallel",)),
        name="ctx_attention",
    )(qkv, qkv, qkv)


NA_QROWS = 2
NA_KROWS = WIN_R + NA_QROWS


def _na_window_start(p, rows):
    return jnp.clip(NA_QROWS * p - WIN_R // 2, 0, rows - NA_KROWS)


def _na_kernel(q_ref, kt_ref, v_ref, ckt_ref, cv_ref, bias_ref, o_ref, s_ref, *, rows):
    ws = _na_window_start(pl.program_id(1), rows)
    start = pl.multiple_of(ws * GRID_W, NA_QROWS * GRID_W)
    n_loc = NA_KROWS * GRID_W
    scale = NA_DH ** -0.5
    maxes = []
    for h in range(NA_HEADS):
        sl = slice(h * NA_DH, (h + 1) * NA_DH)
        q = q_ref[0, :, sl] * scale
        s_loc = _dot(q, kt_ref[0, sl, pl.ds(start, n_loc)]) + bias_ref[0, h]
        s_ctx = _dot(q, ckt_ref[0, sl, :])
        s_ref[h, :, :n_loc] = s_loc
        s_ref[h, :, n_loc:] = s_ctx
        maxes.append(jnp.maximum(jnp.max(s_loc, axis=-1, keepdims=True),
                                 jnp.max(s_ctx, axis=-1, keepdims=True)))
    for h in range(NA_HEADS):
        sl = slice(h * NA_DH, (h + 1) * NA_DH)
        p_loc = jnp.exp(s_ref[h, :, :n_loc] - maxes[h])
        p_ctx = jnp.exp(s_ref[h, :, n_loc:] - maxes[h])
        den = jnp.sum(p_loc, axis=-1, keepdims=True) + jnp.sum(p_ctx, axis=-1, keepdims=True)
        o = (_dot(p_loc.astype(BF16), v_ref[0, pl.ds(start, n_loc), sl])
             + _dot(p_ctx.astype(BF16), cv_ref[0, :, sl])) / den
        o_ref[0, :, sl] = o.astype(BF16)


NA_EDGE = 2


def _na_variant(p, n_pairs):
    return jnp.where(p < NA_EDGE, p, jnp.where(p >= n_pairs - NA_EDGE, p - (n_pairs - 2 * NA_EDGE - 1), NA_EDGE))


def _na_bias_slabs(rpb, rows):
    n_pairs = rows // NA_QROWS
    ps = np.array([0, 1, 2, n_pairs - 2, n_pairs - 1])
    r = NA_QROWS * ps[:, None] + np.arange(NA_QROWS)[None, :]
    rs = np.clip(r - WIN_R // 2, 0, rows - WIN_R)
    ws = np.clip(NA_QROWS * ps - WIN_R // 2, 0, rows - NA_KROWS)
    krow = ws[:, None] + np.arange(NA_KROWS)[None, :]
    row_ok = (krow[:, None, :] >= rs[:, :, None]) & (krow[:, None, :] < rs[:, :, None] + WIN_R)
    dr = np.clip(krow[:, None, :] - r[:, :, None] + (WIN_R - 1), 0, 2 * WIN_R - 2)
    j = np.arange(GRID_W)[:, None]
    kc = np.arange(GRID_W)[None, :]
    cs = np.clip(j - WIN_C // 2, 0, GRID_W - WIN_C)
    col_ok = (kc >= cs) & (kc < cs + WIN_C)
    dc = np.clip(kc - j + (WIN_C - 1), 0, 2 * WIN_C - 2)
    tab = rpb.astype(F32)[:, dr[:, :, None, :, None], dc[None, None, :, None, :]]
    ok = row_ok[:, :, None, :, None] & col_ok[None, None, :, None, :]
    tab = jnp.where(jnp.asarray(ok)[None], tab, -jnp.inf)
    tab = jnp.transpose(tab, (1, 0, 2, 3, 4, 5))
    return tab.reshape(len(ps), NA_HEADS, NA_QROWS * GRID_W, NA_KROWS * GRID_W)


def _na_attention(qkv, kt, ckt, cv, bias):
    b, l, _ = qkv.shape
    rows = l // GRID_W
    n_pairs = rows // NA_QROWS
    lc = cv.shape[1]
    tq = NA_QROWS * GRID_W
    n_loc = NA_KROWS * GRID_W
    return pl.pallas_call(
        functools.partial(_na_kernel, rows=rows),
        grid=(b, n_pairs),
        in_specs=[
            pl.BlockSpec((1, tq, NA_W), lambda i, p: (i, p, 0)),
            pl.BlockSpec((1, NA_W, l), lambda i, p: (i, 0, 0)),
            pl.BlockSpec((1, l, NA_W), lambda i, p: (i, 0, 2)),
            pl.BlockSpec((1, NA_W, lc), lambda i, p: (i, 0, 0)),
            pl.BlockSpec((1, lc, NA_W), lambda i, p: (i, 0, 0)),
            pl.BlockSpec((1, NA_HEADS, tq, n_loc), lambda i, p: (_na_variant(p, n_pairs), 0, 0, 0)),
        ],
        out_specs=pl.BlockSpec((1, tq, NA_W), lambda i, p: (i, p, 0)),
        out_shape=jax.ShapeDtypeStruct((b, l, NA_W), BF16),
        scratch_shapes=[pltpu.VMEM((NA_HEADS, tq, n_loc + lc), F32)],
        compiler_params=_cparams(("parallel", "arbitrary")),
        name="na_attention",
    )(qkv, kt, qkv, ckt, cv, bias)


def _split3(x):
    hi = x.astype(BF16)
    r1 = x - hi.astype(F32)
    mid = r1.astype(BF16)
    lo = (r1 - mid.astype(F32)).astype(BF16)
    return hi, mid, lo


def _hgrn_chunk(z, qs, v, st, consts, reverse):
    c = z.shape[0]
    loglb, log1mlb, oml = consts
    e = jnp.exp(-jnp.abs(z))
    logsig = jnp.minimum(z, 0.0) - jnp.log1p(e)
    bb = log1mlb + logsig
    logf = jnp.maximum(loglb, bb) + jnp.log1p(jnp.exp(-jnp.abs(loglb - bb)))
    k = oml * (jnp.where(z >= 0.0, e, 1.0) / (1.0 + e))

    ti = lax.broadcasted_iota(jnp.int32, (c, c), 0)
    si = lax.broadcasted_iota(jnp.int32, (c, c), 1)
    tri = jnp.where((si >= ti) if reverse else (si <= ti), 1.0, 0.0).astype(BF16)
    hi, mid, lo = _split3(logf)
    cum = _dot(tri, hi) + _dot(tri, mid) + _dot(tri, lo)

    row = lax.broadcasted_iota(jnp.int32, (c, 1), 0)
    a_off = jnp.zeros((c, c), F32)
    m = c // 2
    while m >= HG_SUB:
        groups = c // (2 * m)
        ref_row = m if reverse else m - 1
        ref = cum.reshape(groups, 2 * m, HG_D)[:, ref_row:ref_row + 1, :]
        ref = jnp.broadcast_to(ref, (groups, 2 * m, HG_D)).reshape(c, HG_D)
        upper = jnp.bitwise_and(row, m) != 0
        q_role = jnp.logical_not(upper) if reverse else upper
        dec = jnp.exp(jnp.where(q_role, cum - ref, ref - cum))
        qm = jnp.where(q_role, qs * dec, 0.0).astype(BF16)
        km = jnp.where(q_role, 0.0, k * dec).astype(BF16)
        am = _dot_nt(qm, km)
        if groups > 1:
            am = jnp.where(jnp.bitwise_xor(ti, si) < 2 * m, am, 0.0)
        a_off = a_off + am
        m //= 2

    sub_row = lax.broadcasted_iota(jnp.int32, (SUBLANES, 1), 0)
    tiles_per_sub = HG_SUB // SUBLANES
    o_tiles = []
    for blk in range(c // HG_SUB):
        accs = [jnp.zeros((SUBLANES, HG_D), F32) for _ in range(tiles_per_sub)]
        for j in range(HG_SUB):
            s = blk * HG_SUB + j
            cs, ks, vs = cum[s:s + 1, :], k[s:s + 1, :], v[s:s + 1, :]
            for t in range(tiles_per_sub):
                lo_r, hi_r = t * SUBLANES, t * SUBLANES + SUBLANES - 1
                if (lo_r > j) if reverse else (hi_r < j):
                    continue
                base = blk * HG_SUB + lo_r
                x = qs[base:base + SUBLANES, :] * ks * jnp.exp(
                    jnp.minimum(cum[base:base + SUBLANES, :] - cs, 0.0))
                a = jnp.sum(x, axis=-1, keepdims=True)
                if reverse and hi_r > j:
                    a = jnp.where(sub_row + lo_r <= j, a, 0.0)
                elif (not reverse) and lo_r < j:
                    a = jnp.where(sub_row + lo_r >= j, a, 0.0)
                accs[t] = accs[t] + a * vs
        o_tiles.extend(accs)
    o = jnp.concatenate(o_tiles, axis=0)

    edge = cum[0:1, :] if reverse else cum[c - 1:c, :]
    o = o + _dot(a_off.astype(BF16), v.astype(BF16))
    o = o + _dot_nt((qs * jnp.exp(cum)).astype(BF16), st.astype(BF16))
    kd = (k * jnp.exp(edge - cum)).astype(BF16)
    st_new = st * jnp.exp(edge) + _dot_tn(v.astype(BF16), kd)
    return o, st_new


def _hgrn_kernel(q_ref, i_ref, zf_ref, zb_ref, g_ref, lbc_ref, nw_ref, s0_ref,
                 o_ref, s_ref, acc_ref, stf_ref, stb_ref):
    l = q_ref.shape[1]
    c = HG_CHUNK
    nc = l // c
    consts = (lbc_ref[0:1, :], lbc_ref[1:2, :], lbc_ref[2:3, :])
    stf_ref[...] = s0_ref[0, 0, 0].T
    stb_ref[...] = s0_ref[0, 1, 0].T
    acc_ref[...] = jnp.zeros_like(acc_ref)

    def body(i, carry):
        for reverse, z_ref, st_ref in ((False, zf_ref, stf_ref), (True, zb_ref, stb_ref)):
            ci = (nc - 1 - i) if reverse else i
            rows = pl.ds(pl.multiple_of(ci * c, c), c)
            q = q_ref[0, rows, :]
            qs = q * jax.nn.sigmoid(q)
            o, st_new = _hgrn_chunk(z_ref[0, rows, :], qs, i_ref[0, rows, :], st_ref[...],
                                    consts, reverse)
            st_ref[...] = st_new
            acc_ref[rows, :] += o
        return carry

    lax.fori_loop(0, nc, body, 0)
    s_ref[0, 0, 0] = stf_ref[...].T
    s_ref[0, 1, 0] = stb_ref[...].T

    def finish(i, carry):
        rows = pl.ds(pl.multiple_of(i * c, c), c)
        o = acc_ref[rows, :]
        o = o * lax.rsqrt(jnp.mean(o * o, axis=-1, keepdims=True) + EPS) * nw_ref[...]
        g = g_ref[0, rows, :]
        o_ref[0, rows, :] = (o * (g * jax.nn.sigmoid(g))).astype(BF16)
        return carry

    lax.fori_loop(0, nc, finish, 0)


def _hgrn(hgp, lbc, norm_w, s0):
    b, l, _ = hgp.shape
    col = lambda sec: pl.BlockSpec((1, l, HG_D), lambda i, h, sec=sec: (i, 0, sec * HG_HEADS + h))
    state_spec = pl.BlockSpec((1, 2, 1, HG_D, HG_D), lambda i, h: (i, 0, h, 0, 0))
    return pl.pallas_call(
        _hgrn_kernel,
        grid=(b, HG_HEADS),
        in_specs=[col(0), col(1), col(2), col(3), col(4),
                  pl.BlockSpec((SUBLANES, HG_D), lambda i, h: (0, h)),
                  pl.BlockSpec((1, HG_D), lambda i, h: (0, 0)),
                  state_spec],
        out_specs=[pl.BlockSpec((1, l, HG_D), lambda i, h: (i, 0, h)), state_spec],
        out_shape=[jax.ShapeDtypeStruct((b, l, HG_W), BF16),
                   jax.ShapeDtypeStruct((b, 2, HG_HEADS, HG_D, HG_D), F32)],
        scratch_shapes=[pltpu.VMEM((l, HG_D), F32),
                        pltpu.VMEM((HG_D, HG_D), F32),
                        pltpu.VMEM((HG_D, HG_D), F32)],
        compiler_params=_cparams(("parallel", "parallel")),
        name="hgrn",
    )(hgp, hgp, hgp, hgp, hgp, lbc, norm_w.reshape(1, HG_D), s0)


def _outproj_kernel(x_ref, att_ref, hg_ref, mod_ref, w_ref, o_ref):
    y = _dot(att_ref[0], w_ref[:NA_W, :]) + _dot(hg_ref[0], w_ref[NA_W:, :])
    o_ref[0] = x_ref[0] + mod_ref[0, 2:3, :] * y


def _outproj(x, att, hg, mod, w, tm=512):
    b, l, d = x.shape
    tm = min(tm, l)
    return pl.pallas_call(
        _outproj_kernel,
        grid=(b, l // tm),
        in_specs=[
            pl.BlockSpec((1, tm, d), lambda i, j: (i, j, 0)),
            pl.BlockSpec((1, tm, NA_W), lambda i, j: (i, j, 0)),
            pl.BlockSpec((1, tm, HG_W), lambda i, j: (i, j, 0)),
            pl.BlockSpec((1, SUBLANES, d), _mod_index(mod)),
            pl.BlockSpec(w.shape, lambda i, j: (0, 0)),
        ],
        out_specs=pl.BlockSpec((1, tm, d), lambda i, j: (i, j, 0)),
        out_shape=jax.ShapeDtypeStruct(x.shape, F32),
        compiler_params=_cparams(("parallel", "parallel")),
        name="outproj",
    )(x, att, hg, mod, w)


def _ffn_kernel(x_ref, nw_ref, mod_ref, w1_ref, w3_ref, w2_ref, o_ref, h_ref, acc_ref):
    f = pl.program_id(2)

    @pl.when(f == 0)
    def _():
        h = _norm_mod(x_ref[0], nw_ref[...], mod_ref[0, 3:4, :], mod_ref[0, 4:5, :])
        h_ref[...] = h.astype(BF16)
        acc_ref[...] = jnp.zeros_like(acc_ref)

    h = h_ref[...]
    a = _dot(h, w1_ref[...])
    g = (a * jax.nn.sigmoid(a)) * _dot(h, w3_ref[...])
    acc_ref[...] += _dot(g.astype(BF16), w2_ref[...])

    @pl.when(f == pl.num_programs(2) - 1)
    def _():
        o_ref[0] = x_ref[0] + mod_ref[0, 5:6, :] * acc_ref[...]


def _ffn(x, nw, mod, w1, w3, w2, tm=512, tf=1408):
    b, l, d = x.shape
    tm = min(tm, l)
    ff = w1.shape[1]
    tf = min(tf, ff)
    return pl.pallas_call(
        _ffn_kernel,
        grid=(b, l // tm, ff // tf),
        in_specs=[
            pl.BlockSpec((1, tm, d), lambda i, j, f: (i, j, 0)),
            pl.BlockSpec((1, d), lambda i, j, f: (0, 0)),
            pl.BlockSpec((1, SUBLANES, d), _mod_index(mod)),
            pl.BlockSpec((d, tf), lambda i, j, f: (0, f)),
            pl.BlockSpec((d, tf), lambda i, j, f: (0, f)),
            pl.BlockSpec((tf, d), lambda i, j, f: (f, 0)),
        ],
        out_specs=pl.BlockSpec((1, tm, d), lambda i, j, f: (i, j, 0)),
        out_shape=jax.ShapeDtypeStruct(x.shape, F32),
        scratch_shapes=[pltpu.VMEM((tm, d), BF16), pltpu.VMEM((tm, d), F32)],
        compiler_params=_cparams(("parallel", "parallel", "arbitrary")),
        name="ffn",
    )(x, nw.reshape(1, d), mod, w1, w3, w2)


def _router_kernel(x_ref, nw_ref, mod_ref, r_ref, g_ref):
    h = _norm_mod(x_ref[0], nw_ref[...], mod_ref[0, 3:4, :], mod_ref[0, 4:5, :])
    hh = h.astype(BF16)
    hl = (h - hh.astype(F32)).astype(BF16)
    r = r_ref[...]
    rh = r.astype(BF16)
    rl = (r - rh.astype(F32)).astype(BF16)
    logits = _dot(hh, rh) + _dot(hh, rl) + _dot(hl, rh)
    lane = lax.broadcasted_iota(jnp.int32, logits.shape, 1)
    logits = jnp.where(lane < N_EXPERTS, logits, -jnp.inf)
    m1 = jnp.max(logits, axis=-1, keepdims=True)
    i1 = jnp.min(jnp.where(logits == m1, lane, LANES), axis=-1, keepdims=True)
    rest = jnp.where(lane == i1, -jnp.inf, logits)
    m2 = jnp.max(rest, axis=-1, keepdims=True)
    i2 = jnp.min(jnp.where(rest == m2, lane, LANES), axis=-1, keepdims=True)
    e2 = jnp.exp(m2 - m1)
    den = 1.0 + e2
    g_ref[0] = jnp.where(lane == i1, 1.0 / den, 0.0) + jnp.where(lane == i2, e2 / den, 0.0)


def _router(x, nw, mod, router, tm=512):
    b, l, d = x.shape
    tm = min(tm, l)
    rp = jnp.zeros((d, LANES), F32).at[:, :router.shape[1]].set(router)
    return pl.pallas_call(
        _router_kernel,
        grid=(b, l // tm),
        in_specs=[
            pl.BlockSpec((1, tm, d), lambda i, j: (i, j, 0)),
            pl.BlockSpec((1, d), lambda i, j: (0, 0)),
            pl.BlockSpec((1, SUBLANES, d), _mod_index(mod)),
            pl.BlockSpec((d, LANES), lambda i, j: (0, 0)),
        ],
        out_specs=pl.BlockSpec((1, tm, LANES), lambda i, j: (i, j, 0)),
        out_shape=jax.ShapeDtypeStruct((b, l, LANES), F32),
        compiler_params=_cparams(("parallel", "parallel")),
        name="router",
    )(x, nw.reshape(1, d), mod, rp)


def _moe_kernel(x_ref, nw_ref, mod_ref, gate_ref, w1_ref, w3_ref, w2_ref, o_ref,
                h_ref, acc_ref, acc_e_ref):
    e = pl.program_id(2)
    f = pl.program_id(3)
    nf = pl.num_programs(3)

    @pl.when((e == 0) & (f == 0))
    def _():
        h = _norm_mod(x_ref[0], nw_ref[...], mod_ref[0, 3:4, :], mod_ref[0, 4:5, :])
        h_ref[...] = h.astype(BF16)
        acc_ref[...] = jnp.zeros_like(acc_ref)

    h = h_ref[...]
    a = _dot(h, w1_ref[0])
    g = (a * jax.nn.sigmoid(a)) * _dot(h, w3_ref[0])
    y = _dot(g.astype(BF16), w2_ref[0])

    @pl.when(f == 0)
    def _():
        acc_e_ref[...] = y

    @pl.when(f > 0)
    def _():
        acc_e_ref[...] += y

    @pl.when(f == nf - 1)
    def _():
        gates = gate_ref[0]
        lane = lax.broadcasted_iota(jnp.int32, gates.shape, 1)
        ge = jnp.sum(jnp.where(lane == e, gates, 0.0), axis=-1, keepdims=True)
        acc_ref[...] += ge * acc_e_ref[...]

    @pl.when((e == pl.num_programs(2) - 1) & (f == nf - 1))
    def _():
        o_ref[0] = x_ref[0] + mod_ref[0, 5:6, :] * acc_ref[...]


def _moe(x, nw, mod, gates, w1, w3, w2, tm=512, tf=1408):
    b, l, d = x.shape
    tm = min(tm, l)
    ne, _, ff = w1.shape
    tf = min(tf, ff)
    return pl.pallas_call(
        _moe_kernel,
        grid=(b, l // tm, ne, ff // tf),
        in_specs=[
            pl.BlockSpec((1, tm, d), lambda i, j, e, f: (i, j, 0)),
            pl.BlockSpec((1, d), lambda i, j, e, f: (0, 0)),
            pl.BlockSpec((1, SUBLANES, d), _mod_index(mod)),
            pl.BlockSpec((1, tm, LANES), lambda i, j, e, f: (i, j, 0)),
            pl.BlockSpec((1, d, tf), lambda i, j, e, f: (e, 0, f)),
            pl.BlockSpec((1, d, tf), lambda i, j, e, f: (e, 0, f)),
            pl.BlockSpec((1, tf, d), lambda i, j, e, f: (e, f, 0)),
        ],
        out_specs=pl.BlockSpec((1, tm, d), lambda i, j, e, f: (i, j, 0)),
        out_shape=jax.ShapeDtypeStruct(x.shape, F32),
        scratch_shapes=[pltpu.VMEM((tm, d), BF16), pltpu.VMEM((tm, d), F32),
                        pltpu.VMEM((tm, d), F32)],
        compiler_params=_cparams(("parallel", "parallel", "arbitrary", "arbitrary")),
        name="moe",
    )(x, nw.reshape(1, d), mod, gates, w1, w3, w2)


def _final_norm_kernel(x_ref, nw_ref, o_ref):
    x = x_ref[0]
    o_ref[0] = x * lax.rsqrt(jnp.mean(x * x, axis=-1, keepdims=True) + EPS) * nw_ref[...]


def _final_norm(x, nw, tm=512):
    b, l, d = x.shape
    tm = min(tm, l)
    return pl.pallas_call(
        _final_norm_kernel,
        grid=(b, l // tm),
        in_specs=[pl.BlockSpec((1, tm, d), lambda i, j: (i, j, 0)),
                  pl.BlockSpec((1, d), lambda i, j: (0, 0))],
        out_specs=pl.BlockSpec((1, tm, d), lambda i, j: (i, j, 0)),
        out_shape=jax.ShapeDtypeStruct(x.shape, F32),
        compiler_params=_cparams(("parallel", "parallel")),
        name="final_norm",
    )(x, nw.reshape(1, d))


def _pad_rows(a, rows):
    return jnp.zeros((rows,) + a.shape[1:], a.dtype).at[:a.shape[0]].set(a)


def kernel(x_prompt, x_sample, cache_k, cache_v, state_hgrn, c, c_ctx, norm1_w, norm2_w, w_ada, b_ada,
           w_in, rpb, hg_lower, hg_norm_w, w_out, ffn_w1, ffn_w3, ffn_w2, moe_router, moe_w1, moe_w3,
           moe_w2, final_norm_w):
    depth = w_in.shape[0]
    d = x_prompt.shape[-1]
    nb = x_prompt.shape[0]
    nd = x_sample.shape[0]
    rows = x_sample.shape[1] // GRID_W

    lbs = jnp.cumsum(jax.nn.softmax(hg_lower.astype(F32), axis=0), axis=0)
    lbs = lbs - lbs[0:1]
    lbc = _pad_rows(jnp.stack([jnp.log(lbs), jnp.log1p(-lbs), 1.0 - lbs], axis=1).transpose(1, 0, 2),
                    SUBLANES).transpose(1, 0, 2)

    cond = _pad_rows(jnp.concatenate([c, c_ctx[None, :]], axis=0), SUBLANES)
    mods = _modulation(cond, w_ada, b_ada)
    mods = _pad_rows(mods.reshape(depth, SUBLANES, 6, d).transpose(2, 0, 1, 3),
                     SUBLANES).transpose(1, 2, 0, 3)

    w_in_b = w_in.astype(BF16)
    w_out_b = w_out.astype(BF16)
    ffn_w1_b, ffn_w3_b, ffn_w2_b = ffn_w1.astype(BF16), ffn_w3.astype(BF16), ffn_w2.astype(BF16)
    moe_w1_b, moe_w3_b, moe_w2_b = moe_w1.astype(BF16), moe_w3.astype(BF16), moe_w2.astype(BF16)
    wkt_b = jnp.swapaxes(w_in[:, :, NA_W:2 * NA_W], 1, 2).astype(BF16)
    lc = cache_k.shape[2]
    ckt_b = jnp.swapaxes(cache_k.astype(BF16).reshape(nd, depth, lc, NA_W), 2, 3)
    cv_b = cache_v.astype(BF16).reshape(nd, depth, lc, NA_W)
    zero_state = jnp.zeros((nb, 2, HG_HEADS, HG_D, HG_D), F32)

    def layer(x, seq_shape, mod, l, attend, s0, wkt):
        tok = lambda a: a.reshape(x.shape[:2] + a.shape[-1:])
        seq = lambda a: a.reshape(seq_shape + a.shape[-1:])
        want_kv = wkt is None
        outs = _inproj(x, norm1_w[l], mod, w_in_b[l], wkt)
        att = attend(seq(outs[0]), outs[2])
        hg, s_fin = _hgrn(seq(outs[1]), lbc[l], hg_norm_w[l], s0)
        x = _outproj(x, tok(att), tok(hg), mod, w_out_b[l])
        i = l // 2
        if l % 2 == 0:
            x = _ffn(x, norm2_w[l], mod, ffn_w1_b[i], ffn_w3_b[i], ffn_w2_b[i])
        else:
            gates = _router(x, norm2_w[l], mod, moe_router[i])
            x = _moe(x, norm2_w[l], mod, gates, moe_w1_b[i], moe_w3_b[i], moe_w2_b[i])
        return x, (seq(outs[2]) if want_kv else None), s_fin

    seq = x_prompt.shape[1]
    xp, xs = x_prompt.reshape(1, nb * seq, d), x_sample
    ks_out, vs_out, ss_out = [], [], []
    for l in range(depth):
        mod_p = mods[l, nd:nd + 1]
        mod_s = mods[l, :nd]
        xp, kv, s_fin = layer(xp, (nb, seq), mod_p, l, lambda qkv, _: _ctx_attention(qkv),
                              zero_state, None)
        ks_out.append(kv[..., :NA_W])
        vs_out.append(kv[..., NA_W:])
        ss_out.append(s_fin)
        bias = _na_bias_slabs(rpb[l], rows)
        attend_lat = lambda qkv, kt, l=l, bias=bias: _na_attention(qkv, kt, ckt_b[:, l], cv_b[:, l], bias)
        xs, _, _ = layer(xs, x_sample.shape[:2], mod_s, l, attend_lat,
                         state_hgrn[:, l].astype(F32), wkt_b[l])

    y_prompt = _final_norm(xp, final_norm_w).reshape(x_prompt.shape)
    y_sample = _final_norm(xs, final_norm_w)
    new_cache_k = jnp.stack(ks_out, axis=1).reshape(nb, depth, seq, NA_HEADS, NA_DH)
    new_cache_v = jnp.stack(vs_out, axis=1).reshape(nb, depth, seq, NA_HEADS, NA_DH)
    new_state = jnp.stack(ss_out, axis=1).astype(x_prompt.dtype)
    return (y_prompt, y_sample, new_cache_k, new_cache_v, new_state)
```

```python
import functools

import numpy as np
import jax
import jax.numpy as jnp
from jax import lax
from jax.experimental import pallas as pl
from jax.experimental.pallas import tpu as pltpu

F32 = jnp.float32
BF16 = jnp.bfloat16

EPS = 1e-6
NA_HEADS = 8
NA_DH = 64
NA_W = NA_HEADS * NA_DH
GRID_W = 64
WIN_R = 8
WIN_C = 16
HG_HEADS = 4
HG_D = 128
HG_W = HG_HEADS * HG_D
N_EXPERTS = 8
LANES = 128
SUBLANES = 8
VMEM_LIMIT = 56 * 1024 * 1024

HG_CHUNK = 128
HG_SUB = 16
HG_FLAT = 32
HG_FLAT_RANGE = 60.0


def _cparams(sem):
    return pltpu.CompilerParams(dimension_semantics=sem, vmem_limit_bytes=VMEM_LIMIT)


def _dot(a, b):
    return jnp.dot(a, b, preferred_element_type=F32)


def _dot_nt(a, b):
    return lax.dot_general(a, b, (((1,), (1,)), ((), ())), preferred_element_type=F32)


def _dot_tn(a, b):
    return lax.dot_general(a, b, (((0,), (0,)), ((), ())), preferred_element_type=F32)


def _norm_mod(x, nw, shift, scale):
    ms = jnp.mean(x * x, axis=-1, keepdims=True)
    y = x * lax.rsqrt(ms + EPS) * nw
    return y * (1.0 + scale) + shift


def _mod_kernel(cond_ref, w_ref, b_ref, o_ref):
    c = cond_ref[...]
    s = c * jax.nn.sigmoid(c)
    o_ref[0] = _dot(s.astype(BF16), w_ref[0].astype(BF16)) + b_ref[0]


def _modulation(cond, w_ada, b_ada, tn=1536):
    depth, d, n = w_ada.shape
    rows = cond.shape[0]
    return pl.pallas_call(
        _mod_kernel,
        grid=(depth, n // tn),
        in_specs=[
            pl.BlockSpec((rows, d), lambda l, j: (0, 0)),
            pl.BlockSpec((1, d, tn), lambda l, j: (l, 0, j)),
            pl.BlockSpec((1, 1, tn), lambda l, j: (l, 0, j)),
        ],
        out_specs=pl.BlockSpec((1, rows, tn), lambda l, j: (l, 0, j)),
        out_shape=jax.ShapeDtypeStruct((depth, rows, n), F32),
        compiler_params=_cparams(("parallel", "parallel")),
        name="modulation",
    )(cond, w_ada, b_ada.reshape(depth, 1, n))


def _mod_index(mod):
    if mod.shape[0] == 1:
        return lambda b, *_: (0, 0, 0)
    return lambda b, *_: (b, 0, 0)


def _inproj_kernel(x_ref, nw_ref, mod_ref, w_ref, *refs, want_kt):
    if want_kt:
        wkt_ref, qkv_ref, hg_ref, extra_ref = refs
    else:
        qkv_ref, hg_ref, extra_ref = refs
    h = _norm_mod(x_ref[0], nw_ref[...], mod_ref[0, 0:1, :], mod_ref[0, 1:2, :]).astype(BF16)
    p = _dot(h, w_ref[...])
    qkv_ref[0] = p[:, :3 * NA_W].astype(BF16)
    hg_ref[0] = p[:, 3 * NA_W:]
    if want_kt:
        extra_ref[0] = _dot_nt(wkt_ref[...], h).astype(BF16)
    else:
        extra_ref[0] = p[:, NA_W:3 * NA_W]


def _inproj(x, nw, mod, w, wkt=None, tm=256):
    b, l, d = x.shape
    n = w.shape[1]
    want_kt = wkt is not None
    out_shape = [jax.ShapeDtypeStruct((b, l, 3 * NA_W), BF16),
                 jax.ShapeDtypeStruct((b, l, n - 3 * NA_W), F32)]
    out_specs = [pl.BlockSpec((1, tm, 3 * NA_W), lambda i, j: (i, j, 0)),
                 pl.BlockSpec((1, tm, n - 3 * NA_W), lambda i, j: (i, j, 0))]
    in_specs = [pl.BlockSpec((1, tm, d), lambda i, j: (i, j, 0)),
                pl.BlockSpec((1, d), lambda i, j: (0, 0)),
                pl.BlockSpec((1, SUBLANES, d), _mod_index(mod)),
                pl.BlockSpec((d, n), lambda i, j: (0, 0))]
    args = [x, nw.reshape(1, d), mod, w]
    if want_kt:
        in_specs.append(pl.BlockSpec((NA_W, d), lambda i, j: (0, 0)))
        args.append(wkt)
        out_shape.append(jax.ShapeDtypeStruct((b, NA_W, l), BF16))
        out_specs.append(pl.BlockSpec((1, NA_W, tm), lambda i, j: (i, 0, j)))
    else:
        out_shape.append(jax.ShapeDtypeStruct((b, l, 2 * NA_W), F32))
        out_specs.append(pl.BlockSpec((1, tm, 2 * NA_W), lambda i, j: (i, j, 0)))
    return pl.pallas_call(
        functools.partial(_inproj_kernel, want_kt=want_kt),
        grid=(b, l // tm),
        in_specs=in_specs,
        out_specs=out_specs,
        out_shape=out_shape,
        compiler_params=_cparams(("parallel", "parallel")),
        name="inproj",
    )(*args)


def _ctx_attn_kernel(q_ref, k_ref, v_ref, o_ref):
    scale = NA_DH ** -0.5
    for h in range(NA_HEADS):
        sl = slice(h * NA_DH, (h + 1) * NA_DH)
        q = q_ref[0, :, sl] * scale
        s = _dot_nt(q, k_ref[0, :, sl])
        m = jnp.max(s, axis=-1, keepdims=True)
        p = jnp.exp(s - m)
        den = jnp.sum(p, axis=-1, keepdims=True)
        o = _dot(p.astype(BF16), v_ref[0, :, sl]) / den
        o_ref[0, :, sl] = o.astype(BF16)


def _ctx_attention(qkv):
    b, l, _ = qkv.shape
    spec = lambda c: pl.BlockSpec((1, l, NA_W), lambda i, c=c: (i, 0, c))
    return pl.pallas_call(
        _ctx_attn_kernel,
        grid=(b,),
        in_specs=[spec(0), spec(1), spec(2)],
        out_specs=pl.BlockSpec((1, l, NA_W), lambda i: (i, 0, 0)),
        out_shape=jax.ShapeDtypeStruct((b, l, NA_W), BF16),
        compiler_params=_cparams(("parallel",)),
        name="ctx_attention",
    )(qkv, qkv, qkv)


NA_QROWS = 2
NA_KROWS = WIN_R + NA_QROWS


def _na_window_start(p, rows):
    return jnp.clip(NA_QROWS * p - WIN_R // 2, 0, rows - NA_KROWS)


def _na_kernel(q_ref, kt_ref, v_ref, ckt_ref, cv_ref, bias_ref, o_ref, s_ref, *, rows):
    ws = _na_window_start(pl.program_id(1), rows)
    start = pl.multiple_of(ws * GRID_W, NA_QROWS * GRID_W)
    n_loc = NA_KROWS * GRID_W
    scale = NA_DH ** -0.5
    maxes = []
    for h in range(NA_HEADS):
        sl = slice(h * NA_DH, (h + 1) * NA_DH)
        q = q_ref[0, :, sl] * scale
        s_loc = _dot(q, kt_ref[0, sl, pl.ds(start, n_loc)]) + bias_ref[0, h]
        s_ctx = _dot(q, ckt_ref[0, sl, :])
        s_ref[h, :, :n_loc] = s_loc
        s_ref[h, :, n_loc:] = s_ctx
        maxes.append(jnp.maximum(jnp.max(s_loc, axis=-1, keepdims=True),
                                 jnp.max(s_ctx, axis=-1, keepdims=True)))
    for h in range(NA_HEADS):
        sl = slice(h * NA_DH, (h + 1) * NA_DH)
        p_loc = jnp.exp(s_ref[h, :, :n_loc] - maxes[h])
        p_ctx = jnp.exp(s_ref[h, :, n_loc:] - maxes[h])
        den = jnp.sum(p_loc, axis=-1, keepdims=True) + jnp.sum(p_ctx, axis=-1, keepdims=True)
        o = (_dot(p_loc.astype(BF16), v_ref[0, pl.ds(start, n_loc), sl])
             + _dot(p_ctx.astype(BF16), cv_ref[0, :, sl])) / den
        o_ref[0, :, sl] = o.astype(BF16)


NA_EDGE = 2


def _na_variant(p, n_pairs):
    return jnp.where(p < NA_EDGE, p, jnp.where(p >= n_pairs - NA_EDGE, p - (n_pairs - 2 * NA_EDGE - 1), NA_EDGE))


def _na_bias_slabs(rpb, rows):
    n_pairs = rows // NA_QROWS
    ps = np.array([0, 1, 2, n_pairs - 2, n_pairs - 1])
    r = NA_QROWS * ps[:, None] + np.arange(NA_QROWS)[None, :]
    rs = np.clip(r - WIN_R // 2, 0, rows - WIN_R)
    ws = np.clip(NA_QROWS * ps - WIN_R // 2, 0, rows - NA_KROWS)
    krow = ws[:, None] + np.arange(NA_KROWS)[None, :]
    row_ok = (krow[:, None, :] >= rs[:, :, None]) & (krow[:, None, :] < rs[:, :, None] + WIN_R)
    dr = np.clip(krow[:, None, :] - r[:, :, None] + (WIN_R - 1), 0, 2 * WIN_R - 2)
    j = np.arange(GRID_W)[:, None]
    kc = np.arange(GRID_W)[None, :]
    cs = np.clip(j - WIN_C // 2, 0, GRID_W - WIN_C)
    col_ok = (kc >= cs) & (kc < cs + WIN_C)
    dc = np.clip(kc - j + (WIN_C - 1), 0, 2 * WIN_C - 2)
    planes = jnp.where(jnp.asarray(col_ok)[None, None], rpb.astype(F32)[:, :, dc], -jnp.inf)
    tab = jnp.take(planes, jnp.asarray(dr.reshape(-1)), axis=1)
    tab = tab.reshape((NA_HEADS,) + dr.shape + (GRID_W, GRID_W))
    tab = jnp.where(jnp.asarray(row_ok)[None, :, :, :, None, None], tab, -jnp.inf)
    tab = jnp.transpose(tab, (1, 0, 2, 4, 3, 5))
    return tab.reshape(len(ps), NA_HEADS, NA_QROWS * GRID_W, NA_KROWS * GRID_W)


def _na_attention(qkv, kt, ckt, cv, bias):
    b, l, _ = qkv.shape
    rows = l // GRID_W
    n_pairs = rows // NA_QROWS
    lc = cv.shape[1]
    tq = NA_QROWS * GRID_W
    n_loc = NA_KROWS * GRID_W
    return pl.pallas_call(
        functools.partial(_na_kernel, rows=rows),
        grid=(b, n_pairs),
        in_specs=[
            pl.BlockSpec((1, tq, NA_W), lambda i, p: (i, p, 0)),
            pl.BlockSpec((1, NA_W, l), lambda i, p: (i, 0, 0)),
            pl.BlockSpec((1, l, NA_W), lambda i, p: (i, 0, 2)),
            pl.BlockSpec((1, NA_W, lc), lambda i, p: (i, 0, 0)),
            pl.BlockSpec((1, lc, NA_W), lambda i, p: (i, 0, 0)),
            pl.BlockSpec((1, NA_HEADS, tq, n_loc), lambda i, p: (_na_variant(p, n_pairs), 0, 0, 0)),
        ],
        out_specs=pl.BlockSpec((1, tq, NA_W), lambda i, p: (i, p, 0)),
        out_shape=jax.ShapeDtypeStruct((b, l, NA_W), BF16),
        scratch_shapes=[pltpu.VMEM((NA_HEADS, tq, n_loc + lc), F32)],
        compiler_params=_cparams(("parallel", "arbitrary")),
        name="na_attention",
    )(qkv, kt, qkv, ckt, cv, bias)


def _split3(x):
    hi = x.astype(BF16)
    r1 = x - hi.astype(F32)
    mid = r1.astype(BF16)
    lo = (r1 - mid.astype(F32)).astype(BF16)
    return hi, mid, lo


def _block_ref(cum, block, ref_row):
    c = cum.shape[0]
    ref = cum.reshape(c // block, block, HG_D)[:, ref_row:ref_row + 1, :]
    return jnp.broadcast_to(ref, (c // block, block, HG_D)).reshape(c, HG_D)


def _hgrn_level(cum, qs, k, m, reverse, ti, si):
    c = cum.shape[0]
    ref = _block_ref(cum, 2 * m, m if reverse else m - 1)
    row = lax.broadcasted_iota(jnp.int32, (c, 1), 0)
    upper = jnp.bitwise_and(row, m) != 0
    q_role = jnp.logical_not(upper) if reverse else upper
    dec = jnp.exp(jnp.where(q_role, cum - ref, ref - cum))
    qm = jnp.where(q_role, qs * dec, 0.0).astype(BF16)
    km = jnp.where(q_role, 0.0, k * dec).astype(BF16)
    am = _dot_nt(qm, km)
    if 2 * m < c:
        am = jnp.where(jnp.bitwise_xor(ti, si) < 2 * m, am, 0.0)
    return am


def _hgrn_pairwise(cum, qs, k, v, reverse):
    c = cum.shape[0]
    sub_row = lax.broadcasted_iota(jnp.int32, (SUBLANES, 1), 0)
    tiles_per_sub = HG_SUB // SUBLANES
    o_tiles = []
    for blk in range(c // HG_SUB):
        accs = [jnp.zeros((SUBLANES, HG_D), F32) for _ in range(tiles_per_sub)]
        for j in range(HG_SUB):
            s = blk * HG_SUB + j
            cs, ks, vs = cum[s:s + 1, :], k[s:s + 1, :], v[s:s + 1, :]
            for t in range(tiles_per_sub):
                lo_r, hi_r = t * SUBLANES, t * SUBLANES + SUBLANES - 1
                if (lo_r > j) if reverse else (hi_r < j):
                    continue
                base = blk * HG_SUB + lo_r
                x = qs[base:base + SUBLANES, :] * ks * jnp.exp(
                    jnp.minimum(cum[base:base + SUBLANES, :] - cs, 0.0))
                a = jnp.sum(x, axis=-1, keepdims=True)
                if reverse and hi_r > j:
                    a = jnp.where(sub_row + lo_r <= j, a, 0.0)
                elif (not reverse) and lo_r < j:
                    a = jnp.where(sub_row + lo_r >= j, a, 0.0)
                accs[t] = accs[t] + a * vs
        o_tiles.extend(accs)
    return jnp.concatenate(o_tiles, axis=0)


def _hgrn_chunk(z, qs, v, st, consts, reverse):
    c = z.shape[0]
    loglb, log1mlb, oml = consts
    e = jnp.exp(-jnp.abs(z))
    logsig = jnp.minimum(z, 0.0) - jnp.log1p(e)
    bb = log1mlb + logsig
    logf = jnp.maximum(loglb, bb) + jnp.log1p(jnp.exp(-jnp.abs(loglb - bb)))
    k = oml * (jnp.where(z >= 0.0, e, 1.0) / (1.0 + e))

    ti = lax.broadcasted_iota(jnp.int32, (c, c), 0)
    si = lax.broadcasted_iota(jnp.int32, (c, c), 1)
    tri = jnp.where((si >= ti) if reverse else (si <= ti), 1.0, 0.0).astype(BF16)
    hi, mid, lo = _split3(logf)
    cum = _dot(tri, hi) + _dot(tri, mid) + _dot(tri, lo)

    a_off = jnp.zeros((c, c), F32)
    m = c // 2
    while m >= HG_FLAT:
        a_off = a_off + _hgrn_level(cum, qs, k, m, reverse, ti, si)
        m //= 2

    dq = cum - _block_ref(cum, HG_FLAT, HG_FLAT // 2 if reverse else HG_FLAT // 2 - 1)

    def flat(_):
        qf = (qs * jnp.exp(dq)).astype(BF16)
        kf = (k * jnp.exp(-dq)).astype(BF16)
        keep = (jnp.bitwise_xor(ti, si) < HG_FLAT) & ((si >= ti) if reverse else (si <= ti))
        return a_off + jnp.where(keep, _dot_nt(qf, kf), 0.0), jnp.zeros((c, HG_D), F32)

    def fine(_):
        a = a_off
        mm = HG_FLAT // 2
        while mm >= HG_SUB:
            a = a + _hgrn_level(cum, qs, k, mm, reverse, ti, si)
            mm //= 2
        return a, _hgrn_pairwise(cum, qs, k, v, reverse)

    a_in, o = lax.cond(jnp.max(jnp.abs(dq)) <= HG_FLAT_RANGE, flat, fine, 0)

    edge = cum[0:1, :] if reverse else cum[c - 1:c, :]
    o = o + _dot(a_in.astype(BF16), v.astype(BF16))
    o = o + _dot_nt((qs * jnp.exp(cum)).astype(BF16), st.astype(BF16))
    kd = (k * jnp.exp(edge - cum)).astype(BF16)
    st_new = st * jnp.exp(edge) + _dot_tn(v.astype(BF16), kd)
    return o, st_new


def _hgrn_kernel(q_ref, i_ref, zf_ref, zb_ref, g_ref, lbc_ref, nw_ref, s0_ref,
                 o_ref, s_ref, acc_ref, stf_ref, stb_ref):
    l = q_ref.shape[1]
    c = HG_CHUNK
    nc = l // c
    consts = (lbc_ref[0:1, :], lbc_ref[1:2, :], lbc_ref[2:3, :])
    stf_ref[...] = s0_ref[0, 0, 0].T
    stb_ref[...] = s0_ref[0, 1, 0].T
    acc_ref[...] = jnp.zeros_like(acc_ref)

    def body(i, carry):
        for reverse, z_ref, st_ref in ((False, zf_ref, stf_ref), (True, zb_ref, stb_ref)):
            ci = (nc - 1 - i) if reverse else i
            rows = pl.ds(pl.multiple_of(ci * c, c), c)
            q = q_ref[0, rows, :]
            qs = q * jax.nn.sigmoid(q)
            o, st_new = _hgrn_chunk(z_ref[0, rows, :], qs, i_ref[0, rows, :], st_ref[...],
                                    consts, reverse)
            st_ref[...] = st_new
            acc_ref[rows, :] += o
        return carry

    lax.fori_loop(0, nc, body, 0)
    s_ref[0, 0, 0] = stf_ref[...].T
    s_ref[0, 1, 0] = stb_ref[...].T

    def finish(i, carry):
        rows = pl.ds(pl.multiple_of(i * c, c), c)
        o = acc_ref[rows, :]
        o = o * lax.rsqrt(jnp.mean(o * o, axis=-1, keepdims=True) + EPS) * nw_ref[...]
        g = g_ref[0, rows, :]
        o_ref[0, rows, :] = (o * (g * jax.nn.sigmoid(g))).astype(BF16)
        return carry

    lax.fori_loop(0, nc, finish, 0)


def _hgrn(hgp, lbc, norm_w, s0):
    b, l, _ = hgp.shape
    col = lambda sec: pl.BlockSpec((1, l, HG_D), lambda i, h, sec=sec: (i, 0, sec * HG_HEADS + h))
    state_spec = pl.BlockSpec((1, 2, 1, HG_D, HG_D), lambda i, h: (i, 0, h, 0, 0))
    return pl.pallas_call(
        _hgrn_kernel,
        grid=(b, HG_HEADS),
        in_specs=[col(0), col(1), col(2), col(3), col(4),
                  pl.BlockSpec((SUBLANES, HG_D), lambda i, h: (0, h)),
                  pl.BlockSpec((1, HG_D), lambda i, h: (0, 0)),
                  state_spec],
        out_specs=[pl.BlockSpec((1, l, HG_D), lambda i, h: (i, 0, h)), state_spec],
        out_shape=[jax.ShapeDtypeStruct((b, l, HG_W), BF16),
                   jax.ShapeDtypeStruct((b, 2, HG_HEADS, HG_D, HG_D), F32)],
        scratch_shapes=[pltpu.VMEM((l, HG_D), F32),
                        pltpu.VMEM((HG_D, HG_D), F32),
                        pltpu.VMEM((HG_D, HG_D), F32)],
        compiler_params=_cparams(("parallel", "parallel")),
        name="hgrn",
    )(hgp, hgp, hgp, hgp, hgp, lbc, norm_w.reshape(1, HG_D), s0)


def _outproj_kernel(x_ref, att_ref, hg_ref, mod_ref, w_ref, o_ref):
    y = _dot(att_ref[0], w_ref[:NA_W, :]) + _dot(hg_ref[0], w_ref[NA_W:, :])
    o_ref[0] = x_ref[0] + mod_ref[0, 2:3, :] * y


def _outproj(x, att, hg, mod, w, tm=512):
    b, l, d = x.shape
    tm = min(tm, l)
    return pl.pallas_call(
        _outproj_kernel,
        grid=(b, l // tm),
        in_specs=[
            pl.BlockSpec((1, tm, d), lambda i, j: (i, j, 0)),
            pl.BlockSpec((1, tm, NA_W), lambda i, j: (i, j, 0)),
            pl.BlockSpec((1, tm, HG_W), lambda i, j: (i, j, 0)),
            pl.BlockSpec((1, SUBLANES, d), _mod_index(mod)),
            pl.BlockSpec(w.shape, lambda i, j: (0, 0)),
        ],
        out_specs=pl.BlockSpec((1, tm, d), lambda i, j: (i, j, 0)),
        out_shape=jax.ShapeDtypeStruct(x.shape, F32),
        compiler_params=_cparams(("parallel", "parallel")),
        name="outproj",
    )(x, att, hg, mod, w)


def _ffn_kernel(x_ref, nw_ref, mod_ref, w1_ref, w3_ref, w2_ref, o_ref, h_ref, acc_ref):
    f = pl.program_id(2)

    @pl.when(f == 0)
    def _():
        h = _norm_mod(x_ref[0], nw_ref[...], mod_ref[0, 3:4, :], mod_ref[0, 4:5, :])
        h_ref[...] = h.astype(BF16)
        acc_ref[...] = jnp.zeros_like(acc_ref)

    h = h_ref[...]
    a = _dot(h, w1_ref[...])
    g = (a * jax.nn.sigmoid(a)) * _dot(h, w3_ref[...])
    acc_ref[...] += _dot(g.astype(BF16), w2_ref[...])

    @pl.when(f == pl.num_programs(2) - 1)
    def _():
        o_ref[0] = x_ref[0] + mod_ref[0, 5:6, :] * acc_ref[...]


def _ffn(x, nw, mod, w1, w3, w2, tm=512, tf=1408):
    b, l, d = x.shape
    tm = min(tm, l)
    ff = w1.shape[1]
    tf = min(tf, ff)
    return pl.pallas_call(
        _ffn_kernel,
        grid=(b, l // tm, ff // tf),
        in_specs=[
            pl.BlockSpec((1, tm, d), lambda i, j, f: (i, j, 0)),
            pl.BlockSpec((1, d), lambda i, j, f: (0, 0)),
            pl.BlockSpec((1, SUBLANES, d), _mod_index(mod)),
            pl.BlockSpec((d, tf), lambda i, j, f: (0, f)),
            pl.BlockSpec((d, tf), lambda i, j, f: (0, f)),
            pl.BlockSpec((tf, d), lambda i, j, f: (f, 0)),
        ],
        out_specs=pl.BlockSpec((1, tm, d), lambda i, j, f: (i, j, 0)),
        out_shape=jax.ShapeDtypeStruct(x.shape, F32),
        scratch_shapes=[pltpu.VMEM((tm, d), BF16), pltpu.VMEM((tm, d), F32)],
        compiler_params=_cparams(("parallel", "parallel", "arbitrary")),
        name="ffn",
    )(x, nw.reshape(1, d), mod, w1, w3, w2)


def _router_kernel(x_ref, nw_ref, mod_ref, r_ref, g_ref):
    h = _norm_mod(x_ref[0], nw_ref[...], mod_ref[0, 3:4, :], mod_ref[0, 4:5, :])
    hh = h.astype(BF16)
    hl = (h - hh.astype(F32)).astype(BF16)
    r = r_ref[...]
    rh = r.astype(BF16)
    rl = (r - rh.astype(F32)).astype(BF16)
    logits = _dot(hh, rh) + _dot(hh, rl) + _dot(hl, rh)
    lane = lax.broadcasted_iota(jnp.int32, logits.shape, 1)
    logits = jnp.where(lane < N_EXPERTS, logits, -jnp.inf)
    m1 = jnp.max(logits, axis=-1, keepdims=True)
    i1 = jnp.min(jnp.where(logits == m1, lane, LANES), axis=-1, keepdims=True)
    rest = jnp.where(lane == i1, -jnp.inf, logits)
    m2 = jnp.max(rest, axis=-1, keepdims=True)
    i2 = jnp.min(jnp.where(rest == m2, lane, LANES), axis=-1, keepdims=True)
    e2 = jnp.exp(m2 - m1)
    den = 1.0 + e2
    g_ref[0] = jnp.where(lane == i1, 1.0 / den, 0.0) + jnp.where(lane == i2, e2 / den, 0.0)


def _router(x, nw, mod, router, tm=512):
    b, l, d = x.shape
    tm = min(tm, l)
    rp = jnp.zeros((d, LANES), F32).at[:, :router.shape[1]].set(router)
    return pl.pallas_call(
        _router_kernel,
        grid=(b, l // tm),
        in_specs=[
            pl.BlockSpec((1, tm, d), lambda i, j: (i, j, 0)),
            pl.BlockSpec((1, d), lambda i, j: (0, 0)),
            pl.BlockSpec((1, SUBLANES, d), _mod_index(mod)),
            pl.BlockSpec((d, LANES), lambda i, j: (0, 0)),
        ],
        out_specs=pl.BlockSpec((1, tm, LANES), lambda i, j: (i, j, 0)),
        out_shape=jax.ShapeDtypeStruct((b, l, LANES), F32),
        compiler_params=_cparams(("parallel", "parallel")),
        name="router",
    )(x, nw.reshape(1, d), mod, rp)


def _moe_kernel(x_ref, nw_ref, mod_ref, gate_ref, w1_ref, w3_ref, w2_ref, o_ref,
                h_ref, acc_ref, acc_e_ref):
    e = pl.program_id(2)
    f = pl.program_id(3)
    nf = pl.num_programs(3)

    @pl.when((e == 0) & (f == 0))
    def _():
        h = _norm_mod(x_ref[0], nw_ref[...], mod_ref[0, 3:4, :], mod_ref[0, 4:5, :])
        h_ref[...] = h.astype(BF16)
        acc_ref[...] = jnp.zeros_like(acc_ref)

    h = h_ref[...]
    a = _dot(h, w1_ref[0])
    g = (a * jax.nn.sigmoid(a)) * _dot(h, w3_ref[0])
    y = _dot(g.astype(BF16), w2_ref[0])

    @pl.when(f == 0)
    def _():
        acc_e_ref[...] = y

    @pl.when(f > 0)
    def _():
        acc_e_ref[...] += y

    @pl.when(f == nf - 1)
    def _():
        gates = gate_ref[0]
        lane = lax.broadcasted_iota(jnp.int32, gates.shape, 1)
        ge = jnp.sum(jnp.where(lane == e, gates, 0.0), axis=-1, keepdims=True)
        acc_ref[...] += ge * acc_e_ref[...]

    @pl.when((e == pl.num_programs(2) - 1) & (f == nf - 1))
    def _():
        o_ref[0] = x_ref[0] + mod_ref[0, 5:6, :] * acc_ref[...]


def _moe(x, nw, mod, gates, w1, w3, w2, tm=512, tf=1408):
    b, l, d = x.shape
    tm = min(tm, l)
    ne, _, ff = w1.shape
    tf = min(tf, ff)
    return pl.pallas_call(
        _moe_kernel,
        grid=(b, l // tm, ne, ff // tf),
        in_specs=[
            pl.BlockSpec((1, tm, d), lambda i, j, e, f: (i, j, 0)),
            pl.BlockSpec((1, d), lambda i, j, e, f: (0, 0)),
            pl.BlockSpec((1, SUBLANES, d), _mod_index(mod)),
            pl.BlockSpec((1, tm, LANES), lambda i, j, e, f: (i, j, 0)),
            pl.BlockSpec((1, d, tf), lambda i, j, e, f: (e, 0, f)),
            pl.BlockSpec((1, d, tf), lambda i, j, e, f: (e, 0, f)),
            pl.BlockSpec((1, tf, d), lambda i, j, e, f: (e, f, 0)),
        ],
        out_specs=pl.BlockSpec((1, tm, d), lambda i, j, e, f: (i, j, 0)),
        out_shape=jax.ShapeDtypeStruct(x.shape, F32),
        scratch_shapes=[pltpu.VMEM((tm, d), BF16), pltpu.VMEM((tm, d), F32),
                        pltpu.VMEM((tm, d), F32)],
        compiler_params=_cparams(("parallel", "parallel", "arbitrary", "arbitrary")),
        name="moe",
    )(x, nw.reshape(1, d), mod, gates, w1, w3, w2)


def _final_norm_kernel(x_ref, nw_ref, o_ref):
    x = x_ref[0]
    o_ref[0] = x * lax.rsqrt(jnp.mean(x * x, axis=-1, keepdims=True) + EPS) * nw_ref[...]


def _final_norm(x, nw, tm=512):
    b, l, d = x.shape
    tm = min(tm, l)
    return pl.pallas_call(
        _final_norm_kernel,
        grid=(b, l // tm),
        in_specs=[pl.BlockSpec((1, tm, d), lambda i, j: (i, j, 0)),
                  pl.BlockSpec((1, d), lambda i, j: (0, 0))],
        out_specs=pl.BlockSpec((1, tm, d), lambda i, j: (i, j, 0)),
        out_shape=jax.ShapeDtypeStruct(x.shape, F32),
        compiler_params=_cparams(("parallel", "parallel")),
        name="final_norm",
    )(x, nw.reshape(1, d))


def _pad_rows(a, rows):
    return jnp.zeros((rows,) + a.shape[1:], a.dtype).at[:a.shape[0]].set(a)


def kernel(x_prompt, x_sample, cache_k, cache_v, state_hgrn, c, c_ctx, norm1_w, norm2_w, w_ada, b_ada,
           w_in, rpb, hg_lower, hg_norm_w, w_out, ffn_w1, ffn_w3, ffn_w2, moe_router, moe_w1, moe_w3,
           moe_w2, final_norm_w):
    depth = w_in.shape[0]
    d = x_prompt.shape[-1]
    nb = x_prompt.shape[0]
    nd = x_sample.shape[0]
    rows = x_sample.shape[1] // GRID_W

    lbs = jnp.cumsum(jax.nn.softmax(hg_lower.astype(F32), axis=0), axis=0)
    lbs = lbs - lbs[0:1]
    lbc = _pad_rows(jnp.stack([jnp.log(lbs), jnp.log1p(-lbs), 1.0 - lbs], axis=1).transpose(1, 0, 2),
                    SUBLANES).transpose(1, 0, 2)

    cond = _pad_rows(jnp.concatenate([c, c_ctx[None, :]], axis=0), SUBLANES)
    mods = _modulation(cond, w_ada, b_ada)
    mods = _pad_rows(mods.reshape(depth, SUBLANES, 6, d).transpose(2, 0, 1, 3),
                     SUBLANES).transpose(1, 2, 0, 3)

    w_in_b = w_in.astype(BF16)
    w_out_b = w_out.astype(BF16)
    ffn_w1_b, ffn_w3_b, ffn_w2_b = ffn_w1.astype(BF16), ffn_w3.astype(BF16), ffn_w2.astype(BF16)
    moe_w1_b, moe_w3_b, moe_w2_b = moe_w1.astype(BF16), moe_w3.astype(BF16), moe_w2.astype(BF16)
    wkt_b = jnp.swapaxes(w_in[:, :, NA_W:2 * NA_W], 1, 2).astype(BF16)
    lc = cache_k.shape[2]
    ckt_b = jnp.swapaxes(cache_k.astype(BF16).reshape(nd, depth, lc, NA_W), 2, 3)
    cv_b = cache_v.astype(BF16).reshape(nd, depth, lc, NA_W)
    zero_state = jnp.zeros((nb, 2, HG_HEADS, HG_D, HG_D), F32)

    def layer(x, seq_shape, mod, l, attend, s0, wkt):
        tok = lambda a: a.reshape(x.shape[:2] + a.shape[-1:])
        seq = lambda a: a.reshape(seq_shape + a.shape[-1:])
        want_kv = wkt is None
        outs = _inproj(x, norm1_w[l], mod, w_in_b[l], wkt)
        att = attend(seq(outs[0]), outs[2])
        hg, s_fin = _hgrn(seq(outs[1]), lbc[l], hg_norm_w[l], s0)
        x = _outproj(x, tok(att), tok(hg), mod, w_out_b[l])
        i = l // 2
        if l % 2 == 0:
            x = _ffn(x, norm2_w[l], mod, ffn_w1_b[i], ffn_w3_b[i], ffn_w2_b[i])
        else:
            gates = _router(x, norm2_w[l], mod, moe_router[i])
            x = _moe(x, norm2_w[l], mod, gates, moe_w1_b[i], moe_w3_b[i], moe_w2_b[i])
        return x, (seq(outs[2]) if want_kv else None), s_fin

    seq = x_prompt.shape[1]
    xp, xs = x_prompt.reshape(1, nb * seq, d), x_sample
    ks_out, vs_out, ss_out = [], [], []
    for l in range(depth):
        mod_p = mods[l, nd:nd + 1]
        mod_s = mods[l, :nd]
        xp, kv, s_fin = layer(xp, (nb, seq), mod_p, l, lambda qkv, _: _ctx_attention(qkv),
                              zero_state, None)
        ks_out.append(kv[..., :NA_W])
        vs_out.append(kv[..., NA_W:])
        ss_out.append(s_fin)
        bias = _na_bias_slabs(rpb[l], rows)
        attend_lat = lambda qkv, kt, l=l, bias=bias: _na_attention(qkv, kt, ckt_b[:, l], cv_b[:, l], bias)
        xs, _, _ = layer(xs, x_sample.shape[:2], mod_s, l, attend_lat,
                         state_hgrn[:, l].astype(F32), wkt_b[l])

    y_prompt = _final_norm(xp, final_norm_w).reshape(x_prompt.shape)
    y_sample = _final_norm(xs, final_norm_w)
    new_cache_k = jnp.stack(ks_out, axis=1).reshape(nb, depth, seq, NA_HEADS, NA_DH)
    new_cache_v = jnp.stack(vs_out, axis=1).reshape(nb, depth, seq, NA_HEADS, NA_DH)
    new_state = jnp.stack(ss_out, axis=1).astype(x_prompt.dtype)
    return (y_prompt, y_sample, new_cache_k, new_cache_v, new_state)
```

```python
import functools

import numpy as np
import jax
import jax.numpy as jnp
from jax import lax
from jax.experimental import pallas as pl
from jax.experimental.pallas import tpu as pltpu

F32 = jnp.float32
BF16 = jnp.bfloat16

EPS = 1e-6
NA_HEADS = 8
NA_DH = 64
NA_W = NA_HEADS * NA_DH
GRID_W = 64
WIN_R = 8
WIN_C = 16
HG_HEADS = 4
HG_D = 128
HG_W = HG_HEADS * HG_D
N_EXPERTS = 8
LANES = 128
SUBLANES = 8
VMEM_LIMIT = 56 * 1024 * 1024

HG_CHUNK = 128
HG_SUB = 16
HG_UNROLL = 4
HG_FLAT = 32
HG_FLAT_RANGE = 60.0


def _cparams(sem):
    return pltpu.CompilerParams(dimension_semantics=sem, vmem_limit_bytes=VMEM_LIMIT)


def _dot(a, b):
    return jnp.dot(a, b, preferred_element_type=F32)


def _dot_nt(a, b):
    return lax.dot_general(a, b, (((1,), (1,)), ((), ())), preferred_element_type=F32)


def _dot_tn(a, b):
    return lax.dot_general(a, b, (((0,), (0,)), ((), ())), preferred_element_type=F32)


def _norm_mod(x, nw, shift, scale):
    ms = jnp.mean(x * x, axis=-1, keepdims=True)
    y = x * lax.rsqrt(ms + EPS) * nw
    return y * (1.0 + scale) + shift


def _mod_kernel(cond_ref, w_ref, b_ref, o_ref):
    c = cond_ref[...]
    s = c * jax.nn.sigmoid(c)
    o_ref[0] = _dot(s.astype(BF16), w_ref[0].astype(BF16)) + b_ref[0]


def _modulation(cond, w_ada, b_ada, tn=1536):
    depth, d, n = w_ada.shape
    rows = cond.shape[0]
    return pl.pallas_call(
        _mod_kernel,
        grid=(depth, n // tn),
        in_specs=[
            pl.BlockSpec((rows, d), lambda l, j: (0, 0)),
            pl.BlockSpec((1, d, tn), lambda l, j: (l, 0, j)),
            pl.BlockSpec((1, 1, tn), lambda l, j: (l, 0, j)),
        ],
        out_specs=pl.BlockSpec((1, rows, tn), lambda l, j: (l, 0, j)),
        out_shape=jax.ShapeDtypeStruct((depth, rows, n), F32),
        compiler_params=_cparams(("parallel", "parallel")),
        name="modulation",
    )(cond, w_ada, b_ada.reshape(depth, 1, n))


def _mod_index(mod):
    if mod.shape[0] == 1:
        return lambda b, *_: (0, 0, 0)
    return lambda b, *_: (b, 0, 0)


def _inproj_kernel(x_ref, nw_ref, mod_ref, w_ref, *refs, want_kt):
    if want_kt:
        wkt_ref, qkv_ref, hg_ref, extra_ref = refs
    else:
        qkv_ref, hg_ref, extra_ref = refs
    h = _norm_mod(x_ref[0], nw_ref[...], mod_ref[0, 0:1, :], mod_ref[0, 1:2, :]).astype(BF16)
    p = _dot(h, w_ref[...])
    qkv_ref[0] = p[:, :3 * NA_W].astype(BF16)
    hg_ref[0] = p[:, 3 * NA_W:]
    if want_kt:
        extra_ref[0] = _dot_nt(wkt_ref[...], h).astype(BF16)
    else:
        extra_ref[0] = p[:, NA_W:3 * NA_W]


def _inproj(x, nw, mod, w, wkt=None, tm=256):
    b, l, d = x.shape
    n = w.shape[1]
    want_kt = wkt is not None
    out_shape = [jax.ShapeDtypeStruct((b, l, 3 * NA_W), BF16),
                 jax.ShapeDtypeStruct((b, l, n - 3 * NA_W), F32)]
    out_specs = [pl.BlockSpec((1, tm, 3 * NA_W), lambda i, j: (i, j, 0)),
                 pl.BlockSpec((1, tm, n - 3 * NA_W), lambda i, j: (i, j, 0))]
    in_specs = [pl.BlockSpec((1, tm, d), lambda i, j: (i, j, 0)),
                pl.BlockSpec((1, d), lambda i, j: (0, 0)),
                pl.BlockSpec((1, SUBLANES, d), _mod_index(mod)),
                pl.BlockSpec((d, n), lambda i, j: (0, 0))]
    args = [x, nw.reshape(1, d), mod, w]
    if want_kt:
        in_specs.append(pl.BlockSpec((NA_W, d), lambda i, j: (0, 0)))
        args.append(wkt)
        out_shape.append(jax.ShapeDtypeStruct((b, NA_W, l), BF16))
        out_specs.append(pl.BlockSpec((1, NA_W, tm), lambda i, j: (i, 0, j)))
    else:
        out_shape.append(jax.ShapeDtypeStruct((b, l, 2 * NA_W), F32))
        out_specs.append(pl.BlockSpec((1, tm, 2 * NA_W), lambda i, j: (i, j, 0)))
    return pl.pallas_call(
        functools.partial(_inproj_kernel, want_kt=want_kt),
        grid=(b, l // tm),
        in_specs=in_specs,
        out_specs=out_specs,
        out_shape=out_shape,
        compiler_params=_cparams(("parallel", "parallel")),
        name="inproj",
    )(*args)


def _ctx_attn_kernel(q_ref, k_ref, v_ref, o_ref):
    scale = NA_DH ** -0.5
    for h in range(NA_HEADS):
        sl = slice(h * NA_DH, (h + 1) * NA_DH)
        q = q_ref[0, :, sl] * scale
        s = _dot_nt(q, k_ref[0, :, sl])
        m = jnp.max(s, axis=-1, keepdims=True)
        p = jnp.exp(s - m)
        den = jnp.sum(p, axis=-1, keepdims=True)
        o = _dot(p.astype(BF16), v_ref[0, :, sl]) / den
        o_ref[0, :, sl] = o.astype(BF16)


def _ctx_attention(qkv):
    b, l, _ = qkv.shape
    spec = lambda c: pl.BlockSpec((1, l, NA_W), lambda i, c=c: (i, 0, c))
    return pl.pallas_call(
        _ctx_attn_kernel,
        grid=(b,),
        in_specs=[spec(0), spec(1), spec(2)],
        out_specs=pl.BlockSpec((1, l, NA_W), lambda i: (i, 0, 0)),
        out_shape=jax.ShapeDtypeStruct((b, l, NA_W), BF16),
        compiler_params=_cparams(("parallel",)),
        name="ctx_attention",
    )(qkv, qkv, qkv)


NA_QROWS = 2
NA_KROWS = WIN_R + NA_QROWS


def _na_window_start(p, rows):
    return jnp.clip(NA_QROWS * p - WIN_R // 2, 0, rows - NA_KROWS)


def _na_kernel(q_ref, kt_ref, v_ref, ckt_ref, cv_ref, bias_ref, o_ref, s_ref, *, rows):
    ws = _na_window_start(pl.program_id(1), rows)
    start = pl.multiple_of(ws * GRID_W, NA_QROWS * GRID_W)
    n_loc = NA_KROWS * GRID_W
    scale = NA_DH ** -0.5
    maxes = []
    for h in range(NA_HEADS):
        sl = slice(h * NA_DH, (h + 1) * NA_DH)
        q = q_ref[0, :, sl] * scale
        s_loc = _dot(q, kt_ref[0, sl, pl.ds(start, n_loc)]) + bias_ref[0, h]
        s_ctx = _dot(q, ckt_ref[0, sl, :])
        s_ref[h, :, :n_loc] = s_loc
        s_ref[h, :, n_loc:] = s_ctx
        maxes.append(jnp.maximum(jnp.max(s_loc, axis=-1, keepdims=True),
                                 jnp.max(s_ctx, axis=-1, keepdims=True)))
    for h in range(NA_HEADS):
        sl = slice(h * NA_DH, (h + 1) * NA_DH)
        p_loc = jnp.exp(s_ref[h, :, :n_loc] - maxes[h])
        p_ctx = jnp.exp(s_ref[h, :, n_loc:] - maxes[h])
        den = jnp.sum(p_loc, axis=-1, keepdims=True) + jnp.sum(p_ctx, axis=-1, keepdims=True)
        o = (_dot(p_loc.astype(BF16), v_ref[0, pl.ds(start, n_loc), sl])
             + _dot(p_ctx.astype(BF16), cv_ref[0, :, sl])) / den
        o_ref[0, :, sl] = o.astype(BF16)


NA_EDGE = 2


def _na_variant(p, n_pairs):
    return jnp.where(p < NA_EDGE, p, jnp.where(p >= n_pairs - NA_EDGE, p - (n_pairs - 2 * NA_EDGE - 1), NA_EDGE))


def _na_bias_slabs(rpb, rows):
    n_pairs = rows // NA_QROWS
    ps = np.array([0, 1, 2, n_pairs - 2, n_pairs - 1])
    r = NA_QROWS * ps[:, None] + np.arange(NA_QROWS)[None, :]
    rs = np.clip(r - WIN_R // 2, 0, rows - WIN_R)
    ws = np.clip(NA_QROWS * ps - WIN_R // 2, 0, rows - NA_KROWS)
    krow = ws[:, None] + np.arange(NA_KROWS)[None, :]
    row_ok = (krow[:, None, :] >= rs[:, :, None]) & (krow[:, None, :] < rs[:, :, None] + WIN_R)
    dr = np.clip(krow[:, None, :] - r[:, :, None] + (WIN_R - 1), 0, 2 * WIN_R - 2)
    j = np.arange(GRID_W)[:, None]
    kc = np.arange(GRID_W)[None, :]
    cs = np.clip(j - WIN_C // 2, 0, GRID_W - WIN_C)
    col_ok = (kc >= cs) & (kc < cs + WIN_C)
    dc = np.clip(kc - j + (WIN_C - 1), 0, 2 * WIN_C - 2)
    planes = jnp.where(jnp.asarray(col_ok)[None, None], rpb.astype(F32)[:, :, dc], -jnp.inf)
    tab = jnp.take(planes, jnp.asarray(dr.reshape(-1)), axis=1)
    tab = tab.reshape((NA_HEADS,) + dr.shape + (GRID_W, GRID_W))
    tab = jnp.where(jnp.asarray(row_ok)[None, :, :, :, None, None], tab, -jnp.inf)
    tab = jnp.transpose(tab, (1, 0, 2, 4, 3, 5))
    return tab.reshape(len(ps), NA_HEADS, NA_QROWS * GRID_W, NA_KROWS * GRID_W)


def _na_attention(qkv, kt, ckt, cv, bias):
    b, l, _ = qkv.shape
    rows = l // GRID_W
    n_pairs = rows // NA_QROWS
    lc = cv.shape[1]
    tq = NA_QROWS * GRID_W
    n_loc = NA_KROWS * GRID_W
    return pl.pallas_call(
        functools.partial(_na_kernel, rows=rows),
        grid=(b, n_pairs),
        in_specs=[
            pl.BlockSpec((1, tq, NA_W), lambda i, p: (i, p, 0)),
            pl.BlockSpec((1, NA_W, l), lambda i, p: (i, 0, 0)),
            pl.BlockSpec((1, l, NA_W), lambda i, p: (i, 0, 2)),
            pl.BlockSpec((1, NA_W, lc), lambda i, p: (i, 0, 0)),
            pl.BlockSpec((1, lc, NA_W), lambda i, p: (i, 0, 0)),
            pl.BlockSpec((1, NA_HEADS, tq, n_loc), lambda i, p: (_na_variant(p, n_pairs), 0, 0, 0)),
        ],
        out_specs=pl.BlockSpec((1, tq, NA_W), lambda i, p: (i, p, 0)),
        out_shape=jax.ShapeDtypeStruct((b, l, NA_W), BF16),
        scratch_shapes=[pltpu.VMEM((NA_HEADS, tq, n_loc + lc), F32)],
        compiler_params=_cparams(("parallel", "arbitrary")),
        name="na_attention",
    )(qkv, kt, qkv, ckt, cv, bias)


def _split3(x):
    hi = x.astype(BF16)
    r1 = x - hi.astype(F32)
    mid = r1.astype(BF16)
    lo = (r1 - mid.astype(F32)).astype(BF16)
    return hi, mid, lo


def _block_ref(cum, block, ref_row):
    c = cum.shape[0]
    ref = cum.reshape(c // block, block, HG_D)[:, ref_row:ref_row + 1, :]
    return jnp.broadcast_to(ref, (c // block, block, HG_D)).reshape(c, HG_D)


def _hgrn_level(cum_f, cum_b, qs, k_f, k_b, m, ti, si):
    c = cum_f.shape[0]
    ref_f = _block_ref(cum_f, 2 * m, m - 1)
    ref_b = _block_ref(cum_b, 2 * m, m)
    row = lax.broadcasted_iota(jnp.int32, (c, 1), 0)
    upper = jnp.bitwise_and(row, m) != 0
    dec_f = jnp.exp(jnp.where(upper, cum_f - ref_f, ref_f - cum_f))
    dec_b = jnp.exp(jnp.where(upper, ref_b - cum_b, cum_b - ref_b))
    qm = jnp.concatenate([jnp.where(upper, qs * dec_f, 0.0), jnp.where(upper, 0.0, qs * dec_b)], axis=1)
    km = jnp.concatenate([jnp.where(upper, 0.0, k_f * dec_f), jnp.where(upper, k_b * dec_b, 0.0)], axis=1)
    am = _dot_nt(qm.astype(BF16), km.astype(BF16))
    if 2 * m < c:
        am = jnp.where(jnp.bitwise_xor(ti, si) < 2 * m, am, 0.0)
    return am


def _hgrn_pairwise(cum, qs, k, v, reverse):
    c = cum.shape[0]
    sub_row = lax.broadcasted_iota(jnp.int32, (SUBLANES, 1), 0)
    tiles_per_sub = HG_SUB // SUBLANES
    o_tiles = []
    for blk in range(c // HG_SUB):
        accs = [jnp.zeros((SUBLANES, HG_D), F32) for _ in range(tiles_per_sub)]
        for j in range(HG_SUB):
            s = blk * HG_SUB + j
            cs, ks, vs = cum[s:s + 1, :], k[s:s + 1, :], v[s:s + 1, :]
            for t in range(tiles_per_sub):
                lo_r, hi_r = t * SUBLANES, t * SUBLANES + SUBLANES - 1
                if (lo_r > j) if reverse else (hi_r < j):
                    continue
                base = blk * HG_SUB + lo_r
                x = qs[base:base + SUBLANES, :] * ks * jnp.exp(
                    jnp.minimum(cum[base:base + SUBLANES, :] - cs, 0.0))
                a = jnp.sum(x, axis=-1, keepdims=True)
                if reverse and hi_r > j:
                    a = jnp.where(sub_row + lo_r <= j, a, 0.0)
                elif (not reverse) and lo_r < j:
                    a = jnp.where(sub_row + lo_r >= j, a, 0.0)
                accs[t] = accs[t] + a * vs
        o_tiles.extend(accs)
    return jnp.concatenate(o_tiles, axis=0)


def _hgrn_gates(z, consts):
    loglb, log1mlb, oml = consts
    e = jnp.exp(-jnp.abs(z))
    logsig = jnp.minimum(z, 0.0) - jnp.log1p(e)
    bb = log1mlb + logsig
    logf = jnp.maximum(loglb, bb) + jnp.log1p(jnp.exp(-jnp.abs(loglb - bb)))
    k = oml * (jnp.where(z >= 0.0, e, 1.0) / (1.0 + e))
    return logf, k


def _flat_offsets(cum_f, cum_b):
    return (cum_f - _block_ref(cum_f, HG_FLAT, HG_FLAT // 2 - 1),
            cum_b - _block_ref(cum_b, HG_FLAT, HG_FLAT // 2))


def _hgrn_kernel(q_ref, i_ref, zf_ref, zb_ref, g_ref, lbc_ref, nw_ref, s0_ref,
                 o_ref, s_ref, cum_ref, k_ref, qs_ref, acc_ref, qt_ref, u_ref, de_ref):
    l = q_ref.shape[1]
    c = HG_CHUNK
    nc = l // c
    consts = (lbc_ref[0:1, :], lbc_ref[1:2, :], lbc_ref[2:3, :])
    ti = lax.broadcasted_iota(jnp.int32, (c, c), 0)
    si = lax.broadcasted_iota(jnp.int32, (c, c), 1)

    def chunk_rows(i):
        return pl.ds(pl.multiple_of(i * c, c), c)

    def prepare(i, gmax):
        rows = chunk_rows(i)
        q = q_ref[0, rows, :]
        qs_ref[rows, :] = q * jax.nn.sigmoid(q)
        cums = []
        for d, z_ref in enumerate((zf_ref, zb_ref)):
            logf, k = _hgrn_gates(z_ref[0, rows, :], consts)
            k_ref[d, rows, :] = k
            tri = jnp.where((si >= ti) if d else (si <= ti), 1.0, 0.0).astype(BF16)
            parts = _dot(tri, jnp.concatenate(_split3(logf), axis=1))
            cum = parts[:, :HG_D] + parts[:, HG_D:2 * HG_D] + parts[:, 2 * HG_D:]
            cum_ref[d, rows, :] = cum
            cums.append(cum)
        dq_f, dq_b = _flat_offsets(*cums)
        g = jnp.maximum(jnp.abs(dq_f), jnp.abs(dq_b))
        return jnp.maximum(gmax, jnp.max(g.reshape(c // SUBLANES, SUBLANES, HG_D), axis=0))

    gmax = lax.fori_loop(0, nc, prepare, jnp.zeros((SUBLANES, HG_D), F32), unroll=HG_UNROLL)
    flat_ok = jnp.max(gmax) <= HG_FLAT_RANGE

    def intra(i, flat):
        rows = chunk_rows(i)
        cum_f, cum_b = cum_ref[0, rows, :], cum_ref[1, rows, :]
        k_f, k_b = k_ref[0, rows, :], k_ref[1, rows, :]
        qs = qs_ref[rows, :]
        v = i_ref[0, rows, :]
        vb = v.astype(BF16)
        a = jnp.zeros((c, c), F32)
        m = c // 2
        while m >= (HG_FLAT if flat else HG_SUB):
            a = a + _hgrn_level(cum_f, cum_b, qs, k_f, k_b, m, ti, si)
            m //= 2
        if flat:
            dq_f, dq_b = _flat_offsets(cum_f, cum_b)
            same = jnp.bitwise_xor(ti, si) < HG_FLAT
            a_f = _dot_nt((qs * jnp.exp(dq_f)).astype(BF16), (k_f * jnp.exp(-dq_f)).astype(BF16))
            a_b = _dot_nt((qs * jnp.exp(dq_b)).astype(BF16), (k_b * jnp.exp(-dq_b)).astype(BF16))
            a = a + jnp.where(same & (si <= ti), a_f, 0.0) + jnp.where(same & (si >= ti), a_b, 0.0)
            o = _dot(a.astype(BF16), vb)
        else:
            o = (_dot(a.astype(BF16), vb) + _hgrn_pairwise(cum_f, qs, k_f, v, False)
                 + _hgrn_pairwise(cum_b, qs, k_b, v, True))
        acc_ref[rows, :] = o
        edge_f, edge_b = cum_f[c - 1:c, :], cum_b[0:1, :]
        qt_ref[rows, :] = jnp.concatenate([(qs * jnp.exp(cum_f)).astype(BF16),
                                           (qs * jnp.exp(cum_b)).astype(BF16)], axis=1)
        kd = jnp.concatenate([(k_f * jnp.exp(edge_f - cum_f)).astype(BF16),
                              (k_b * jnp.exp(edge_b - cum_b)).astype(BF16)], axis=1)
        u_ref[i] = _dot_tn(vb, kd)
        de_ref[i] = jnp.broadcast_to(jnp.concatenate([jnp.exp(edge_f), jnp.exp(edge_b)], axis=1),
                                     (SUBLANES, 2 * HG_D))

    def run_intra(flat):
        def body(i, carry):
            intra(i, flat)
            return carry
        lax.fori_loop(0, nc, body, 0, unroll=HG_UNROLL if flat else 1)

    pl.when(flat_ok)(lambda: run_intra(True))
    pl.when(jnp.logical_not(flat_ok))(lambda: run_intra(False))

    def sweep(i, carry):
        st_f, st_b = carry
        j = nc - 1 - i
        inc = u_ref[i, :, :HG_D]
        u_ref[i, :, :HG_D] = st_f
        st_f = st_f * de_ref[i, 0:1, :HG_D] + inc
        inc = u_ref[j, :, HG_D:]
        u_ref[j, :, HG_D:] = st_b
        st_b = st_b * de_ref[j, 0:1, HG_D:] + inc
        return st_f, st_b

    st_f, st_b = lax.fori_loop(0, nc, sweep, (s0_ref[0, 0, 0].T, s0_ref[0, 1, 0].T))
    s_ref[0, 0, 0] = st_f.T
    s_ref[0, 1, 0] = st_b.T

    def finish(i, carry):
        rows = chunk_rows(i)
        o = acc_ref[rows, :] + _dot_nt(qt_ref[rows, :], u_ref[i].astype(BF16))
        o = o * lax.rsqrt(jnp.mean(o * o, axis=-1, keepdims=True) + EPS) * nw_ref[...]
        g = g_ref[0, rows, :]
        o_ref[0, rows, :] = (o * (g * jax.nn.sigmoid(g))).astype(BF16)
        return carry

    lax.fori_loop(0, nc, finish, 0, unroll=HG_UNROLL)


def _hgrn(hgp, lbc, norm_w, s0):
    b, l, _ = hgp.shape
    col = lambda sec: pl.BlockSpec((1, l, HG_D), lambda i, h, sec=sec: (i, 0, sec * HG_HEADS + h))
    state_spec = pl.BlockSpec((1, 2, 1, HG_D, HG_D), lambda i, h: (i, 0, h, 0, 0))
    return pl.pallas_call(
        _hgrn_kernel,
        grid=(b, HG_HEADS),
        in_specs=[col(0), col(1), col(2), col(3), col(4),
                  pl.BlockSpec((SUBLANES, HG_D), lambda i, h: (0, h)),
                  pl.BlockSpec((1, HG_D), lambda i, h: (0, 0)),
                  state_spec],
        out_specs=[pl.BlockSpec((1, l, HG_D), lambda i, h: (i, 0, h)), state_spec],
        out_shape=[jax.ShapeDtypeStruct((b, l, HG_W), BF16),
                   jax.ShapeDtypeStruct((b, 2, HG_HEADS, HG_D, HG_D), F32)],
        scratch_shapes=[pltpu.VMEM((2, l, HG_D), F32),
                        pltpu.VMEM((2, l, HG_D), F32),
                        pltpu.VMEM((l, HG_D), F32),
                        pltpu.VMEM((l, HG_D), F32),
                        pltpu.VMEM((l, 2 * HG_D), BF16),
                        pltpu.VMEM((l // HG_CHUNK, HG_D, 2 * HG_D), F32),
                        pltpu.VMEM((l // HG_CHUNK, SUBLANES, 2 * HG_D), F32)],
        compiler_params=_cparams(("parallel", "parallel")),
        name="hgrn",
    )(hgp, hgp, hgp, hgp, hgp, lbc, norm_w.reshape(1, HG_D), s0)


def _outproj_kernel(x_ref, att_ref, hg_ref, mod_ref, w_ref, o_ref):
    y = _dot(att_ref[0], w_ref[:NA_W, :]) + _dot(hg_ref[0], w_ref[NA_W:, :])
    o_ref[0] = x_ref[0] + mod_ref[0, 2:3, :] * y


def _outproj(x, att, hg, mod, w, tm=512):
    b, l, d = x.shape
    tm = min(tm, l)
    return pl.pallas_call(
        _outproj_kernel,
        grid=(b, l // tm),
        in_specs=[
            pl.BlockSpec((1, tm, d), lambda i, j: (i, j, 0)),
            pl.BlockSpec((1, tm, NA_W), lambda i, j: (i, j, 0)),
            pl.BlockSpec((1, tm, HG_W), lambda i, j: (i, j, 0)),
            pl.BlockSpec((1, SUBLANES, d), _mod_index(mod)),
            pl.BlockSpec(w.shape, lambda i, j: (0, 0)),
        ],
        out_specs=pl.BlockSpec((1, tm, d), lambda i, j: (i, j, 0)),
        out_shape=jax.ShapeDtypeStruct(x.shape, F32),
        compiler_params=_cparams(("parallel", "parallel")),
        name="outproj",
    )(x, att, hg, mod, w)


def _ffn_kernel(x_ref, nw_ref, mod_ref, w1_ref, w3_ref, w2_ref, o_ref, h_ref, acc_ref):
    f = pl.program_id(2)

    @pl.when(f == 0)
    def _():
        h = _norm_mod(x_ref[0], nw_ref[...], mod_ref[0, 3:4, :], mod_ref[0, 4:5, :])
        h_ref[...] = h.astype(BF16)
        acc_ref[...] = jnp.zeros_like(acc_ref)

    h = h_ref[...]
    a = _dot(h, w1_ref[...])
    g = (a * jax.nn.sigmoid(a)) * _dot(h, w3_ref[...])
    acc_ref[...] += _dot(g.astype(BF16), w2_ref[...])

    @pl.when(f == pl.num_programs(2) - 1)
    def _():
        o_ref[0] = x_ref[0] + mod_ref[0, 5:6, :] * acc_ref[...]


def _ffn(x, nw, mod, w1, w3, w2, tm=512, tf=1408):
    b, l, d = x.shape
    tm = min(tm, l)
    ff = w1.shape[1]
    tf = min(tf, ff)
    return pl.pallas_call(
        _ffn_kernel,
        grid=(b, l // tm, ff // tf),
        in_specs=[
            pl.BlockSpec((1, tm, d), lambda i, j, f: (i, j, 0)),
            pl.BlockSpec((1, d), lambda i, j, f: (0, 0)),
            pl.BlockSpec((1, SUBLANES, d), _mod_index(mod)),
            pl.BlockSpec((d, tf), lambda i, j, f: (0, f)),
            pl.BlockSpec((d, tf), lambda i, j, f: (0, f)),
            pl.BlockSpec((tf, d), lambda i, j, f: (f, 0)),
        ],
        out_specs=pl.BlockSpec((1, tm, d), lambda i, j, f: (i, j, 0)),
        out_shape=jax.ShapeDtypeStruct(x.shape, F32),
        scratch_shapes=[pltpu.VMEM((tm, d), BF16), pltpu.VMEM((tm, d), F32)],
        compiler_params=_cparams(("parallel", "parallel", "arbitrary")),
        name="ffn",
    )(x, nw.reshape(1, d), mod, w1, w3, w2)


def _router_kernel(x_ref, nw_ref, mod_ref, r_ref, g_ref):
    h = _norm_mod(x_ref[0], nw_ref[...], mod_ref[0, 3:4, :], mod_ref[0, 4:5, :])
    hh = h.astype(BF16)
    hl = (h - hh.astype(F32)).astype(BF16)
    r = r_ref[...]
    rh = r.astype(BF16)
    rl = (r - rh.astype(F32)).astype(BF16)
    logits = _dot(hh, rh) + _dot(hh, rl) + _dot(hl, rh)
    lane = lax.broadcasted_iota(jnp.int32, logits.shape, 1)
    logits = jnp.where(lane < N_EXPERTS, logits, -jnp.inf)
    m1 = jnp.max(logits, axis=-1, keepdims=True)
    i1 = jnp.min(jnp.where(logits == m1, lane, LANES), axis=-1, keepdims=True)
    rest = jnp.where(lane == i1, -jnp.inf, logits)
    m2 = jnp.max(rest, axis=-1, keepdims=True)
    i2 = jnp.min(jnp.where(rest == m2, lane, LANES), axis=-1, keepdims=True)
    e2 = jnp.exp(m2 - m1)
    den = 1.0 + e2
    g_ref[0] = jnp.where(lane == i1, 1.0 / den, 0.0) + jnp.where(lane == i2, e2 / den, 0.0)


def _router(x, nw, mod, router, tm=512):
    b, l, d = x.shape
    tm = min(tm, l)
    rp = jnp.zeros((d, LANES), F32).at[:, :router.shape[1]].set(router)
    return pl.pallas_call(
        _router_kernel,
        grid=(b, l // tm),
        in_specs=[
            pl.BlockSpec((1, tm, d), lambda i, j: (i, j, 0)),
            pl.BlockSpec((1, d), lambda i, j: (0, 0)),
            pl.BlockSpec((1, SUBLANES, d), _mod_index(mod)),
            pl.BlockSpec((d, LANES), lambda i, j: (0, 0)),
        ],
        out_specs=pl.BlockSpec((1, tm, LANES), lambda i, j: (i, j, 0)),
        out_shape=jax.ShapeDtypeStruct((b, l, LANES), F32),
        compiler_params=_cparams(("parallel", "parallel")),
        name="router",
    )(x, nw.reshape(1, d), mod, rp)


def _moe_kernel(x_ref, nw_ref, mod_ref, gate_ref, w1_ref, w3_ref, w2_ref, o_ref,
                h_ref, acc_ref, acc_e_ref):
    e = pl.program_id(2)
    f = pl.program_id(3)
    nf = pl.num_programs(3)

    @pl.when((e == 0) & (f == 0))
    def _():
        h = _norm_mod(x_ref[0], nw_ref[...], mod_ref[0, 3:4, :], mod_ref[0, 4:5, :])
        h_ref[...] = h.astype(BF16)
        acc_ref[...] = jnp.zeros_like(acc_ref)

    h = h_ref[...]
    a = _dot(h, w1_ref[0])
    g = (a * jax.nn.sigmoid(a)) * _dot(h, w3_ref[0])
    y = _dot(g.astype(BF16), w2_ref[0])

    @pl.when(f == 0)
    def _():
        acc_e_ref[...] = y

    @pl.when(f > 0)
    def _():
        acc_e_ref[...] += y

    @pl.when(f == nf - 1)
    def _():
        gates = gate_ref[0]
        lane = lax.broadcasted_iota(jnp.int32, gates.shape, 1)
        ge = jnp.sum(jnp.where(lane == e, gates, 0.0), axis=-1, keepdims=True)
        acc_ref[...] += ge * acc_e_ref[...]

    @pl.when((e == pl.num_programs(2) - 1) & (f == nf - 1))
    def _():
        o_ref[0] = x_ref[0] + mod_ref[0, 5:6, :] * acc_ref[...]


def _moe(x, nw, mod, gates, w1, w3, w2, tm=512, tf=1408):
    b, l, d = x.shape
    tm = min(tm, l)
    ne, _, ff = w1.shape
    tf = min(tf, ff)
    return pl.pallas_call(
        _moe_kernel,
        grid=(b, l // tm, ne, ff // tf),
        in_specs=[
            pl.BlockSpec((1, tm, d), lambda i, j, e, f: (i, j, 0)),
            pl.BlockSpec((1, d), lambda i, j, e, f: (0, 0)),
            pl.BlockSpec((1, SUBLANES, d), _mod_index(mod)),
            pl.BlockSpec((1, tm, LANES), lambda i, j, e, f: (i, j, 0)),
            pl.BlockSpec((1, d, tf), lambda i, j, e, f: (e, 0, f)),
            pl.BlockSpec((1, d, tf), lambda i, j, e, f: (e, 0, f)),
            pl.BlockSpec((1, tf, d), lambda i, j, e, f: (e, f, 0)),
        ],
        out_specs=pl.BlockSpec((1, tm, d), lambda i, j, e, f: (i, j, 0)),
        out_shape=jax.ShapeDtypeStruct(x.shape, F32),
        scratch_shapes=[pltpu.VMEM((tm, d), BF16), pltpu.VMEM((tm, d), F32),
                        pltpu.VMEM((tm, d), F32)],
        compiler_params=_cparams(("parallel", "parallel", "arbitrary", "arbitrary")),
        name="moe",
    )(x, nw.reshape(1, d), mod, gates, w1, w3, w2)


def _final_norm_kernel(x_ref, nw_ref, o_ref):
    x = x_ref[0]
    o_ref[0] = x * lax.rsqrt(jnp.mean(x * x, axis=-1, keepdims=True) + EPS) * nw_ref[...]


def _final_norm(x, nw, tm=512):
    b, l, d = x.shape
    tm = min(tm, l)
    return pl.pallas_call(
        _final_norm_kernel,
        grid=(b, l // tm),
        in_specs=[pl.BlockSpec((1, tm, d), lambda i, j: (i, j, 0)),
                  pl.BlockSpec((1, d), lambda i, j: (0, 0))],
        out_specs=pl.BlockSpec((1, tm, d), lambda i, j: (i, j, 0)),
        out_shape=jax.ShapeDtypeStruct(x.shape, F32),
        compiler_params=_cparams(("parallel", "parallel")),
        name="final_norm",
    )(x, nw.reshape(1, d))


def _pad_rows(a, rows):
    return jnp.zeros((rows,) + a.shape[1:], a.dtype).at[:a.shape[0]].set(a)


def kernel(x_prompt, x_sample, cache_k, cache_v, state_hgrn, c, c_ctx, norm1_w, norm2_w, w_ada, b_ada,
           w_in, rpb, hg_lower, hg_norm_w, w_out, ffn_w1, ffn_w3, ffn_w2, moe_router, moe_w1, moe_w3,
           moe_w2, final_norm_w):
    depth = w_in.shape[0]
    d = x_prompt.shape[-1]
    nb = x_prompt.shape[0]
    nd = x_sample.shape[0]
    rows = x_sample.shape[1] // GRID_W

    lbs = jnp.cumsum(jax.nn.softmax(hg_lower.astype(F32), axis=0), axis=0)
    lbs = lbs - lbs[0:1]
    lbc = _pad_rows(jnp.stack([jnp.log(lbs), jnp.log1p(-lbs), 1.0 - lbs], axis=1).transpose(1, 0, 2),
                    SUBLANES).transpose(1, 0, 2)

    cond = _pad_rows(jnp.concatenate([c, c_ctx[None, :]], axis=0), SUBLANES)
    mods = _modulation(cond, w_ada, b_ada)
    mods = _pad_rows(mods.reshape(depth, SUBLANES, 6, d).transpose(2, 0, 1, 3),
                     SUBLANES).transpose(1, 2, 0, 3)

    w_in_b = w_in.astype(BF16)
    w_out_b = w_out.astype(BF16)
    ffn_w1_b, ffn_w3_b, ffn_w2_b = ffn_w1.astype(BF16), ffn_w3.astype(BF16), ffn_w2.astype(BF16)
    moe_w1_b, moe_w3_b, moe_w2_b = moe_w1.astype(BF16), moe_w3.astype(BF16), moe_w2.astype(BF16)
    wkt_b = jnp.swapaxes(w_in[:, :, NA_W:2 * NA_W], 1, 2).astype(BF16)
    lc = cache_k.shape[2]
    ckt_b = jnp.swapaxes(cache_k.astype(BF16).reshape(nd, depth, lc, NA_W), 2, 3)
    cv_b = cache_v.astype(BF16).reshape(nd, depth, lc, NA_W)
    zero_state = jnp.zeros((nb, 2, HG_HEADS, HG_D, HG_D), F32)

    def layer(x, seq_shape, mod, l, attend, s0, wkt):
        tok = lambda a: a.reshape(x.shape[:2] + a.shape[-1:])
        seq = lambda a: a.reshape(seq_shape + a.shape[-1:])
        want_kv = wkt is None
        outs = _inproj(x, norm1_w[l], mod, w_in_b[l], wkt)
        att = attend(seq(outs[0]), outs[2])
        hg, s_fin = _hgrn(seq(outs[1]), lbc[l], hg_norm_w[l], s0)
        x = _outproj(x, tok(att), tok(hg), mod, w_out_b[l])
        i = l // 2
        if l % 2 == 0:
            x = _ffn(x, norm2_w[l], mod, ffn_w1_b[i], ffn_w3_b[i], ffn_w2_b[i])
        else:
            gates = _router(x, norm2_w[l], mod, moe_router[i])
            x = _moe(x, norm2_w[l], mod, gates, moe_w1_b[i], moe_w3_b[i], moe_w2_b[i])
        return x, (seq(outs[2]) if want_kv else None), s_fin

    seq = x_prompt.shape[1]
    xp, xs = x_prompt.reshape(1, nb * seq, d), x_sample
    ks_out, vs_out, ss_out = [], [], []
    for l in range(depth):
        mod_p = mods[l, nd:nd + 1]
        mod_s = mods[l, :nd]
        xp, kv, s_fin = layer(xp, (nb, seq), mod_p, l, lambda qkv, _: _ctx_attention(qkv),
                              zero_state, None)
        ks_out.append(kv[..., :NA_W])
        vs_out.append(kv[..., NA_W:])
        ss_out.append(s_fin)
        bias = _na_bias_slabs(rpb[l], rows)
        attend_lat = lambda qkv, kt, l=l, bias=bias: _na_attention(qkv, kt, ckt_b[:, l], cv_b[:, l], bias)
        xs, _, _ = layer(xs, x_sample.shape[:2], mod_s, l, attend_lat,
                         state_hgrn[:, l].astype(F32), wkt_b[l])

    y_prompt = _final_norm(xp, final_norm_w).reshape(x_prompt.shape)
    y_sample = _final_norm(xs, final_norm_w)
    new_cache_k = jnp.stack(ks_out, axis=1).reshape(nb, depth, seq, NA_HEADS, NA_DH)
    new_cache_v = jnp.stack(vs_out, axis=1).reshape(nb, depth, seq, NA_HEADS, NA_DH)
    new_state = jnp.stack(ss_out, axis=1).astype(x_prompt.dtype)
    return (y_prompt, y_sample, new_cache_k, new_cache_v, new_state)
```

```python
import functools

import numpy as np
import jax
import jax.numpy as jnp
from jax import lax
from jax.experimental import pallas as pl
from jax.experimental.pallas import tpu as pltpu

F32 = jnp.float32
BF16 = jnp.bfloat16

EPS = 1e-6
NA_HEADS = 8
NA_DH = 64
NA_W = NA_HEADS * NA_DH
GRID_W = 64
WIN_R = 8
WIN_C = 16
HG_HEADS = 4
HG_D = 128
HG_W = HG_HEADS * HG_D
N_EXPERTS = 8
LANES = 128
SUBLANES = 8
VMEM_LIMIT = 56 * 1024 * 1024

HG_CHUNK = 128
HG_SUB = 16
HG_UNROLL = 4
HG_FLAT = 32
HG_FLAT_RANGE = 60.0


def _cparams(sem):
    return pltpu.CompilerParams(dimension_semantics=sem, vmem_limit_bytes=VMEM_LIMIT)


def _dot(a, b):
    return jnp.dot(a, b, preferred_element_type=F32)


def _dot_nt(a, b):
    return lax.dot_general(a, b, (((1,), (1,)), ((), ())), preferred_element_type=F32)


def _dot_tn(a, b):
    return lax.dot_general(a, b, (((0,), (0,)), ((), ())), preferred_element_type=F32)


def _norm_mod(x, nw, shift, scale):
    ms = jnp.mean(x * x, axis=-1, keepdims=True)
    y = x * lax.rsqrt(ms + EPS) * nw
    return y * (1.0 + scale) + shift


def _mod_kernel(cond_ref, w_ref, b_ref, o_ref):
    c = cond_ref[...]
    s = c * jax.nn.sigmoid(c)
    o_ref[0] = _dot(s.astype(BF16), w_ref[0].astype(BF16)) + b_ref[0]


def _modulation(cond, w_ada, b_ada, tn=1536):
    depth, d, n = w_ada.shape
    rows = cond.shape[0]
    return pl.pallas_call(
        _mod_kernel,
        grid=(depth, n // tn),
        in_specs=[
            pl.BlockSpec((rows, d), lambda l, j: (0, 0)),
            pl.BlockSpec((1, d, tn), lambda l, j: (l, 0, j)),
            pl.BlockSpec((1, 1, tn), lambda l, j: (l, 0, j)),
        ],
        out_specs=pl.BlockSpec((1, rows, tn), lambda l, j: (l, 0, j)),
        out_shape=jax.ShapeDtypeStruct((depth, rows, n), F32),
        compiler_params=_cparams(("parallel", "parallel")),
        name="modulation",
    )(cond, w_ada, b_ada.reshape(depth, 1, n))


def _mod_index(mod):
    if mod.shape[0] == 1:
        return lambda b, *_: (0, 0, 0)
    return lambda b, *_: (b, 0, 0)


def _inproj_kernel(x_ref, nw_ref, mod_ref, w_ref, *refs, want_kt):
    if want_kt:
        wkt_ref, qkv_ref, hg_ref, extra_ref = refs
    else:
        qkv_ref, hg_ref, extra_ref = refs
    h = _norm_mod(x_ref[0], nw_ref[...], mod_ref[0, 0:1, :], mod_ref[0, 1:2, :]).astype(BF16)
    p = _dot(h, w_ref[...])
    qkv_ref[0] = p[:, :3 * NA_W].astype(BF16)
    hg_ref[0] = p[:, 3 * NA_W:]
    if want_kt:
        extra_ref[0] = _dot_nt(wkt_ref[...], h).astype(BF16)
    else:
        extra_ref[0] = p[:, NA_W:3 * NA_W]


def _inproj(x, nw, mod, w, wkt=None, tm=256):
    b, l, d = x.shape
    n = w.shape[1]
    want_kt = wkt is not None
    out_shape = [jax.ShapeDtypeStruct((b, l, 3 * NA_W), BF16),
                 jax.ShapeDtypeStruct((b, l, n - 3 * NA_W), F32)]
    out_specs = [pl.BlockSpec((1, tm, 3 * NA_W), lambda i, j: (i, j, 0)),
                 pl.BlockSpec((1, tm, n - 3 * NA_W), lambda i, j: (i, j, 0))]
    in_specs = [pl.BlockSpec((1, tm, d), lambda i, j: (i, j, 0)),
                pl.BlockSpec((1, d), lambda i, j: (0, 0)),
                pl.BlockSpec((1, SUBLANES, d), _mod_index(mod)),
                pl.BlockSpec((d, n), lambda i, j: (0, 0))]
    args = [x, nw.reshape(1, d), mod, w]
    if want_kt:
        in_specs.append(pl.BlockSpec((NA_W, d), lambda i, j: (0, 0)))
        args.append(wkt)
        out_shape.append(jax.ShapeDtypeStruct((b, NA_W, l), BF16))
        out_specs.append(pl.BlockSpec((1, NA_W, tm), lambda i, j: (i, 0, j)))
    else:
        out_shape.append(jax.ShapeDtypeStruct((b, l, 2 * NA_W), F32))
        out_specs.append(pl.BlockSpec((1, tm, 2 * NA_W), lambda i, j: (i, j, 0)))
    return pl.pallas_call(
        functools.partial(_inproj_kernel, want_kt=want_kt),
        grid=(b, l // tm),
        in_specs=in_specs,
        out_specs=out_specs,
        out_shape=out_shape,
        compiler_params=_cparams(("parallel", "parallel")),
        name="inproj",
    )(*args)


def _ctx_attn_kernel(q_ref, k_ref, v_ref, o_ref):
    scale = NA_DH ** -0.5
    for h in range(NA_HEADS):
        sl = slice(h * NA_DH, (h + 1) * NA_DH)
        q = q_ref[0, :, sl] * scale
        s = _dot_nt(q, k_ref[0, :, sl])
        m = jnp.max(s, axis=-1, keepdims=True)
        p = jnp.exp(s - m)
        den = jnp.sum(p, axis=-1, keepdims=True)
        o = _dot(p.astype(BF16), v_ref[0, :, sl]) / den
        o_ref[0, :, sl] = o.astype(BF16)


def _ctx_attention(qkv):
    b, l, _ = qkv.shape
    spec = lambda c: pl.BlockSpec((1, l, NA_W), lambda i, c=c: (i, 0, c))
    return pl.pallas_call(
        _ctx_attn_kernel,
        grid=(b,),
        in_specs=[spec(0), spec(1), spec(2)],
        out_specs=pl.BlockSpec((1, l, NA_W), lambda i: (i, 0, 0)),
        out_shape=jax.ShapeDtypeStruct((b, l, NA_W), BF16),
        compiler_params=_cparams(("parallel",)),
        name="ctx_attention",
    )(qkv, qkv, qkv)


NA_QROWS = 2
NA_KROWS = WIN_R + NA_QROWS


def _na_window_start(p, rows):
    return jnp.clip(NA_QROWS * p - WIN_R // 2, 0, rows - NA_KROWS)


def _na_kernel(q_ref, kt_ref, v_ref, ckt_ref, cv_ref, bias_ref, o_ref, s_ref, *, rows):
    ws = _na_window_start(pl.program_id(1), rows)
    start = pl.multiple_of(ws * GRID_W, NA_QROWS * GRID_W)
    n_loc = NA_KROWS * GRID_W
    scale = NA_DH ** -0.5
    maxes = []
    for h in range(NA_HEADS):
        sl = slice(h * NA_DH, (h + 1) * NA_DH)
        q = q_ref[0, :, sl] * scale
        s_loc = _dot(q, kt_ref[0, sl, pl.ds(start, n_loc)]) + bias_ref[0, h]
        s_ctx = _dot(q, ckt_ref[0, sl, :])
        s_ref[h, :, :n_loc] = s_loc
        s_ref[h, :, n_loc:] = s_ctx
        maxes.append(jnp.maximum(jnp.max(s_loc, axis=-1, keepdims=True),
                                 jnp.max(s_ctx, axis=-1, keepdims=True)))
    for h in range(NA_HEADS):
        sl = slice(h * NA_DH, (h + 1) * NA_DH)
        p_loc = jnp.exp(s_ref[h, :, :n_loc] - maxes[h])
        p_ctx = jnp.exp(s_ref[h, :, n_loc:] - maxes[h])
        den = jnp.sum(p_loc, axis=-1, keepdims=True) + jnp.sum(p_ctx, axis=-1, keepdims=True)
        o = (_dot(p_loc.astype(BF16), v_ref[0, pl.ds(start, n_loc), sl])
             + _dot(p_ctx.astype(BF16), cv_ref[0, :, sl])) / den
        o_ref[0, :, sl] = o.astype(BF16)


NA_EDGE = 2


def _na_variant(p, n_pairs):
    return jnp.where(p < NA_EDGE, p, jnp.where(p >= n_pairs - NA_EDGE, p - (n_pairs - 2 * NA_EDGE - 1), NA_EDGE))


def _na_bias_slabs(rpb, rows):
    n_pairs = rows // NA_QROWS
    ps = np.array([0, 1, 2, n_pairs - 2, n_pairs - 1])
    r = NA_QROWS * ps[:, None] + np.arange(NA_QROWS)[None, :]
    rs = np.clip(r - WIN_R // 2, 0, rows - WIN_R)
    ws = np.clip(NA_QROWS * ps - WIN_R // 2, 0, rows - NA_KROWS)
    krow = ws[:, None] + np.arange(NA_KROWS)[None, :]
    row_ok = (krow[:, None, :] >= rs[:, :, None]) & (krow[:, None, :] < rs[:, :, None] + WIN_R)
    dr = np.clip(krow[:, None, :] - r[:, :, None] + (WIN_R - 1), 0, 2 * WIN_R - 2)
    j = np.arange(GRID_W)[:, None]
    kc = np.arange(GRID_W)[None, :]
    cs = np.clip(j - WIN_C // 2, 0, GRID_W - WIN_C)
    col_ok = (kc >= cs) & (kc < cs + WIN_C)
    dc = np.clip(kc - j + (WIN_C - 1), 0, 2 * WIN_C - 2)
    planes = jnp.where(jnp.asarray(col_ok)[None, None], rpb.astype(F32)[:, :, dc], -jnp.inf)
    tab = jnp.take(planes, jnp.asarray(dr.reshape(-1)), axis=1)
    tab = tab.reshape((NA_HEADS,) + dr.shape + (GRID_W, GRID_W))
    tab = jnp.where(jnp.asarray(row_ok)[None, :, :, :, None, None], tab, -jnp.inf)
    tab = jnp.transpose(tab, (1, 0, 2, 4, 3, 5))
    return tab.reshape(len(ps), NA_HEADS, NA_QROWS * GRID_W, NA_KROWS * GRID_W)


def _na_attention(qkv, kt, ckt, cv, bias):
    b, l, _ = qkv.shape
    rows = l // GRID_W
    n_pairs = rows // NA_QROWS
    lc = cv.shape[1]
    tq = NA_QROWS * GRID_W
    n_loc = NA_KROWS * GRID_W
    return pl.pallas_call(
        functools.partial(_na_kernel, rows=rows),
        grid=(b, n_pairs),
        in_specs=[
            pl.BlockSpec((1, tq, NA_W), lambda i, p: (i, p, 0)),
            pl.BlockSpec((1, NA_W, l), lambda i, p: (i, 0, 0)),
            pl.BlockSpec((1, l, NA_W), lambda i, p: (i, 0, 2)),
            pl.BlockSpec((1, NA_W, lc), lambda i, p: (i, 0, 0)),
            pl.BlockSpec((1, lc, NA_W), lambda i, p: (i, 0, 0)),
            pl.BlockSpec((1, NA_HEADS, tq, n_loc), lambda i, p: (_na_variant(p, n_pairs), 0, 0, 0)),
        ],
        out_specs=pl.BlockSpec((1, tq, NA_W), lambda i, p: (i, p, 0)),
        out_shape=jax.ShapeDtypeStruct((b, l, NA_W), BF16),
        scratch_shapes=[pltpu.VMEM((NA_HEADS, tq, n_loc + lc), F32)],
        compiler_params=_cparams(("parallel", "arbitrary")),
        name="na_attention",
    )(qkv, kt, qkv, ckt, cv, bias)


def _split3(x):
    hi = x.astype(BF16)
    r1 = x - hi.astype(F32)
    mid = r1.astype(BF16)
    lo = (r1 - mid.astype(F32)).astype(BF16)
    return hi, mid, lo


def _block_ref(cum, block, ref_row):
    c = cum.shape[0]
    ref = cum.reshape(c // block, block, HG_D)[:, ref_row:ref_row + 1, :]
    return jnp.broadcast_to(ref, (c // block, block, HG_D)).reshape(c, HG_D)


def _hgrn_level(cum_f, cum_b, qs, k_f, k_b, m, ti, si):
    c = cum_f.shape[0]
    ref_f = _block_ref(cum_f, 2 * m, m - 1)
    ref_b = _block_ref(cum_b, 2 * m, m)
    row = lax.broadcasted_iota(jnp.int32, (c, 1), 0)
    upper = jnp.bitwise_and(row, m) != 0
    dec_f = jnp.exp(jnp.where(upper, cum_f - ref_f, ref_f - cum_f))
    dec_b = jnp.exp(jnp.where(upper, ref_b - cum_b, cum_b - ref_b))
    qm = jnp.concatenate([jnp.where(upper, qs * dec_f, 0.0), jnp.where(upper, 0.0, qs * dec_b)], axis=1)
    km = jnp.concatenate([jnp.where(upper, 0.0, k_f * dec_f), jnp.where(upper, k_b * dec_b, 0.0)], axis=1)
    am = _dot_nt(qm.astype(BF16), km.astype(BF16))
    if 2 * m < c:
        am = jnp.where(jnp.bitwise_xor(ti, si) < 2 * m, am, 0.0)
    return am


def _hgrn_pairwise(cum, qs, k, v, reverse):
    c = cum.shape[0]
    sub_row = lax.broadcasted_iota(jnp.int32, (SUBLANES, 1), 0)
    tiles_per_sub = HG_SUB // SUBLANES
    o_tiles = []
    for blk in range(c // HG_SUB):
        accs = [jnp.zeros((SUBLANES, HG_D), F32) for _ in range(tiles_per_sub)]
        for j in range(HG_SUB):
            s = blk * HG_SUB + j
            cs, ks, vs = cum[s:s + 1, :], k[s:s + 1, :], v[s:s + 1, :]
            for t in range(tiles_per_sub):
                lo_r, hi_r = t * SUBLANES, t * SUBLANES + SUBLANES - 1
                if (lo_r > j) if reverse else (hi_r < j):
                    continue
                base = blk * HG_SUB + lo_r
                x = qs[base:base + SUBLANES, :] * ks * jnp.exp(
                    jnp.minimum(cum[base:base + SUBLANES, :] - cs, 0.0))
                a = jnp.sum(x, axis=-1, keepdims=True)
                if reverse and hi_r > j:
                    a = jnp.where(sub_row + lo_r <= j, a, 0.0)
                elif (not reverse) and lo_r < j:
                    a = jnp.where(sub_row + lo_r >= j, a, 0.0)
                accs[t] = accs[t] + a * vs
        o_tiles.extend(accs)
    return jnp.concatenate(o_tiles, axis=0)


def _hgrn_gates(z, consts):
    loglb, log1mlb, oml = consts
    e = jnp.exp(-jnp.abs(z))
    logsig = jnp.minimum(z, 0.0) - jnp.log1p(e)
    bb = log1mlb + logsig
    logf = jnp.maximum(loglb, bb) + jnp.log1p(jnp.exp(-jnp.abs(loglb - bb)))
    k = oml * (jnp.where(z >= 0.0, e, 1.0) / (1.0 + e))
    return logf, k


def _flat_offsets(cum_f, cum_b):
    return (cum_f - _block_ref(cum_f, HG_FLAT, HG_FLAT // 2 - 1),
            cum_b - _block_ref(cum_b, HG_FLAT, HG_FLAT // 2))


def _hgrn_kernel(q_ref, i_ref, zf_ref, zb_ref, g_ref, lbc_ref, nw_ref, s0_ref,
                 o_ref, s_ref, cum_ref, k_ref, qs_ref, acc_ref, qt_ref, u_ref, de_ref):
    l = q_ref.shape[1]
    c = HG_CHUNK
    nc = l // c
    consts = (lbc_ref[0:1, :], lbc_ref[1:2, :], lbc_ref[2:3, :])
    ti = lax.broadcasted_iota(jnp.int32, (c, c), 0)
    si = lax.broadcasted_iota(jnp.int32, (c, c), 1)

    def chunk_rows(i):
        return pl.ds(pl.multiple_of(i * c, c), c)

    def prepare(i, gmax):
        rows = chunk_rows(i)
        q = q_ref[0, rows, :]
        qs_ref[rows, :] = q * jax.nn.sigmoid(q)
        cums = []
        for d, z_ref in enumerate((zf_ref, zb_ref)):
            logf, k = _hgrn_gates(z_ref[0, rows, :], consts)
            k_ref[d, rows, :] = k
            tri = jnp.where((si >= ti) if d else (si <= ti), 1.0, 0.0).astype(BF16)
            parts = _dot(tri, jnp.concatenate(_split3(logf), axis=1))
            cum = parts[:, :HG_D] + parts[:, HG_D:2 * HG_D] + parts[:, 2 * HG_D:]
            cum_ref[d, rows, :] = cum
            cums.append(cum)
        dq_f, dq_b = _flat_offsets(*cums)
        g = jnp.maximum(jnp.abs(dq_f), jnp.abs(dq_b))
        return jnp.maximum(gmax, jnp.max(g.reshape(c // SUBLANES, SUBLANES, HG_D), axis=0))

    gmax = lax.fori_loop(0, nc, prepare, jnp.zeros((SUBLANES, HG_D), F32), unroll=HG_UNROLL)
    flat_ok = jnp.max(gmax) <= HG_FLAT_RANGE

    def intra(i, flat):
        rows = chunk_rows(i)
        cum_f, cum_b = cum_ref[0, rows, :], cum_ref[1, rows, :]
        k_f, k_b = k_ref[0, rows, :], k_ref[1, rows, :]
        qs = qs_ref[rows, :]
        v = i_ref[0, rows, :]
        vb = v.astype(BF16)
        a = jnp.zeros((c, c), F32)
        m = c // 2
        while m >= (HG_FLAT if flat else HG_SUB):
            a = a + _hgrn_level(cum_f, cum_b, qs, k_f, k_b, m, ti, si)
            m //= 2
        if flat:
            dq_f, dq_b = _flat_offsets(cum_f, cum_b)
            same = jnp.bitwise_xor(ti, si) < HG_FLAT
            a_f = _dot_nt((qs * jnp.exp(dq_f)).astype(BF16), (k_f * jnp.exp(-dq_f)).astype(BF16))
            a_b = _dot_nt((qs * jnp.exp(dq_b)).astype(BF16), (k_b * jnp.exp(-dq_b)).astype(BF16))
            a = a + jnp.where(same & (si <= ti), a_f, 0.0) + jnp.where(same & (si >= ti), a_b, 0.0)
            o = _dot(a.astype(BF16), vb)
        else:
            o = (_dot(a.astype(BF16), vb) + _hgrn_pairwise(cum_f, qs, k_f, v, False)
                 + _hgrn_pairwise(cum_b, qs, k_b, v, True))
        acc_ref[rows, :] = o
        edge_f, edge_b = cum_f[c - 1:c, :], cum_b[0:1, :]
        qt_ref[rows, :] = jnp.concatenate([(qs * jnp.exp(cum_f)).astype(BF16),
                                           (qs * jnp.exp(cum_b)).astype(BF16)], axis=1)
        kd = jnp.concatenate([(k_f * jnp.exp(edge_f - cum_f)).astype(BF16),
                              (k_b * jnp.exp(edge_b - cum_b)).astype(BF16)], axis=1)
        u_ref[i] = _dot_tn(vb, kd)
        de_ref[i] = jnp.broadcast_to(jnp.concatenate([jnp.exp(edge_f), jnp.exp(edge_b)], axis=1),
                                     (SUBLANES, 2 * HG_D))

    def run_intra(flat):
        def body(i, carry):
            intra(i, flat)
            return carry
        lax.fori_loop(0, nc, body, 0, unroll=HG_UNROLL if flat else 1)

    pl.when(flat_ok)(lambda: run_intra(True))
    pl.when(jnp.logical_not(flat_ok))(lambda: run_intra(False))

    def sweep(i, carry):
        st_f, st_b = carry
        j = nc - 1 - i
        inc = u_ref[i, :, :HG_D]
        u_ref[i, :, :HG_D] = st_f
        st_f = st_f * de_ref[i, 0:1, :HG_D] + inc
        inc = u_ref[j, :, HG_D:]
        u_ref[j, :, HG_D:] = st_b
        st_b = st_b * de_ref[j, 0:1, HG_D:] + inc
        return st_f, st_b

    st_f, st_b = lax.fori_loop(0, nc, sweep, (s0_ref[0, 0, 0].T, s0_ref[0, 1, 0].T))
    s_ref[0, 0, 0] = st_f.T
    s_ref[0, 1, 0] = st_b.T

    def finish(i, carry):
        rows = chunk_rows(i)
        o = acc_ref[rows, :] + _dot_nt(qt_ref[rows, :], u_ref[i].astype(BF16))
        o = o * lax.rsqrt(jnp.mean(o * o, axis=-1, keepdims=True) + EPS) * nw_ref[...]
        g = g_ref[0, rows, :]
        o_ref[0, rows, :] = (o * (g * jax.nn.sigmoid(g))).astype(BF16)
        return carry

    lax.fori_loop(0, nc, finish, 0, unroll=HG_UNROLL)


def _hgrn(hgp, lbc, norm_w, s0):
    b, l, _ = hgp.shape
    col = lambda sec: pl.BlockSpec((1, l, HG_D), lambda i, h, sec=sec: (i, 0, sec * HG_HEADS + h))
    state_spec = pl.BlockSpec((1, 2, 1, HG_D, HG_D), lambda i, h: (i, 0, h, 0, 0))
    return pl.pallas_call(
        _hgrn_kernel,
        grid=(b, HG_HEADS),
        in_specs=[col(0), col(1), col(2), col(3), col(4),
                  pl.BlockSpec((SUBLANES, HG_D), lambda i, h: (0, h)),
                  pl.BlockSpec((1, HG_D), lambda i, h: (0, 0)),
                  state_spec],
        out_specs=[pl.BlockSpec((1, l, HG_D), lambda i, h: (i, 0, h)), state_spec],
        out_shape=[jax.ShapeDtypeStruct((b, l, HG_W), BF16),
                   jax.ShapeDtypeStruct((b, 2, HG_HEADS, HG_D, HG_D), F32)],
        scratch_shapes=[pltpu.VMEM((2, l, HG_D), F32),
                        pltpu.VMEM((2, l, HG_D), F32),
                        pltpu.VMEM((l, HG_D), F32),
                        pltpu.VMEM((l, HG_D), F32),
                        pltpu.VMEM((l, 2 * HG_D), BF16),
                        pltpu.VMEM((l // HG_CHUNK, HG_D, 2 * HG_D), F32),
                        pltpu.VMEM((l // HG_CHUNK, SUBLANES, 2 * HG_D), F32)],
        compiler_params=_cparams(("parallel", "parallel")),
        name="hgrn",
    )(hgp, hgp, hgp, hgp, hgp, lbc, norm_w.reshape(1, HG_D), s0)


def _outproj_kernel(x_ref, att_ref, hg_ref, mod_ref, w_ref, o_ref):
    y = _dot(att_ref[0], w_ref[:NA_W, :]) + _dot(hg_ref[0], w_ref[NA_W:, :])
    o_ref[0] = x_ref[0] + mod_ref[0, 2:3, :] * y


def _outproj(x, att, hg, mod, w, tm=512):
    b, l, d = x.shape
    tm = min(tm, l)
    return pl.pallas_call(
        _outproj_kernel,
        grid=(b, l // tm),
        in_specs=[
            pl.BlockSpec((1, tm, d), lambda i, j: (i, j, 0)),
            pl.BlockSpec((1, tm, NA_W), lambda i, j: (i, j, 0)),
            pl.BlockSpec((1, tm, HG_W), lambda i, j: (i, j, 0)),
            pl.BlockSpec((1, SUBLANES, d), _mod_index(mod)),
            pl.BlockSpec(w.shape, lambda i, j: (0, 0)),
        ],
        out_specs=pl.BlockSpec((1, tm, d), lambda i, j: (i, j, 0)),
        out_shape=jax.ShapeDtypeStruct(x.shape, F32),
        compiler_params=_cparams(("parallel", "parallel")),
        name="outproj",
    )(x, att, hg, mod, w)


def _ffn_kernel(x_ref, nw_ref, mod_ref, w1_ref, w3_ref, w2_ref, o_ref, h_ref, acc_ref):
    f = pl.program_id(2)

    @pl.when(f == 0)
    def _():
        h = _norm_mod(x_ref[0], nw_ref[...], mod_ref[0, 3:4, :], mod_ref[0, 4:5, :])
        h_ref[...] = h.astype(BF16)
        acc_ref[...] = jnp.zeros_like(acc_ref)

    h = h_ref[...]
    a = _dot(h, w1_ref[...])
    g = (a * jax.nn.sigmoid(a)) * _dot(h, w3_ref[...])
    acc_ref[...] += _dot(g.astype(BF16), w2_ref[...])

    @pl.when(f == pl.num_programs(2) - 1)
    def _():
        o_ref[0] = x_ref[0] + mod_ref[0, 5:6, :] * acc_ref[...]


def _ffn(x, nw, mod, w1, w3, w2, tm=512, tf=1408):
    b, l, d = x.shape
    tm = min(tm, l)
    ff = w1.shape[1]
    tf = min(tf, ff)
    return pl.pallas_call(
        _ffn_kernel,
        grid=(b, l // tm, ff // tf),
        in_specs=[
            pl.BlockSpec((1, tm, d), lambda i, j, f: (i, j, 0)),
            pl.BlockSpec((1, d), lambda i, j, f: (0, 0)),
            pl.BlockSpec((1, SUBLANES, d), _mod_index(mod)),
            pl.BlockSpec((d, tf), lambda i, j, f: (0, f)),
            pl.BlockSpec((d, tf), lambda i, j, f: (0, f)),
            pl.BlockSpec((tf, d), lambda i, j, f: (f, 0)),
        ],
        out_specs=pl.BlockSpec((1, tm, d), lambda i, j, f: (i, j, 0)),
        out_shape=jax.ShapeDtypeStruct(x.shape, F32),
        scratch_shapes=[pltpu.VMEM((tm, d), BF16), pltpu.VMEM((tm, d), F32)],
        compiler_params=_cparams(("parallel", "parallel", "arbitrary")),
        name="ffn",
    )(x, nw.reshape(1, d), mod, w1, w3, w2)


HALF_MASK = 0xFFFF0000


def _pack_bf16_pairs(hb):
    n = hb.shape[1] // 2
    u = pltpu.bitcast(hb.astype(F32), jnp.uint32)
    return jnp.bitwise_or(jnp.bitwise_and(u[:, n:], jnp.uint32(HALF_MASK)),
                          lax.shift_right_logical(u[:, :n], jnp.uint32(16)))


def _unpack_bf16_pairs(u):
    lo = pltpu.bitcast(lax.shift_left(u, jnp.uint32(16)), F32)
    hi = pltpu.bitcast(jnp.bitwise_and(u, jnp.uint32(HALF_MASK)), F32)
    return jnp.concatenate([lo, hi], axis=1).astype(BF16)


def _router_kernel(x_ref, nw_ref, mod_ref, r_ref, hp_ref, idx_ref, wt_ref):
    h = _norm_mod(x_ref[0], nw_ref[...], mod_ref[0, 3:4, :], mod_ref[0, 4:5, :])
    hh = h.astype(BF16)
    hl = (h - hh.astype(F32)).astype(BF16)
    r = r_ref[...]
    rh = r.astype(BF16)
    rl = (r - rh.astype(F32)).astype(BF16)
    logits = _dot(hh, rh) + _dot(hh, rl) + _dot(hl, rh)
    lane = lax.broadcasted_iota(jnp.int32, logits.shape, 1)
    logits = jnp.where(lane < N_EXPERTS, logits, -jnp.inf)
    m1 = jnp.max(logits, axis=-1, keepdims=True)
    i1 = jnp.min(jnp.where(logits == m1, lane, LANES), axis=-1, keepdims=True)
    rest = jnp.where(lane == i1, -jnp.inf, logits)
    m2 = jnp.max(rest, axis=-1, keepdims=True)
    i2 = jnp.min(jnp.where(rest == m2, lane, LANES), axis=-1, keepdims=True)
    e2 = jnp.exp(m2 - m1)
    den = 1.0 + e2
    idx_ref[0] = jnp.where(lane == 0, i1, jnp.where(lane == 1, i2, 0))
    wt_ref[0] = jnp.where(lane == 0, 1.0 / den, jnp.where(lane == 1, e2 / den, 0.0))
    hp_ref[0] = _pack_bf16_pairs(hh)


def _router(x, nw, mod, router, tm=512):
    b, l, d = x.shape
    tm = min(tm, l)
    rp = jnp.zeros((d, LANES), F32).at[:, :router.shape[1]].set(router)
    tile = lambda w: pl.BlockSpec((1, tm, w), lambda i, j: (i, j, 0))
    return pl.pallas_call(
        _router_kernel,
        grid=(b, l // tm),
        in_specs=[
            tile(d),
            pl.BlockSpec((1, d), lambda i, j: (0, 0)),
            pl.BlockSpec((1, SUBLANES, d), _mod_index(mod)),
            pl.BlockSpec((d, LANES), lambda i, j: (0, 0)),
        ],
        out_specs=[tile(d // 2), tile(LANES), tile(LANES)],
        out_shape=[jax.ShapeDtypeStruct((b, l, d // 2), jnp.uint32),
                   jax.ShapeDtypeStruct((b, l, LANES), jnp.int32),
                   jax.ShapeDtypeStruct((b, l, LANES), F32)],
        compiler_params=_cparams(("parallel", "parallel")),
        name="router",
    )(x, nw.reshape(1, d), mod, rp)


MOE_TM = 512
MOE_TG = 256
TOP_K = 2


def _route_plan(idx, n_experts, tm):
    t = idx.shape[0]
    chosen = idx[:, :, None] == jnp.arange(n_experts, dtype=jnp.int32)[None, None, :]
    sel = jnp.any(chosen, axis=1)
    rank = jnp.cumsum(sel.astype(jnp.int32), axis=0) - 1
    count = rank[-1] + 1
    tiles = (count + tm - 1) // tm
    tile_end = jnp.cumsum(tiles)
    start = (tile_end - tiles) * tm
    dest = jnp.sum(jnp.where(chosen, (start[None, :] + rank)[:, None, :], 0), axis=-1)
    n_tiles_max = (t * TOP_K) // tm + n_experts
    used = jnp.minimum(jnp.arange(n_tiles_max, dtype=jnp.int32), tile_end[-1] - 1)
    tile_expert = jnp.sum((used[:, None] >= tile_end[None, :]).astype(jnp.int32), axis=1)
    return dest.astype(jnp.int32), tile_expert.astype(jnp.int32), tile_end[-1:].astype(jnp.int32)


def _row_copies(dest_ref, src_of, dst_of, sem, n):
    def start(r, carry):
        for s in range(TOP_K):
            row = dest_ref[0, s, r]
            pltpu.make_async_copy(src_of(s, r, row), dst_of(s, r, row), sem).start()
        return carry

    def wait(r, carry):
        for s in range(TOP_K):
            row = dest_ref[0, s, r]
            pltpu.make_async_copy(src_of(s, r, row), dst_of(s, r, row), sem).wait()
        return carry

    lax.fori_loop(0, n, start, 0, unroll=8)
    lax.fori_loop(0, n, wait, 0, unroll=8)


def _dispatch_kernel(dest_ref, hp_ref, init_ref, xs_ref, sem):
    del init_ref
    _row_copies(dest_ref,
                lambda s, r, row: hp_ref.at[pl.ds(r, 1)],
                lambda s, r, row: xs_ref.at[pl.ds(row, 1)],
                sem, hp_ref.shape[0])


def _dispatch(hp, dest_tiles, n_rows):
    t, w = hp.shape
    return pl.pallas_call(
        _dispatch_kernel,
        grid=(t // MOE_TG,),
        in_specs=[
            pl.BlockSpec((1, TOP_K, MOE_TG), lambda i: (i, 0, 0), memory_space=pltpu.SMEM),
            pl.BlockSpec((MOE_TG, w), lambda i: (i, 0)),
            pl.BlockSpec(memory_space=pl.ANY),
        ],
        out_specs=pl.BlockSpec(memory_space=pl.ANY),
        out_shape=jax.ShapeDtypeStruct((n_rows, w), jnp.uint32),
        scratch_shapes=[pltpu.SemaphoreType.DMA],
        input_output_aliases={2: 0},
        compiler_params=_cparams(("arbitrary",)),
        name="moe_dispatch",
    )(dest_tiles, hp, jnp.zeros((n_rows, w), jnp.uint32))


def _expert_kernel(te_ref, nt_ref, xs_ref, w1_ref, w3_ref, w2_ref, y_ref):
    del te_ref
    i = pl.program_id(0)

    @pl.when(i < nt_ref[0])
    def _():
        h = _unpack_bf16_pairs(xs_ref[...])
        a = _dot(h, w1_ref[0])
        g = (a * jax.nn.sigmoid(a)) * _dot(h, w3_ref[0])
        y_ref[...] = _dot(g.astype(BF16), w2_ref[0])

    @pl.when(i >= nt_ref[0])
    def _():
        y_ref[...] = jnp.zeros_like(y_ref)


def _experts(xs, tile_expert, n_tiles, w1, w3, w2):
    n_rows, w = xs.shape
    _, d, ff = w1.shape
    return pl.pallas_call(
        _expert_kernel,
        grid_spec=pltpu.PrefetchScalarGridSpec(
            num_scalar_prefetch=2,
            grid=(n_rows // MOE_TM,),
            in_specs=[
                pl.BlockSpec((MOE_TM, w), lambda i, te, nt: (i, 0)),
                pl.BlockSpec((1, d, ff), lambda i, te, nt: (te[i], 0, 0)),
                pl.BlockSpec((1, d, ff), lambda i, te, nt: (te[i], 0, 0)),
                pl.BlockSpec((1, ff, d), lambda i, te, nt: (te[i], 0, 0)),
            ],
            out_specs=pl.BlockSpec((MOE_TM, d), lambda i, te, nt: (i, 0)),
        ),
        out_shape=jax.ShapeDtypeStruct((n_rows, d), F32),
        compiler_params=_cparams(("arbitrary",)),
        name="moe_experts",
    )(tile_expert, n_tiles, xs, w1, w3, w2)


def _combine_kernel(dest_ref, x_ref, mod_ref, wt_ref, y_ref, o_ref, buf_ref, sem):
    _row_copies(dest_ref,
                lambda s, r, row: y_ref.at[pl.ds(row, 1)],
                lambda s, r, row: buf_ref.at[s, pl.ds(r, 1)],
                sem, x_ref.shape[1])
    wt = wt_ref[0]
    mixed = wt[:, 0:1] * buf_ref[0] + wt[:, 1:2] * buf_ref[1]
    o_ref[0] = x_ref[0] + mod_ref[0, 5:6, :] * mixed


def _combine(x, mod, wts, dest_tiles, y):
    b, l, d = x.shape
    per_b = l // MOE_TG
    return pl.pallas_call(
        _combine_kernel,
        grid=(b, per_b),
        in_specs=[
            pl.BlockSpec((1, TOP_K, MOE_TG), lambda i, j: (i * per_b + j, 0, 0), memory_space=pltpu.SMEM),
            pl.BlockSpec((1, MOE_TG, d), lambda i, j: (i, j, 0)),
            pl.BlockSpec((1, SUBLANES, d), _mod_index(mod)),
            pl.BlockSpec((1, MOE_TG, LANES), lambda i, j: (i, j, 0)),
            pl.BlockSpec(memory_space=pl.ANY),
        ],
        out_specs=pl.BlockSpec((1, MOE_TG, d), lambda i, j: (i, j, 0)),
        out_shape=jax.ShapeDtypeStruct(x.shape, F32),
        scratch_shapes=[pltpu.VMEM((TOP_K, MOE_TG, d), F32), pltpu.SemaphoreType.DMA],
        compiler_params=_cparams(("arbitrary", "arbitrary")),
        name="moe_combine",
    )(dest_tiles, x, mod, wts, y)


def _moe(x, nw, mod, router, w1, w3, w2):
    b, l, d = x.shape
    t = b * l
    hp, idx, wts = _router(x, nw, mod, router)
    dest, tile_expert, n_tiles = _route_plan(idx.reshape(t, LANES)[:, :TOP_K], w1.shape[0], MOE_TM)
    dest_tiles = dest.reshape(t // MOE_TG, MOE_TG, TOP_K).transpose(0, 2, 1)
    n_rows = t * TOP_K + w1.shape[0] * MOE_TM
    xs = _dispatch(hp.reshape(t, d // 2), dest_tiles, n_rows)
    y = _experts(xs, tile_expert, n_tiles, w1, w3, w2)
    return _combine(x, mod, wts, dest_tiles, y)


def _final_norm_kernel(x_ref, nw_ref, o_ref):
    x = x_ref[0]
    o_ref[0] = x * lax.rsqrt(jnp.mean(x * x, axis=-1, keepdims=True) + EPS) * nw_ref[...]


def _final_norm(x, nw, tm=512):
    b, l, d = x.shape
    tm = min(tm, l)
    return pl.pallas_call(
        _final_norm_kernel,
        grid=(b, l // tm),
        in_specs=[pl.BlockSpec((1, tm, d), lambda i, j: (i, j, 0)),
                  pl.BlockSpec((1, d), lambda i, j: (0, 0))],
        out_specs=pl.BlockSpec((1, tm, d), lambda i, j: (i, j, 0)),
        out_shape=jax.ShapeDtypeStruct(x.shape, F32),
        compiler_params=_cparams(("parallel", "parallel")),
        name="final_norm",
    )(x, nw.reshape(1, d))


def _pad_rows(a, rows):
    return jnp.zeros((rows,) + a.shape[1:], a.dtype).at[:a.shape[0]].set(a)


def kernel(x_prompt, x_sample, cache_k, cache_v, state_hgrn, c, c_ctx, norm1_w, norm2_w, w_ada, b_ada,
           w_in, rpb, hg_lower, hg_norm_w, w_out, ffn_w1, ffn_w3, ffn_w2, moe_router, moe_w1, moe_w3,
           moe_w2, final_norm_w):
    depth = w_in.shape[0]
    d = x_prompt.shape[-1]
    nb = x_prompt.shape[0]
    nd = x_sample.shape[0]
    rows = x_sample.shape[1] // GRID_W

    lbs = jnp.cumsum(jax.nn.softmax(hg_lower.astype(F32), axis=0), axis=0)
    lbs = lbs - lbs[0:1]
    lbc = _pad_rows(jnp.stack([jnp.log(lbs), jnp.log1p(-lbs), 1.0 - lbs], axis=1).transpose(1, 0, 2),
                    SUBLANES).transpose(1, 0, 2)

    cond = _pad_rows(jnp.concatenate([c, c_ctx[None, :]], axis=0), SUBLANES)
    mods = _modulation(cond, w_ada, b_ada)
    mods = _pad_rows(mods.reshape(depth, SUBLANES, 6, d).transpose(2, 0, 1, 3),
                     SUBLANES).transpose(1, 2, 0, 3)

    w_in_b = w_in.astype(BF16)
    w_out_b = w_out.astype(BF16)
    ffn_w1_b, ffn_w3_b, ffn_w2_b = ffn_w1.astype(BF16), ffn_w3.astype(BF16), ffn_w2.astype(BF16)
    moe_w1_b, moe_w3_b, moe_w2_b = moe_w1.astype(BF16), moe_w3.astype(BF16), moe_w2.astype(BF16)
    wkt_b = jnp.swapaxes(w_in[:, :, NA_W:2 * NA_W], 1, 2).astype(BF16)
    lc = cache_k.shape[2]
    ckt_b = jnp.swapaxes(cache_k.astype(BF16).reshape(nd, depth, lc, NA_W), 2, 3)
    cv_b = cache_v.astype(BF16).reshape(nd, depth, lc, NA_W)
    zero_state = jnp.zeros((nb, 2, HG_HEADS, HG_D, HG_D), F32)

    def layer(x, seq_shape, mod, l, attend, s0, wkt):
        tok = lambda a: a.reshape(x.shape[:2] + a.shape[-1:])
        seq = lambda a: a.reshape(seq_shape + a.shape[-1:])
        want_kv = wkt is None
        outs = _inproj(x, norm1_w[l], mod, w_in_b[l], wkt)
        att = attend(seq(outs[0]), outs[2])
        hg, s_fin = _hgrn(seq(outs[1]), lbc[l], hg_norm_w[l], s0)
        x = _outproj(x, tok(att), tok(hg), mod, w_out_b[l])
        i = l // 2
        if l % 2 == 0:
            x = _ffn(x, norm2_w[l], mod, ffn_w1_b[i], ffn_w3_b[i], ffn_w2_b[i])
        else:
            x = _moe(x, norm2_w[l], mod, moe_router[i], moe_w1_b[i], moe_w3_b[i], moe_w2_b[i])
        return x, (seq(outs[2]) if want_kv else None), s_fin

    seq = x_prompt.shape[1]
    xp, xs = x_prompt.reshape(1, nb * seq, d), x_sample
    ks_out, vs_out, ss_out = [], [], []
    for l in range(depth):
        mod_p = mods[l, nd:nd + 1]
        mod_s = mods[l, :nd]
        xp, kv, s_fin = layer(xp, (nb, seq), mod_p, l, lambda qkv, _: _ctx_attention(qkv),
                              zero_state, None)
        ks_out.append(kv[..., :NA_W])
        vs_out.append(kv[..., NA_W:])
        ss_out.append(s_fin)
        bias = _na_bias_slabs(rpb[l], rows)
        attend_lat = lambda qkv, kt, l=l, bias=bias: _na_attention(qkv, kt, ckt_b[:, l], cv_b[:, l], bias)
        xs, _, _ = layer(xs, x_sample.shape[:2], mod_s, l, attend_lat,
                         state_hgrn[:, l].astype(F32), wkt_b[l])

    y_prompt = _final_norm(xp, final_norm_w).reshape(x_prompt.shape)
    y_sample = _final_norm(xs, final_norm_w)
    new_cache_k = jnp.stack(ks_out, axis=1).reshape(nb, depth, seq, NA_HEADS, NA_DH)
    new_cache_v = jnp.stack(vs_out, axis=1).reshape(nb, depth, seq, NA_HEADS, NA_DH)
    new_state = jnp.stack(ss_out, axis=1).astype(x_prompt.dtype)
    return (y_prompt, y_sample, new_cache_k, new_cache_v, new_state)
```

```python
import functools

import numpy as np
import jax
import jax.numpy as jnp
from jax import lax
from jax.experimental import pallas as pl
from jax.experimental.pallas import tpu as pltpu

F32 = jnp.float32
BF16 = jnp.bfloat16

EPS = 1e-6
NA_HEADS = 8
NA_DH = 64
NA_W = NA_HEADS * NA_DH
GRID_W = 64
WIN_R = 8
WIN_C = 16
HG_HEADS = 4
HG_D = 128
HG_W = HG_HEADS * HG_D
N_EXPERTS = 8
LANES = 128
SUBLANES = 8
VMEM_LIMIT = 56 * 1024 * 1024

HG_CHUNK = 128
HG_SUB = 16
HG_UNROLL = 4
HG_FLAT = 32
HG_FLAT_RANGE = 60.0


def _cparams(sem):
    return pltpu.CompilerParams(dimension_semantics=sem, vmem_limit_bytes=VMEM_LIMIT)


def _dot(a, b):
    return jnp.dot(a, b, preferred_element_type=F32)


def _dot_nt(a, b):
    return lax.dot_general(a, b, (((1,), (1,)), ((), ())), preferred_element_type=F32)


def _dot_tn(a, b):
    return lax.dot_general(a, b, (((0,), (0,)), ((), ())), preferred_element_type=F32)


def _norm_mod(x, nw, shift, scale):
    ms = jnp.mean(x * x, axis=-1, keepdims=True)
    y = x * lax.rsqrt(ms + EPS) * nw
    return y * (1.0 + scale) + shift


def _mod_kernel(cond_ref, w_ref, b_ref, o_ref):
    c = cond_ref[...]
    s = c * jax.nn.sigmoid(c)
    o_ref[0] = _dot(s.astype(BF16), w_ref[0].astype(BF16)) + b_ref[0]


def _modulation(cond, w_ada, b_ada, tn=1536):
    depth, d, n = w_ada.shape
    rows = cond.shape[0]
    return pl.pallas_call(
        _mod_kernel,
        grid=(depth, n // tn),
        in_specs=[
            pl.BlockSpec((rows, d), lambda l, j: (0, 0)),
            pl.BlockSpec((1, d, tn), lambda l, j: (l, 0, j)),
            pl.BlockSpec((1, 1, tn), lambda l, j: (l, 0, j)),
        ],
        out_specs=pl.BlockSpec((1, rows, tn), lambda l, j: (l, 0, j)),
        out_shape=jax.ShapeDtypeStruct((depth, rows, n), F32),
        compiler_params=_cparams(("parallel", "parallel")),
        name="modulation",
    )(cond, w_ada, b_ada.reshape(depth, 1, n))


def _mod_index(mod):
    if mod.shape[0] == 1:
        return lambda b, *_: (0, 0, 0)
    return lambda b, *_: (b, 0, 0)


def _inproj_kernel(x_ref, nw_ref, mod_ref, w_ref, *refs, want_kt):
    if want_kt:
        wkt_ref, qkv_ref, hg_ref, extra_ref = refs
    else:
        qkv_ref, hg_ref, extra_ref = refs
    h = _norm_mod(x_ref[0], nw_ref[...], mod_ref[0, 0:1, :], mod_ref[0, 1:2, :]).astype(BF16)
    p = _dot(h, w_ref[...])
    qkv_ref[0] = p[:, :3 * NA_W].astype(BF16)
    hg_ref[0] = p[:, 3 * NA_W:]
    if want_kt:
        extra_ref[0] = _dot_nt(wkt_ref[...], h).astype(BF16)
    else:
        extra_ref[0] = p[:, NA_W:3 * NA_W]


def _inproj(x, nw, mod, w, wkt=None, tm=512):
    b, l, d = x.shape
    n = w.shape[1]
    want_kt = wkt is not None
    out_shape = [jax.ShapeDtypeStruct((b, l, 3 * NA_W), BF16),
                 jax.ShapeDtypeStruct((b, l, n - 3 * NA_W), F32)]
    out_specs = [pl.BlockSpec((1, tm, 3 * NA_W), lambda i, j: (i, j, 0)),
                 pl.BlockSpec((1, tm, n - 3 * NA_W), lambda i, j: (i, j, 0))]
    in_specs = [pl.BlockSpec((1, tm, d), lambda i, j: (i, j, 0)),
                pl.BlockSpec((1, d), lambda i, j: (0, 0)),
                pl.BlockSpec((1, SUBLANES, d), _mod_index(mod)),
                pl.BlockSpec((d, n), lambda i, j: (0, 0))]
    args = [x, nw.reshape(1, d), mod, w]
    if want_kt:
        in_specs.append(pl.BlockSpec((NA_W, d), lambda i, j: (0, 0)))
        args.append(wkt)
        out_shape.append(jax.ShapeDtypeStruct((b, NA_W, l), BF16))
        out_specs.append(pl.BlockSpec((1, NA_W, tm), lambda i, j: (i, 0, j)))
    else:
        out_shape.append(jax.ShapeDtypeStruct((b, l, 2 * NA_W), F32))
        out_specs.append(pl.BlockSpec((1, tm, 2 * NA_W), lambda i, j: (i, j, 0)))
    return pl.pallas_call(
        functools.partial(_inproj_kernel, want_kt=want_kt),
        grid=(b, l // tm),
        in_specs=in_specs,
        out_specs=out_specs,
        out_shape=out_shape,
        compiler_params=_cparams(("parallel", "parallel")),
        name="inproj",
    )(*args)


def _ctx_attn_kernel(q_ref, k_ref, v_ref, o_ref):
    scale = NA_DH ** -0.5
    for h in range(NA_HEADS):
        sl = slice(h * NA_DH, (h + 1) * NA_DH)
        q = q_ref[0, :, sl] * scale
        s = _dot_nt(q, k_ref[0, :, sl])
        m = jnp.max(s, axis=-1, keepdims=True)
        p = jnp.exp(s - m)
        den = jnp.sum(p, axis=-1, keepdims=True)
        o = _dot(p.astype(BF16), v_ref[0, :, sl]) / den
        o_ref[0, :, sl] = o.astype(BF16)


def _ctx_attention(qkv):
    b, l, _ = qkv.shape
    spec = lambda c: pl.BlockSpec((1, l, NA_W), lambda i, c=c: (i, 0, c))
    return pl.pallas_call(
        _ctx_attn_kernel,
        grid=(b,),
        in_specs=[spec(0), spec(1), spec(2)],
        out_specs=pl.BlockSpec((1, l, NA_W), lambda i: (i, 0, 0)),
        out_shape=jax.ShapeDtypeStruct((b, l, NA_W), BF16),
        compiler_params=_cparams(("parallel",)),
        name="ctx_attention",
    )(qkv, qkv, qkv)


NA_QROWS = 2
NA_KROWS = WIN_R + NA_QROWS


def _na_window_start(p, rows):
    return jnp.clip(NA_QROWS * p - WIN_R // 2, 0, rows - NA_KROWS)


def _na_kernel(q_ref, kt_ref, v_ref, ckt_ref, cv_ref, bias_ref, o_ref, s_ref, *, rows):
    ws = _na_window_start(pl.program_id(1), rows)
    start = pl.multiple_of(ws * GRID_W, NA_QROWS * GRID_W)
    n_loc = NA_KROWS * GRID_W
    scale = NA_DH ** -0.5
    maxes = []
    for h in range(NA_HEADS):
        sl = slice(h * NA_DH, (h + 1) * NA_DH)
        q = q_ref[0, :, sl] * scale
        s_loc = _dot(q, kt_ref[0, sl, pl.ds(start, n_loc)]) + bias_ref[0, h]
        s_ctx = _dot(q, ckt_ref[0, sl, :])
        s_ref[h, :, :n_loc] = s_loc
        s_ref[h, :, n_loc:] = s_ctx
        maxes.append(jnp.maximum(jnp.max(s_loc, axis=-1, keepdims=True),
                                 jnp.max(s_ctx, axis=-1, keepdims=True)))
    for h in range(NA_HEADS):
        sl = slice(h * NA_DH, (h + 1) * NA_DH)
        p_loc = jnp.exp(s_ref[h, :, :n_loc] - maxes[h])
        p_ctx = jnp.exp(s_ref[h, :, n_loc:] - maxes[h])
        den = jnp.sum(p_loc, axis=-1, keepdims=True) + jnp.sum(p_ctx, axis=-1, keepdims=True)
        o = (_dot(p_loc.astype(BF16), v_ref[0, pl.ds(start, n_loc), sl])
             + _dot(p_ctx.astype(BF16), cv_ref[0, :, sl])) / den
        o_ref[0, :, sl] = o.astype(BF16)


NA_EDGE = 2


def _na_variant(p, n_pairs):
    return jnp.where(p < NA_EDGE, p, jnp.where(p >= n_pairs - NA_EDGE, p - (n_pairs - 2 * NA_EDGE - 1), NA_EDGE))


def _na_bias_slabs(rpb, rows):
    n_pairs = rows // NA_QROWS
    ps = np.array([0, 1, 2, n_pairs - 2, n_pairs - 1])
    r = NA_QROWS * ps[:, None] + np.arange(NA_QROWS)[None, :]
    rs = np.clip(r - WIN_R // 2, 0, rows - WIN_R)
    ws = np.clip(NA_QROWS * ps - WIN_R // 2, 0, rows - NA_KROWS)
    krow = ws[:, None] + np.arange(NA_KROWS)[None, :]
    row_ok = (krow[:, None, :] >= rs[:, :, None]) & (krow[:, None, :] < rs[:, :, None] + WIN_R)
    dr = np.clip(krow[:, None, :] - r[:, :, None] + (WIN_R - 1), 0, 2 * WIN_R - 2)
    j = np.arange(GRID_W)[:, None]
    kc = np.arange(GRID_W)[None, :]
    cs = np.clip(j - WIN_C // 2, 0, GRID_W - WIN_C)
    col_ok = (kc >= cs) & (kc < cs + WIN_C)
    dc = np.clip(kc - j + (WIN_C - 1), 0, 2 * WIN_C - 2)
    planes = jnp.where(jnp.asarray(col_ok)[None, None], rpb.astype(F32)[:, :, dc], -jnp.inf)
    tab = jnp.take(planes, jnp.asarray(dr.reshape(-1)), axis=1)
    tab = tab.reshape((NA_HEADS,) + dr.shape + (GRID_W, GRID_W))
    tab = jnp.where(jnp.asarray(row_ok)[None, :, :, :, None, None], tab, -jnp.inf)
    tab = jnp.transpose(tab, (1, 0, 2, 4, 3, 5))
    return tab.reshape(len(ps), NA_HEADS, NA_QROWS * GRID_W, NA_KROWS * GRID_W)


def _na_attention(qkv, kt, ckt, cv, bias):
    b, l, _ = qkv.shape
    rows = l // GRID_W
    n_pairs = rows // NA_QROWS
    lc = cv.shape[1]
    tq = NA_QROWS * GRID_W
    n_loc = NA_KROWS * GRID_W
    return pl.pallas_call(
        functools.partial(_na_kernel, rows=rows),
        grid=(b, n_pairs),
        in_specs=[
            pl.BlockSpec((1, tq, NA_W), lambda i, p: (i, p, 0)),
            pl.BlockSpec((1, NA_W, l), lambda i, p: (i, 0, 0)),
            pl.BlockSpec((1, l, NA_W), lambda i, p: (i, 0, 2)),
            pl.BlockSpec((1, NA_W, lc), lambda i, p: (i, 0, 0)),
            pl.BlockSpec((1, lc, NA_W), lambda i, p: (i, 0, 0)),
            pl.BlockSpec((1, NA_HEADS, tq, n_loc), lambda i, p: (_na_variant(p, n_pairs), 0, 0, 0)),
        ],
        out_specs=pl.BlockSpec((1, tq, NA_W), lambda i, p: (i, p, 0)),
        out_shape=jax.ShapeDtypeStruct((b, l, NA_W), BF16),
        scratch_shapes=[pltpu.VMEM((NA_HEADS, tq, n_loc + lc), F32)],
        compiler_params=_cparams(("parallel", "arbitrary")),
        name="na_attention",
    )(qkv, kt, qkv, ckt, cv, bias)


def _split3(x):
    hi = x.astype(BF16)
    r1 = x - hi.astype(F32)
    mid = r1.astype(BF16)
    lo = (r1 - mid.astype(F32)).astype(BF16)
    return hi, mid, lo


def _block_ref(cum, block, ref_row):
    c = cum.shape[0]
    ref = cum.reshape(c // block, block, HG_D)[:, ref_row:ref_row + 1, :]
    return jnp.broadcast_to(ref, (c // block, block, HG_D)).reshape(c, HG_D)


def _hgrn_level(cum_f, cum_b, qs, k_f, k_b, m, ti, si):
    c = cum_f.shape[0]
    ref_f = _block_ref(cum_f, 2 * m, m - 1)
    ref_b = _block_ref(cum_b, 2 * m, m)
    row = lax.broadcasted_iota(jnp.int32, (c, 1), 0)
    upper = jnp.bitwise_and(row, m) != 0
    dec_f = jnp.exp(jnp.where(upper, cum_f - ref_f, ref_f - cum_f))
    dec_b = jnp.exp(jnp.where(upper, ref_b - cum_b, cum_b - ref_b))
    qm = jnp.concatenate([jnp.where(upper, qs * dec_f, 0.0), jnp.where(upper, 0.0, qs * dec_b)], axis=1)
    km = jnp.concatenate([jnp.where(upper, 0.0, k_f * dec_f), jnp.where(upper, k_b * dec_b, 0.0)], axis=1)
    am = _dot_nt(qm.astype(BF16), km.astype(BF16))
    if 2 * m < c:
        am = jnp.where(jnp.bitwise_xor(ti, si) < 2 * m, am, 0.0)
    return am


def _hgrn_pairwise(cum, qs, k, v, reverse):
    c = cum.shape[0]
    sub_row = lax.broadcasted_iota(jnp.int32, (SUBLANES, 1), 0)
    tiles_per_sub = HG_SUB // SUBLANES
    o_tiles = []
    for blk in range(c // HG_SUB):
        accs = [jnp.zeros((SUBLANES, HG_D), F32) for _ in range(tiles_per_sub)]
        for j in range(HG_SUB):
            s = blk * HG_SUB + j
            cs, ks, vs = cum[s:s + 1, :], k[s:s + 1, :], v[s:s + 1, :]
            for t in range(tiles_per_sub):
                lo_r, hi_r = t * SUBLANES, t * SUBLANES + SUBLANES - 1
                if (lo_r > j) if reverse else (hi_r < j):
                    continue
                base = blk * HG_SUB + lo_r
                x = qs[base:base + SUBLANES, :] * ks * jnp.exp(
                    jnp.minimum(cum[base:base + SUBLANES, :] - cs, 0.0))
                a = jnp.sum(x, axis=-1, keepdims=True)
                if reverse and hi_r > j:
                    a = jnp.where(sub_row + lo_r <= j, a, 0.0)
                elif (not reverse) and lo_r < j:
                    a = jnp.where(sub_row + lo_r >= j, a, 0.0)
                accs[t] = accs[t] + a * vs
        o_tiles.extend(accs)
    return jnp.concatenate(o_tiles, axis=0)


def _hgrn_gates(z, consts):
    loglb, log1mlb, oml = consts
    e = jnp.exp(-jnp.abs(z))
    one_e = 1.0 + e
    logsig = jnp.minimum(z, 0.0) - jnp.log(one_e)
    bb = log1mlb + logsig
    logf = jnp.maximum(loglb, bb) + jnp.log(1.0 + jnp.exp(-jnp.abs(loglb - bb)))
    k = oml * (jnp.where(z >= 0.0, e, 1.0) / one_e)
    return logf, k


def _flat_offsets(cum_f, cum_b):
    return (cum_f - _block_ref(cum_f, HG_FLAT, HG_FLAT // 2 - 1),
            cum_b - _block_ref(cum_b, HG_FLAT, HG_FLAT // 2))


def _hgrn_kernel(q_ref, i_ref, zf_ref, zb_ref, g_ref, lbc_ref, nw_ref, s0_ref,
                 o_ref, s_ref, cum_ref, k_ref, qs_ref, acc_ref, qt_ref, u_ref, de_ref):
    l = q_ref.shape[1]
    c = HG_CHUNK
    nc = l // c
    consts = (lbc_ref[0:1, :], lbc_ref[1:2, :], lbc_ref[2:3, :])
    ti = lax.broadcasted_iota(jnp.int32, (c, c), 0)
    si = lax.broadcasted_iota(jnp.int32, (c, c), 1)

    def chunk_rows(i):
        return pl.ds(pl.multiple_of(i * c, c), c)

    def prepare(i, gmax):
        rows = chunk_rows(i)
        q = q_ref[0, rows, :]
        qs_ref[rows, :] = q * jax.nn.sigmoid(q)
        cums = []
        for d, z_ref in enumerate((zf_ref, zb_ref)):
            logf, k = _hgrn_gates(z_ref[0, rows, :], consts)
            k_ref[d, rows, :] = k
            tri = jnp.where((si >= ti) if d else (si <= ti), 1.0, 0.0).astype(BF16)
            parts = _dot(tri, jnp.concatenate(_split3(logf), axis=1))
            cum = parts[:, :HG_D] + parts[:, HG_D:2 * HG_D] + parts[:, 2 * HG_D:]
            cum_ref[d, rows, :] = cum
            cums.append(cum)
        dq_f, dq_b = _flat_offsets(*cums)
        g = jnp.maximum(jnp.abs(dq_f), jnp.abs(dq_b))
        return jnp.maximum(gmax, jnp.max(g.reshape(c // SUBLANES, SUBLANES, HG_D), axis=0))

    gmax = lax.fori_loop(0, nc, prepare, jnp.zeros((SUBLANES, HG_D), F32), unroll=HG_UNROLL)
    flat_ok = jnp.max(gmax) <= HG_FLAT_RANGE

    def intra(i, flat):
        rows = chunk_rows(i)
        cum_f, cum_b = cum_ref[0, rows, :], cum_ref[1, rows, :]
        k_f, k_b = k_ref[0, rows, :], k_ref[1, rows, :]
        qs = qs_ref[rows, :]
        v = i_ref[0, rows, :]
        vb = v.astype(BF16)
        a = jnp.zeros((c, c), F32)
        m = c // 2
        while m >= (HG_FLAT if flat else HG_SUB):
            a = a + _hgrn_level(cum_f, cum_b, qs, k_f, k_b, m, ti, si)
            m //= 2
        if flat:
            dq_f, dq_b = _flat_offsets(cum_f, cum_b)
            same = jnp.bitwise_xor(ti, si) < HG_FLAT
            a_f = _dot_nt((qs * jnp.exp(dq_f)).astype(BF16), (k_f * jnp.exp(-dq_f)).astype(BF16))
            a_b = _dot_nt((qs * jnp.exp(dq_b)).astype(BF16), (k_b * jnp.exp(-dq_b)).astype(BF16))
            a = a + jnp.where(same & (si <= ti), a_f, 0.0) + jnp.where(same & (si >= ti), a_b, 0.0)
            o = _dot(a.astype(BF16), vb)
        else:
            o = (_dot(a.astype(BF16), vb) + _hgrn_pairwise(cum_f, qs, k_f, v, False)
                 + _hgrn_pairwise(cum_b, qs, k_b, v, True))
        acc_ref[rows, :] = o
        edge_f, edge_b = cum_f[c - 1:c, :], cum_b[0:1, :]
        qt_ref[rows, :] = jnp.concatenate([(qs * jnp.exp(cum_f)).astype(BF16),
                                           (qs * jnp.exp(cum_b)).astype(BF16)], axis=1)
        kd = jnp.concatenate([(k_f * jnp.exp(edge_f - cum_f)).astype(BF16),
                              (k_b * jnp.exp(edge_b - cum_b)).astype(BF16)], axis=1)
        u_ref[i] = _dot_tn(vb, kd)
        de_ref[i] = jnp.broadcast_to(jnp.concatenate([jnp.exp(edge_f), jnp.exp(edge_b)], axis=1),
                                     (SUBLANES, 2 * HG_D))

    def run_intra(flat):
        def body(i, carry):
            intra(i, flat)
            return carry
        lax.fori_loop(0, nc, body, 0, unroll=HG_UNROLL if flat else 1)

    pl.when(flat_ok)(lambda: run_intra(True))
    pl.when(jnp.logical_not(flat_ok))(lambda: run_intra(False))

    def sweep(i, carry):
        st_f, st_b = carry
        j = nc - 1 - i
        inc = u_ref[i, :, :HG_D]
        u_ref[i, :, :HG_D] = st_f
        st_f = st_f * de_ref[i, 0:1, :HG_D] + inc
        inc = u_ref[j, :, HG_D:]
        u_ref[j, :, HG_D:] = st_b
        st_b = st_b * de_ref[j, 0:1, HG_D:] + inc
        return st_f, st_b

    st_f, st_b = lax.fori_loop(0, nc, sweep, (s0_ref[0, 0, 0].T, s0_ref[0, 1, 0].T))
    s_ref[0, 0, 0] = st_f.T
    s_ref[0, 1, 0] = st_b.T

    def finish(i, carry):
        rows = chunk_rows(i)
        o = acc_ref[rows, :] + _dot_nt(qt_ref[rows, :], u_ref[i].astype(BF16))
        o = o * lax.rsqrt(jnp.mean(o * o, axis=-1, keepdims=True) + EPS) * nw_ref[...]
        g = g_ref[0, rows, :]
        o_ref[0, rows, :] = (o * (g * jax.nn.sigmoid(g))).astype(BF16)
        return carry

    lax.fori_loop(0, nc, finish, 0, unroll=HG_UNROLL)


def _hgrn(hgp, lbc, norm_w, s0):
    b, l, _ = hgp.shape
    col = lambda sec: pl.BlockSpec((1, l, HG_D), lambda i, h, sec=sec: (i, 0, sec * HG_HEADS + h))
    state_spec = pl.BlockSpec((1, 2, 1, HG_D, HG_D), lambda i, h: (i, 0, h, 0, 0))
    return pl.pallas_call(
        _hgrn_kernel,
        grid=(b, HG_HEADS),
        in_specs=[col(0), col(1), col(2), col(3), col(4),
                  pl.BlockSpec((SUBLANES, HG_D), lambda i, h: (0, h)),
                  pl.BlockSpec((1, HG_D), lambda i, h: (0, 0)),
                  state_spec],
        out_specs=[pl.BlockSpec((1, l, HG_D), lambda i, h: (i, 0, h)), state_spec],
        out_shape=[jax.ShapeDtypeStruct((b, l, HG_W), BF16),
                   jax.ShapeDtypeStruct((b, 2, HG_HEADS, HG_D, HG_D), F32)],
        scratch_shapes=[pltpu.VMEM((2, l, HG_D), F32),
                        pltpu.VMEM((2, l, HG_D), F32),
                        pltpu.VMEM((l, HG_D), F32),
                        pltpu.VMEM((l, HG_D), F32),
                        pltpu.VMEM((l, 2 * HG_D), BF16),
                        pltpu.VMEM((l // HG_CHUNK, HG_D, 2 * HG_D), F32),
                        pltpu.VMEM((l // HG_CHUNK, SUBLANES, 2 * HG_D), F32)],
        compiler_params=_cparams(("parallel", "parallel")),
        name="hgrn",
    )(hgp, hgp, hgp, hgp, hgp, lbc, norm_w.reshape(1, HG_D), s0)


def _outproj_kernel(x_ref, att_ref, hg_ref, mod_ref, w_ref, o_ref):
    y = _dot(att_ref[0], w_ref[:NA_W, :]) + _dot(hg_ref[0], w_ref[NA_W:, :])
    o_ref[0] = x_ref[0] + mod_ref[0, 2:3, :] * y


def _outproj(x, att, hg, mod, w, tm=512):
    b, l, d = x.shape
    tm = min(tm, l)
    return pl.pallas_call(
        _outproj_kernel,
        grid=(b, l // tm),
        in_specs=[
            pl.BlockSpec((1, tm, d), lambda i, j: (i, j, 0)),
            pl.BlockSpec((1, tm, NA_W), lambda i, j: (i, j, 0)),
            pl.BlockSpec((1, tm, HG_W), lambda i, j: (i, j, 0)),
            pl.BlockSpec((1, SUBLANES, d), _mod_index(mod)),
            pl.BlockSpec(w.shape, lambda i, j: (0, 0)),
        ],
        out_specs=pl.BlockSpec((1, tm, d), lambda i, j: (i, j, 0)),
        out_shape=jax.ShapeDtypeStruct(x.shape, F32),
        compiler_params=_cparams(("parallel", "parallel")),
        name="outproj",
    )(x, att, hg, mod, w)


def _ffn_kernel(x_ref, nw_ref, mod_ref, w1_ref, w3_ref, w2_ref, o_ref, h_ref, acc_ref):
    f = pl.program_id(2)

    @pl.when(f == 0)
    def _():
        h = _norm_mod(x_ref[0], nw_ref[...], mod_ref[0, 3:4, :], mod_ref[0, 4:5, :])
        h_ref[...] = h.astype(BF16)
        acc_ref[...] = jnp.zeros_like(acc_ref)

    h = h_ref[...]
    a = _dot(h, w1_ref[...])
    g = (a * jax.nn.sigmoid(a)) * _dot(h, w3_ref[...])
    acc_ref[...] += _dot(g.astype(BF16), w2_ref[...])

    @pl.when(f == pl.num_programs(2) - 1)
    def _():
        o_ref[0] = x_ref[0] + mod_ref[0, 5:6, :] * acc_ref[...]


def _ffn(x, nw, mod, w1, w3, w2, tm=512, tf=1408):
    b, l, d = x.shape
    tm = min(tm, l)
    ff = w1.shape[1]
    tf = min(tf, ff)
    return pl.pallas_call(
        _ffn_kernel,
        grid=(b, l // tm, ff // tf),
        in_specs=[
            pl.BlockSpec((1, tm, d), lambda i, j, f: (i, j, 0)),
            pl.BlockSpec((1, d), lambda i, j, f: (0, 0)),
            pl.BlockSpec((1, SUBLANES, d), _mod_index(mod)),
            pl.BlockSpec((d, tf), lambda i, j, f: (0, f)),
            pl.BlockSpec((d, tf), lambda i, j, f: (0, f)),
            pl.BlockSpec((tf, d), lambda i, j, f: (f, 0)),
        ],
        out_specs=pl.BlockSpec((1, tm, d), lambda i, j, f: (i, j, 0)),
        out_shape=jax.ShapeDtypeStruct(x.shape, F32),
        scratch_shapes=[pltpu.VMEM((tm, d), BF16), pltpu.VMEM((tm, d), F32)],
        compiler_params=_cparams(("parallel", "parallel", "arbitrary")),
        name="ffn",
    )(x, nw.reshape(1, d), mod, w1, w3, w2)


def _router_kernel(x_ref, nw_ref, mod_ref, r_ref, hp_ref, idx_ref, wt_ref):
    h = _norm_mod(x_ref[0], nw_ref[...], mod_ref[0, 3:4, :], mod_ref[0, 4:5, :])
    hh = h.astype(BF16)
    hl = (h - hh.astype(F32)).astype(BF16)
    r = r_ref[...]
    rh = r.astype(BF16)
    rl = (r - rh.astype(F32)).astype(BF16)
    logits = _dot(hh, rh) + _dot(hh, rl) + _dot(hl, rh)
    lane = lax.broadcasted_iota(jnp.int32, logits.shape, 1)
    logits = jnp.where(lane < N_EXPERTS, logits, -jnp.inf)
    m1 = jnp.max(logits, axis=-1, keepdims=True)
    i1 = jnp.min(jnp.where(logits == m1, lane, LANES), axis=-1, keepdims=True)
    rest = jnp.where(lane == i1, -jnp.inf, logits)
    m2 = jnp.max(rest, axis=-1, keepdims=True)
    i2 = jnp.min(jnp.where(rest == m2, lane, LANES), axis=-1, keepdims=True)
    e2 = jnp.exp(m2 - m1)
    den = 1.0 + e2
    idx_ref[0] = jnp.where(lane == 0, i1, jnp.where(lane == 1, i2, 0))
    wt_ref[0] = jnp.where(lane == 0, 1.0 / den, jnp.where(lane == 1, e2 / den, 0.0))
    hp_ref[0] = h


def _router(x, nw, mod, router, tm=512):
    b, l, d = x.shape
    tm = min(tm, l)
    rp = jnp.zeros((d, LANES), F32).at[:, :router.shape[1]].set(router)
    tile = lambda w: pl.BlockSpec((1, tm, w), lambda i, j: (i, j, 0))
    return pl.pallas_call(
        _router_kernel,
        grid=(b, l // tm),
        in_specs=[
            tile(d),
            pl.BlockSpec((1, d), lambda i, j: (0, 0)),
            pl.BlockSpec((1, SUBLANES, d), _mod_index(mod)),
            pl.BlockSpec((d, LANES), lambda i, j: (0, 0)),
        ],
        out_specs=[tile(d), tile(LANES), tile(LANES)],
        out_shape=[jax.ShapeDtypeStruct((b, l, d), F32),
                   jax.ShapeDtypeStruct((b, l, LANES), jnp.int32),
                   jax.ShapeDtypeStruct((b, l, LANES), F32)],
        compiler_params=_cparams(("parallel", "parallel")),
        name="router",
    )(x, nw.reshape(1, d), mod, rp)


MOE_TM = 512
MOE_TG = 256
TOP_K = 2


def _route_plan(idx, n_experts, tm):
    t = idx.shape[0]
    chosen = idx[:, :, None] == jnp.arange(n_experts, dtype=jnp.int32)[None, None, :]
    sel = jnp.any(chosen, axis=1)
    rank = jnp.cumsum(sel.astype(jnp.int32), axis=0) - 1
    count = rank[-1] + 1
    tiles = (count + tm - 1) // tm
    tile_end = jnp.cumsum(tiles)
    start = (tile_end - tiles) * tm
    dest = jnp.sum(jnp.where(chosen, (start[None, :] + rank)[:, None, :], 0), axis=-1)
    n_tiles_max = (t * TOP_K) // tm + n_experts
    used = jnp.minimum(jnp.arange(n_tiles_max, dtype=jnp.int32), tile_end[-1] - 1)
    tile_expert = jnp.sum((used[:, None] >= tile_end[None, :]).astype(jnp.int32), axis=1)
    return dest.astype(jnp.int32), tile_expert.astype(jnp.int32), tile_end[-1:].astype(jnp.int32)


def _row_copies(dest_ref, src_of, dst_of, sem, n):
    def start(r, carry):
        for s in range(TOP_K):
            row = dest_ref[0, s, r]
            pltpu.make_async_copy(src_of(s, r, row), dst_of(s, r, row), sem).start()
        return carry

    def wait(r, carry):
        for s in range(TOP_K):
            row = dest_ref[0, s, r]
            pltpu.make_async_copy(src_of(s, r, row), dst_of(s, r, row), sem).wait()
        return carry

    lax.fori_loop(0, n, start, 0, unroll=8)
    lax.fori_loop(0, n, wait, 0, unroll=8)


def _dispatch_kernel(dest_ref, hp_ref, init_ref, xs_ref, sem):
    del init_ref
    _row_copies(dest_ref,
                lambda s, r, row: hp_ref.at[pl.ds(r, 1)],
                lambda s, r, row: xs_ref.at[pl.ds(row, 1)],
                sem, hp_ref.shape[0])


def _dispatch(hp, dest_tiles, n_rows):
    t, w = hp.shape
    return pl.pallas_call(
        _dispatch_kernel,
        grid=(t // MOE_TG,),
        in_specs=[
            pl.BlockSpec((1, TOP_K, MOE_TG), lambda i: (i, 0, 0), memory_space=pltpu.SMEM),
            pl.BlockSpec((MOE_TG, w), lambda i: (i, 0)),
            pl.BlockSpec(memory_space=pl.ANY),
        ],
        out_specs=pl.BlockSpec(memory_space=pl.ANY),
        out_shape=jax.ShapeDtypeStruct((n_rows, w), hp.dtype),
        scratch_shapes=[pltpu.SemaphoreType.DMA],
        input_output_aliases={2: 0},
        compiler_params=_cparams(("arbitrary",)),
        name="moe_dispatch",
    )(dest_tiles, hp, jnp.zeros((n_rows, w), hp.dtype))


def _expert_kernel(te_ref, nt_ref, xs_ref, w1_ref, w3_ref, w2_ref, y_ref):
    del te_ref
    i = pl.program_id(0)

    @pl.when(i < nt_ref[0])
    def _():
        h = xs_ref[...].astype(BF16)
        a = _dot(h, w1_ref[0])
        g = (a * jax.nn.sigmoid(a)) * _dot(h, w3_ref[0])
        y_ref[...] = _dot(g.astype(BF16), w2_ref[0])

    @pl.when(i >= nt_ref[0])
    def _():
        y_ref[...] = jnp.zeros_like(y_ref)


def _experts(xs, tile_expert, n_tiles, w1, w3, w2):
    n_rows, w = xs.shape
    _, d, ff = w1.shape
    return pl.pallas_call(
        _expert_kernel,
        grid_spec=pltpu.PrefetchScalarGridSpec(
            num_scalar_prefetch=2,
            grid=(n_rows // MOE_TM,),
            in_specs=[
                pl.BlockSpec((MOE_TM, w), lambda i, te, nt: (i, 0)),
                pl.BlockSpec((1, d, ff), lambda i, te, nt: (te[i], 0, 0)),
                pl.BlockSpec((1, d, ff), lambda i, te, nt: (te[i], 0, 0)),
                pl.BlockSpec((1, ff, d), lambda i, te, nt: (te[i], 0, 0)),
            ],
            out_specs=pl.BlockSpec((MOE_TM, d), lambda i, te, nt: (i, 0)),
        ),
        out_shape=jax.ShapeDtypeStruct((n_rows, d), F32),
        compiler_params=_cparams(("arbitrary",)),
        name="moe_experts",
    )(tile_expert, n_tiles, xs, w1, w3, w2)


def _combine_kernel(dest_ref, x_ref, mod_ref, wt_ref, y_ref, o_ref, buf_ref, sem):
    _row_copies(dest_ref,
                lambda s, r, row: y_ref.at[pl.ds(row, 1)],
                lambda s, r, row: buf_ref.at[s, pl.ds(r, 1)],
                sem, x_ref.shape[1])
    wt = wt_ref[0]
    mixed = wt[:, 0:1] * buf_ref[0] + wt[:, 1:2] * buf_ref[1]
    o_ref[0] = x_ref[0] + mod_ref[0, 5:6, :] * mixed


def _combine(x, mod, wts, dest_tiles, y):
    b, l, d = x.shape
    per_b = l // MOE_TG
    return pl.pallas_call(
        _combine_kernel,
        grid=(b, per_b),
        in_specs=[
            pl.BlockSpec((1, TOP_K, MOE_TG), lambda i, j: (i * per_b + j, 0, 0), memory_space=pltpu.SMEM),
            pl.BlockSpec((1, MOE_TG, d), lambda i, j: (i, j, 0)),
            pl.BlockSpec((1, SUBLANES, d), _mod_index(mod)),
            pl.BlockSpec((1, MOE_TG, LANES), lambda i, j: (i, j, 0)),
            pl.BlockSpec(memory_space=pl.ANY),
        ],
        out_specs=pl.BlockSpec((1, MOE_TG, d), lambda i, j: (i, j, 0)),
        out_shape=jax.ShapeDtypeStruct(x.shape, F32),
        scratch_shapes=[pltpu.VMEM((TOP_K, MOE_TG, d), F32), pltpu.SemaphoreType.DMA],
        compiler_params=_cparams(("arbitrary", "arbitrary")),
        name="moe_combine",
    )(dest_tiles, x, mod, wts, y)


def _moe(x, nw, mod, router, w1, w3, w2):
    b, l, d = x.shape
    t = b * l
    hp, idx, wts = _router(x, nw, mod, router)
    dest, tile_expert, n_tiles = _route_plan(idx.reshape(t, LANES)[:, :TOP_K], w1.shape[0], MOE_TM)
    dest_tiles = dest.reshape(t // MOE_TG, MOE_TG, TOP_K).transpose(0, 2, 1)
    n_rows = t * TOP_K + w1.shape[0] * MOE_TM
    xs = _dispatch(hp.reshape(t, d), dest_tiles, n_rows)
    y = _experts(xs, tile_expert, n_tiles, w1, w3, w2)
    return _combine(x, mod, wts, dest_tiles, y)


def _final_norm_kernel(x_ref, nw_ref, o_ref):
    x = x_ref[0]
    o_ref[0] = x * lax.rsqrt(jnp.mean(x * x, axis=-1, keepdims=True) + EPS) * nw_ref[...]


def _final_norm(x, nw, tm=512):
    b, l, d = x.shape
    tm = min(tm, l)
    return pl.pallas_call(
        _final_norm_kernel,
        grid=(b, l // tm),
        in_specs=[pl.BlockSpec((1, tm, d), lambda i, j: (i, j, 0)),
                  pl.BlockSpec((1, d), lambda i, j: (0, 0))],
        out_specs=pl.BlockSpec((1, tm, d), lambda i, j: (i, j, 0)),
        out_shape=jax.ShapeDtypeStruct(x.shape, F32),
        compiler_params=_cparams(("parallel", "parallel")),
        name="final_norm",
    )(x, nw.reshape(1, d))


def _pad_rows(a, rows):
    return jnp.zeros((rows,) + a.shape[1:], a.dtype).at[:a.shape[0]].set(a)


def kernel(x_prompt, x_sample, cache_k, cache_v, state_hgrn, c, c_ctx, norm1_w, norm2_w, w_ada, b_ada,
           w_in, rpb, hg_lower, hg_norm_w, w_out, ffn_w1, ffn_w3, ffn_w2, moe_router, moe_w1, moe_w3,
           moe_w2, final_norm_w):
    depth = w_in.shape[0]
    d = x_prompt.shape[-1]
    nb = x_prompt.shape[0]
    nd = x_sample.shape[0]
    rows = x_sample.shape[1] // GRID_W

    lbs = jnp.cumsum(jax.nn.softmax(hg_lower.astype(F32), axis=0), axis=0)
    lbs = lbs - lbs[0:1]
    lbc = _pad_rows(jnp.stack([jnp.log(lbs), jnp.log1p(-lbs), 1.0 - lbs], axis=1).transpose(1, 0, 2),
                    SUBLANES).transpose(1, 0, 2)

    cond = _pad_rows(jnp.concatenate([c, c_ctx[None, :]], axis=0), SUBLANES)
    mods = _modulation(cond, w_ada, b_ada)
    mods = _pad_rows(mods.reshape(depth, SUBLANES, 6, d).transpose(2, 0, 1, 3),
                     SUBLANES).transpose(1, 2, 0, 3)

    w_in_b = w_in.astype(BF16)
    w_out_b = w_out.astype(BF16)
    ffn_w1_b, ffn_w3_b, ffn_w2_b = ffn_w1.astype(BF16), ffn_w3.astype(BF16), ffn_w2.astype(BF16)
    moe_w1_b, moe_w3_b, moe_w2_b = moe_w1.astype(BF16), moe_w3.astype(BF16), moe_w2.astype(BF16)
    wkt_b = jnp.swapaxes(w_in[:, :, NA_W:2 * NA_W], 1, 2).astype(BF16)
    lc = cache_k.shape[2]
    ckt_b = jnp.swapaxes(cache_k.astype(BF16).reshape(nd, depth, lc, NA_W), 2, 3)
    cv_b = cache_v.astype(BF16).reshape(nd, depth, lc, NA_W)
    zero_state = jnp.zeros((nb, 2, HG_HEADS, HG_D, HG_D), F32)

    def layer(x, seq_shape, mod, l, attend, s0, wkt):
        tok = lambda a: a.reshape(x.shape[:2] + a.shape[-1:])
        seq = lambda a: a.reshape(seq_shape + a.shape[-1:])
        want_kv = wkt is None
        outs = _inproj(x, norm1_w[l], mod, w_in_b[l], wkt)
        att = attend(seq(outs[0]), outs[2])
        hg, s_fin = _hgrn(seq(outs[1]), lbc[l], hg_norm_w[l], s0)
        x = _outproj(x, tok(att), tok(hg), mod, w_out_b[l])
        i = l // 2
        if l % 2 == 0:
            x = _ffn(x, norm2_w[l], mod, ffn_w1_b[i], ffn_w3_b[i], ffn_w2_b[i])
        else:
            x = _moe(x, norm2_w[l], mod, moe_router[i], moe_w1_b[i], moe_w3_b[i], moe_w2_b[i])
        return x, (seq(outs[2]) if want_kv else None), s_fin

    seq = x_prompt.shape[1]
    xp, xs = x_prompt.reshape(1, nb * seq, d), x_sample
    ks_out, vs_out, ss_out = [], [], []
    for l in range(depth):
        mod_p = mods[l, nd:nd + 1]
        mod_s = mods[l, :nd]
        xp, kv, s_fin = layer(xp, (nb, seq), mod_p, l, lambda qkv, _: _ctx_attention(qkv),
                              zero_state, None)
        ks_out.append(kv[..., :NA_W])
        vs_out.append(kv[..., NA_W:])
        ss_out.append(s_fin)
        bias = _na_bias_slabs(rpb[l], rows)
        attend_lat = lambda qkv, kt, l=l, bias=bias: _na_attention(qkv, kt, ckt_b[:, l], cv_b[:, l], bias)
        xs, _, _ = layer(xs, x_sample.shape[:2], mod_s, l, attend_lat,
                         state_hgrn[:, l].astype(F32), wkt_b[l])

    y_prompt = _final_norm(xp, final_norm_w).reshape(x_prompt.shape)
    y_sample = _final_norm(xs, final_norm_w)
    new_cache_k = jnp.stack(ks_out, axis=1).reshape(nb, depth, seq, NA_HEADS, NA_DH)
    new_cache_v = jnp.stack(vs_out, axis=1).reshape(nb, depth, seq, NA_HEADS, NA_DH)
    new_state = jnp.stack(ss_out, axis=1).astype(x_prompt.dtype)
    return (y_prompt, y_sample, new_cache_k, new_cache_v, new_state)
```

```python
import functools

import numpy as np
import jax
import jax.numpy as jnp
from jax import lax
from jax.experimental import pallas as pl
from jax.experimental.pallas import tpu as pltpu

F32 = jnp.float32
BF16 = jnp.bfloat16

EPS = 1e-6
NA_HEADS = 8
NA_DH = 64
NA_W = NA_HEADS * NA_DH
GRID_W = 64
WIN_R = 8
WIN_C = 16
HG_HEADS = 4
HG_D = 128
HG_W = HG_HEADS * HG_D
N_EXPERTS = 8
LANES = 128
SUBLANES = 8
VMEM_LIMIT = 56 * 1024 * 1024

HG_CHUNK = 128
HG_SUB = 16
HG_UNROLL = 4
HG_FLAT = 32
HG_FLAT_RANGE = 60.0


def _cparams(sem):
    return pltpu.CompilerParams(dimension_semantics=sem, vmem_limit_bytes=VMEM_LIMIT)


def _dot(a, b):
    return jnp.dot(a, b, preferred_element_type=F32)


def _dot_nt(a, b):
    return lax.dot_general(a, b, (((1,), (1,)), ((), ())), preferred_element_type=F32)


def _dot_tn(a, b):
    return lax.dot_general(a, b, (((0,), (0,)), ((), ())), preferred_element_type=F32)


def _norm_mod(x, nw, shift, scale):
    ms = jnp.mean(x * x, axis=-1, keepdims=True)
    y = x * lax.rsqrt(ms + EPS) * nw
    return y * (1.0 + scale) + shift


def _mod_kernel(cond_ref, w_ref, b_ref, o_ref):
    c = cond_ref[...]
    s = c * jax.nn.sigmoid(c)
    o_ref[0] = _dot(s.astype(BF16), w_ref[0].astype(BF16)) + b_ref[0]


def _modulation(cond, w_ada, b_ada, tn=1536):
    depth, d, n = w_ada.shape
    rows = cond.shape[0]
    return pl.pallas_call(
        _mod_kernel,
        grid=(depth, n // tn),
        in_specs=[
            pl.BlockSpec((rows, d), lambda l, j: (0, 0)),
            pl.BlockSpec((1, d, tn), lambda l, j: (l, 0, j)),
            pl.BlockSpec((1, 1, tn), lambda l, j: (l, 0, j)),
        ],
        out_specs=pl.BlockSpec((1, rows, tn), lambda l, j: (l, 0, j)),
        out_shape=jax.ShapeDtypeStruct((depth, rows, n), F32),
        compiler_params=_cparams(("parallel", "parallel")),
        name="modulation",
    )(cond, w_ada, b_ada.reshape(depth, 1, n))


def _mod_index(mod):
    if mod.shape[0] == 1:
        return lambda b, *_: (0, 0, 0)
    return lambda b, *_: (b, 0, 0)


def _inproj_kernel(x_ref, nw_ref, mod_ref, w_ref, *refs, want_kt):
    if want_kt:
        wkt_ref, qkv_ref, hg_ref, extra_ref = refs
    else:
        qkv_ref, hg_ref, extra_ref = refs
    h = _norm_mod(x_ref[0], nw_ref[...], mod_ref[0, 0:1, :], mod_ref[0, 1:2, :]).astype(BF16)
    p = _dot(h, w_ref[...])
    qkv_ref[0] = p[:, :3 * NA_W].astype(BF16)
    hg_ref[0] = p[:, 3 * NA_W:]
    if want_kt:
        extra_ref[0] = _dot_nt(wkt_ref[...], h).astype(BF16)
    else:
        extra_ref[0] = p[:, NA_W:3 * NA_W]


def _inproj(x, nw, mod, w, wkt=None, tm=512):
    b, l, d = x.shape
    n = w.shape[1]
    want_kt = wkt is not None
    out_shape = [jax.ShapeDtypeStruct((b, l, 3 * NA_W), BF16),
                 jax.ShapeDtypeStruct((b, l, n - 3 * NA_W), F32)]
    out_specs = [pl.BlockSpec((1, tm, 3 * NA_W), lambda i, j: (i, j, 0)),
                 pl.BlockSpec((1, tm, n - 3 * NA_W), lambda i, j: (i, j, 0))]
    in_specs = [pl.BlockSpec((1, tm, d), lambda i, j: (i, j, 0)),
                pl.BlockSpec((1, d), lambda i, j: (0, 0)),
                pl.BlockSpec((1, SUBLANES, d), _mod_index(mod)),
                pl.BlockSpec((d, n), lambda i, j: (0, 0))]
    args = [x, nw.reshape(1, d), mod, w]
    if want_kt:
        in_specs.append(pl.BlockSpec((NA_W, d), lambda i, j: (0, 0)))
        args.append(wkt)
        out_shape.append(jax.ShapeDtypeStruct((b, NA_W, l), BF16))
        out_specs.append(pl.BlockSpec((1, NA_W, tm), lambda i, j: (i, 0, j)))
    else:
        out_shape.append(jax.ShapeDtypeStruct((b, l, 2 * NA_W), F32))
        out_specs.append(pl.BlockSpec((1, tm, 2 * NA_W), lambda i, j: (i, j, 0)))
    return pl.pallas_call(
        functools.partial(_inproj_kernel, want_kt=want_kt),
        grid=(b, l // tm),
        in_specs=in_specs,
        out_specs=out_specs,
        out_shape=out_shape,
        compiler_params=_cparams(("parallel", "parallel")),
        name="inproj",
    )(*args)


def _ctx_attn_kernel(q_ref, k_ref, v_ref, o_ref):
    scale = NA_DH ** -0.5
    for h in range(NA_HEADS):
        sl = slice(h * NA_DH, (h + 1) * NA_DH)
        q = q_ref[0, :, sl] * scale
        s = _dot_nt(q, k_ref[0, :, sl])
        m = jnp.max(s, axis=-1, keepdims=True)
        p = jnp.exp(s - m)
        den = jnp.sum(p, axis=-1, keepdims=True)
        o = _dot(p.astype(BF16), v_ref[0, :, sl]) / den
        o_ref[0, :, sl] = o.astype(BF16)


def _ctx_attention(qkv):
    b, l, _ = qkv.shape
    spec = lambda c: pl.BlockSpec((1, l, NA_W), lambda i, c=c: (i, 0, c))
    return pl.pallas_call(
        _ctx_attn_kernel,
        grid=(b,),
        in_specs=[spec(0), spec(1), spec(2)],
        out_specs=pl.BlockSpec((1, l, NA_W), lambda i: (i, 0, 0)),
        out_shape=jax.ShapeDtypeStruct((b, l, NA_W), BF16),
        compiler_params=_cparams(("parallel",)),
        name="ctx_attention",
    )(qkv, qkv, qkv)


NA_QROWS = 2
NA_KROWS = WIN_R + NA_QROWS


def _na_window_start(p, rows):
    return jnp.clip(NA_QROWS * p - WIN_R // 2, 0, rows - NA_KROWS)


def _lane_fold(op, *arrays):
    tiles = [a[:, i:i + LANES] for a in arrays for i in range(0, a.shape[1], LANES)]
    while len(tiles) > 1:
        tiles = [op(tiles[i], tiles[i + 1]) if i + 1 < len(tiles) else tiles[i]
                 for i in range(0, len(tiles), 2)]
    return tiles[0]


def _na_kernel(q_ref, kt_ref, v_ref, ckt_ref, cv_ref, bias_ref, o_ref, s_ref, *, rows):
    ws = _na_window_start(pl.program_id(1), rows)
    start = pl.multiple_of(ws * GRID_W, NA_QROWS * GRID_W)
    n_loc = NA_KROWS * GRID_W
    scale = NA_DH ** -0.5
    maxes = []
    for h in range(NA_HEADS):
        sl = slice(h * NA_DH, (h + 1) * NA_DH)
        q = q_ref[0, :, sl] * scale
        s_loc = _dot(q, kt_ref[0, sl, pl.ds(start, n_loc)]) + bias_ref[0, h]
        s_ctx = _dot(q, ckt_ref[0, sl, :])
        s_ref[h, :, :n_loc] = s_loc
        s_ref[h, :, n_loc:] = s_ctx
        maxes.append(jnp.max(_lane_fold(jnp.maximum, s_loc, s_ctx), axis=-1, keepdims=True))
    for h in range(NA_HEADS):
        sl = slice(h * NA_DH, (h + 1) * NA_DH)
        p_loc = jnp.exp(s_ref[h, :, :n_loc] - maxes[h])
        p_ctx = jnp.exp(s_ref[h, :, n_loc:] - maxes[h])
        den = jnp.sum(_lane_fold(jnp.add, p_loc, p_ctx), axis=-1, keepdims=True)
        o = (_dot(p_loc.astype(BF16), v_ref[0, pl.ds(start, n_loc), sl])
             + _dot(p_ctx.astype(BF16), cv_ref[0, :, sl])) / den
        o_ref[0, :, sl] = o.astype(BF16)


NA_EDGE = 2


def _na_variant(p, n_pairs):
    return jnp.where(p < NA_EDGE, p, jnp.where(p >= n_pairs - NA_EDGE, p - (n_pairs - 2 * NA_EDGE - 1), NA_EDGE))


def _na_bias_slabs(rpb, rows):
    n_pairs = rows // NA_QROWS
    ps = np.array([0, 1, 2, n_pairs - 2, n_pairs - 1])
    r = NA_QROWS * ps[:, None] + np.arange(NA_QROWS)[None, :]
    rs = np.clip(r - WIN_R // 2, 0, rows - WIN_R)
    ws = np.clip(NA_QROWS * ps - WIN_R // 2, 0, rows - NA_KROWS)
    krow = ws[:, None] + np.arange(NA_KROWS)[None, :]
    row_ok = (krow[:, None, :] >= rs[:, :, None]) & (krow[:, None, :] < rs[:, :, None] + WIN_R)
    dr = np.clip(krow[:, None, :] - r[:, :, None] + (WIN_R - 1), 0, 2 * WIN_R - 2)
    j = np.arange(GRID_W)[:, None]
    kc = np.arange(GRID_W)[None, :]
    cs = np.clip(j - WIN_C // 2, 0, GRID_W - WIN_C)
    col_ok = (kc >= cs) & (kc < cs + WIN_C)
    dc = np.clip(kc - j + (WIN_C - 1), 0, 2 * WIN_C - 2)
    planes = jnp.where(jnp.asarray(col_ok)[None, None], rpb.astype(F32)[:, :, dc], -jnp.inf)
    planes = jnp.concatenate([planes, jnp.full_like(planes[:, :1], -jnp.inf)], axis=1)
    dr = np.where(row_ok, dr, 2 * WIN_R - 1)
    tab = jnp.take(planes, jnp.asarray(dr.reshape(-1)), axis=1)
    tab = tab.reshape((NA_HEADS,) + dr.shape + (GRID_W, GRID_W))
    tab = jnp.transpose(tab, (1, 0, 2, 4, 3, 5))
    return tab.reshape(len(ps), NA_HEADS, NA_QROWS * GRID_W, NA_KROWS * GRID_W)


def _na_attention(qkv, kt, ckt, cv, bias):
    b, l, _ = qkv.shape
    rows = l // GRID_W
    n_pairs = rows // NA_QROWS
    lc = cv.shape[1]
    tq = NA_QROWS * GRID_W
    n_loc = NA_KROWS * GRID_W
    return pl.pallas_call(
        functools.partial(_na_kernel, rows=rows),
        grid=(b, n_pairs),
        in_specs=[
            pl.BlockSpec((1, tq, NA_W), lambda i, p: (i, p, 0)),
            pl.BlockSpec((1, NA_W, l), lambda i, p: (i, 0, 0)),
            pl.BlockSpec((1, l, NA_W), lambda i, p: (i, 0, 2)),
            pl.BlockSpec((1, NA_W, lc), lambda i, p: (i, 0, 0)),
            pl.BlockSpec((1, lc, NA_W), lambda i, p: (i, 0, 0)),
            pl.BlockSpec((1, NA_HEADS, tq, n_loc), lambda i, p: (_na_variant(p, n_pairs), 0, 0, 0)),
        ],
        out_specs=pl.BlockSpec((1, tq, NA_W), lambda i, p: (i, p, 0)),
        out_shape=jax.ShapeDtypeStruct((b, l, NA_W), BF16),
        scratch_shapes=[pltpu.VMEM((NA_HEADS, tq, n_loc + lc), F32)],
        compiler_params=_cparams(("parallel", "arbitrary")),
        name="na_attention",
    )(qkv, kt, qkv, ckt, cv, bias)


def _split3(x):
    hi = x.astype(BF16)
    r1 = x - hi.astype(F32)
    mid = r1.astype(BF16)
    lo = (r1 - mid.astype(F32)).astype(BF16)
    return hi, mid, lo


def _block_ref(cum, block, ref_row):
    c = cum.shape[0]
    ref = cum.reshape(c // block, block, HG_D)[:, ref_row:ref_row + 1, :]
    return jnp.broadcast_to(ref, (c // block, block, HG_D)).reshape(c, HG_D)


def _hgrn_level(cum_f, cum_b, qs, k_f, k_b, m, ti, si):
    c = cum_f.shape[0]
    ref_f = _block_ref(cum_f, 2 * m, m - 1)
    ref_b = _block_ref(cum_b, 2 * m, m)
    row = lax.broadcasted_iota(jnp.int32, (c, 1), 0)
    upper = jnp.bitwise_and(row, m) != 0
    dec_f = jnp.exp(jnp.where(upper, cum_f - ref_f, ref_f - cum_f))
    dec_b = jnp.exp(jnp.where(upper, ref_b - cum_b, cum_b - ref_b))
    qm = jnp.concatenate([jnp.where(upper, qs * dec_f, 0.0), jnp.where(upper, 0.0, qs * dec_b)], axis=1)
    km = jnp.concatenate([jnp.where(upper, 0.0, k_f * dec_f), jnp.where(upper, k_b * dec_b, 0.0)], axis=1)
    am = _dot_nt(qm.astype(BF16), km.astype(BF16))
    if 2 * m < c:
        am = jnp.where(jnp.bitwise_xor(ti, si) < 2 * m, am, 0.0)
    return am


def _hgrn_pairwise(cum, qs, k, v, reverse):
    c = cum.shape[0]
    sub_row = lax.broadcasted_iota(jnp.int32, (SUBLANES, 1), 0)
    tiles_per_sub = HG_SUB // SUBLANES
    o_tiles = []
    for blk in range(c // HG_SUB):
        accs = [jnp.zeros((SUBLANES, HG_D), F32) for _ in range(tiles_per_sub)]
        for j in range(HG_SUB):
            s = blk * HG_SUB + j
            cs, ks, vs = cum[s:s + 1, :], k[s:s + 1, :], v[s:s + 1, :]
            for t in range(tiles_per_sub):
                lo_r, hi_r = t * SUBLANES, t * SUBLANES + SUBLANES - 1
                if (lo_r > j) if reverse else (hi_r < j):
                    continue
                base = blk * HG_SUB + lo_r
                x = qs[base:base + SUBLANES, :] * ks * jnp.exp(
                    jnp.minimum(cum[base:base + SUBLANES, :] - cs, 0.0))
                a = jnp.sum(x, axis=-1, keepdims=True)
                if reverse and hi_r > j:
                    a = jnp.where(sub_row + lo_r <= j, a, 0.0)
                elif (not reverse) and lo_r < j:
                    a = jnp.where(sub_row + lo_r >= j, a, 0.0)
                accs[t] = accs[t] + a * vs
        o_tiles.extend(accs)
    return jnp.concatenate(o_tiles, axis=0)


def _hgrn_gates(z, consts):
    loglb, log1mlb, oml = consts
    e = jnp.exp(-jnp.abs(z))
    one_e = 1.0 + e
    logsig = jnp.minimum(z, 0.0) - jnp.log(one_e)
    bb = log1mlb + logsig
    logf = jnp.maximum(loglb, bb) + jnp.log(1.0 + jnp.exp(-jnp.abs(loglb - bb)))
    k = oml * (jnp.where(z >= 0.0, e, 1.0) / one_e)
    return logf, k


def _flat_offsets(cum_f, cum_b):
    return (cum_f - _block_ref(cum_f, HG_FLAT, HG_FLAT // 2 - 1),
            cum_b - _block_ref(cum_b, HG_FLAT, HG_FLAT // 2))


def _hgrn_kernel(q_ref, i_ref, zf_ref, zb_ref, g_ref, lbc_ref, nw_ref, s0_ref,
                 o_ref, s_ref, cum_ref, k_ref, qs_ref, acc_ref, qt_ref, u_ref, de_ref):
    l = q_ref.shape[1]
    c = HG_CHUNK
    nc = l // c
    consts = (lbc_ref[0:1, :], lbc_ref[1:2, :], lbc_ref[2:3, :])
    ti = lax.broadcasted_iota(jnp.int32, (c, c), 0)
    si = lax.broadcasted_iota(jnp.int32, (c, c), 1)

    def chunk_rows(i):
        return pl.ds(pl.multiple_of(i * c, c), c)

    def prepare(i, gmax):
        rows = chunk_rows(i)
        q = q_ref[0, rows, :]
        qs_ref[rows, :] = q * jax.nn.sigmoid(q)
        cums = []
        for d, z_ref in enumerate((zf_ref, zb_ref)):
            logf, k = _hgrn_gates(z_ref[0, rows, :], consts)
            k_ref[d, rows, :] = k
            tri = jnp.where((si >= ti) if d else (si <= ti), 1.0, 0.0).astype(BF16)
            parts = _dot(tri, jnp.concatenate(_split3(logf), axis=1))
            cum = parts[:, :HG_D] + parts[:, HG_D:2 * HG_D] + parts[:, 2 * HG_D:]
            cum_ref[d, rows, :] = cum
            cums.append(cum)
        dq_f, dq_b = _flat_offsets(*cums)
        g = jnp.maximum(jnp.abs(dq_f), jnp.abs(dq_b))
        return jnp.maximum(gmax, jnp.max(g.reshape(c // SUBLANES, SUBLANES, HG_D), axis=0))

    gmax = lax.fori_loop(0, nc, prepare, jnp.zeros((SUBLANES, HG_D), F32), unroll=HG_UNROLL)
    flat_ok = jnp.max(gmax) <= HG_FLAT_RANGE

    def intra(i, flat):
        rows = chunk_rows(i)
        cum_f, cum_b = cum_ref[0, rows, :], cum_ref[1, rows, :]
        k_f, k_b = k_ref[0, rows, :], k_ref[1, rows, :]
        qs = qs_ref[rows, :]
        v = i_ref[0, rows, :]
        vb = v.astype(BF16)
        a = jnp.zeros((c, c), F32)
        m = c // 2
        while m >= (HG_FLAT if flat else HG_SUB):
            a = a + _hgrn_level(cum_f, cum_b, qs, k_f, k_b, m, ti, si)
            m //= 2
        if flat:
            dq_f, dq_b = _flat_offsets(cum_f, cum_b)
            same = jnp.bitwise_xor(ti, si) < HG_FLAT
            a_f = _dot_nt((qs * jnp.exp(dq_f)).astype(BF16), (k_f * jnp.exp(-dq_f)).astype(BF16))
            a_b = _dot_nt((qs * jnp.exp(dq_b)).astype(BF16), (k_b * jnp.exp(-dq_b)).astype(BF16))
            a = a + jnp.where(same & (si <= ti), a_f, 0.0) + jnp.where(same & (si >= ti), a_b, 0.0)
            o = _dot(a.astype(BF16), vb)
        else:
            o = (_dot(a.astype(BF16), vb) + _hgrn_pairwise(cum_f, qs, k_f, v, False)
                 + _hgrn_pairwise(cum_b, qs, k_b, v, True))
        acc_ref[rows, :] = o
        edge_f, edge_b = cum_f[c - 1:c, :], cum_b[0:1, :]
        qt_ref[rows, :] = jnp.concatenate([(qs * jnp.exp(cum_f)).astype(BF16),
                                           (qs * jnp.exp(cum_b)).astype(BF16)], axis=1)
        kd = jnp.concatenate([(k_f * jnp.exp(edge_f - cum_f)).astype(BF16),
                              (k_b * jnp.exp(edge_b - cum_b)).astype(BF16)], axis=1)
        u_ref[i] = _dot_tn(vb, kd)
        de_ref[i] = jnp.broadcast_to(jnp.concatenate([jnp.exp(edge_f), jnp.exp(edge_b)], axis=1),
                                     (SUBLANES, 2 * HG_D))

    def run_intra(flat):
        def body(i, carry):
            intra(i, flat)
            return carry
        lax.fori_loop(0, nc, body, 0, unroll=HG_UNROLL if flat else 1)

    pl.when(flat_ok)(lambda: run_intra(True))
    pl.when(jnp.logical_not(flat_ok))(lambda: run_intra(False))

    def sweep(i, carry):
        st_f, st_b = carry
        j = nc - 1 - i
        inc = u_ref[i, :, :HG_D]
        u_ref[i, :, :HG_D] = st_f
        st_f = st_f * de_ref[i, 0:1, :HG_D] + inc
        inc = u_ref[j, :, HG_D:]
        u_ref[j, :, HG_D:] = st_b
        st_b = st_b * de_ref[j, 0:1, HG_D:] + inc
        return st_f, st_b

    st_f, st_b = lax.fori_loop(0, nc, sweep, (s0_ref[0, 0, 0].T, s0_ref[0, 1, 0].T))
    s_ref[0, 0, 0] = st_f.T
    s_ref[0, 1, 0] = st_b.T

    def finish(i, carry):
        rows = chunk_rows(i)
        o = acc_ref[rows, :] + _dot_nt(qt_ref[rows, :], u_ref[i].astype(BF16))
        o = o * lax.rsqrt(jnp.mean(o * o, axis=-1, keepdims=True) + EPS) * nw_ref[...]
        g = g_ref[0, rows, :]
        o_ref[0, rows, :] = (o * (g * jax.nn.sigmoid(g))).astype(BF16)
        return carry

    lax.fori_loop(0, nc, finish, 0, unroll=HG_UNROLL)


def _hgrn(hgp, lbc, norm_w, s0):
    b, l, _ = hgp.shape
    col = lambda sec: pl.BlockSpec((1, l, HG_D), lambda i, h, sec=sec: (i, 0, sec * HG_HEADS + h))
    state_spec = pl.BlockSpec((1, 2, 1, HG_D, HG_D), lambda i, h: (i, 0, h, 0, 0))
    return pl.pallas_call(
        _hgrn_kernel,
        grid=(b, HG_HEADS),
        in_specs=[col(0), col(1), col(2), col(3), col(4),
                  pl.BlockSpec((SUBLANES, HG_D), lambda i, h: (0, h)),
                  pl.BlockSpec((1, HG_D), lambda i, h: (0, 0)),
                  state_spec],
        out_specs=[pl.BlockSpec((1, l, HG_D), lambda i, h: (i, 0, h)), state_spec],
        out_shape=[jax.ShapeDtypeStruct((b, l, HG_W), BF16),
                   jax.ShapeDtypeStruct((b, 2, HG_HEADS, HG_D, HG_D), F32)],
        scratch_shapes=[pltpu.VMEM((2, l, HG_D), F32),
                        pltpu.VMEM((2, l, HG_D), F32),
                        pltpu.VMEM((l, HG_D), F32),
                        pltpu.VMEM((l, HG_D), F32),
                        pltpu.VMEM((l, 2 * HG_D), BF16),
                        pltpu.VMEM((l // HG_CHUNK, HG_D, 2 * HG_D), F32),
                        pltpu.VMEM((l // HG_CHUNK, SUBLANES, 2 * HG_D), F32)],
        compiler_params=_cparams(("parallel", "parallel")),
        name="hgrn",
    )(hgp, hgp, hgp, hgp, hgp, lbc, norm_w.reshape(1, HG_D), s0)


def _outproj_kernel(x_ref, att_ref, hg_ref, mod_ref, w_ref, o_ref):
    y = _dot(att_ref[0], w_ref[:NA_W, :]) + _dot(hg_ref[0], w_ref[NA_W:, :])
    o_ref[0] = x_ref[0] + mod_ref[0, 2:3, :] * y


def _outproj(x, att, hg, mod, w, tm=512):
    b, l, d = x.shape
    tm = min(tm, l)
    return pl.pallas_call(
        _outproj_kernel,
        grid=(b, l // tm),
        in_specs=[
            pl.BlockSpec((1, tm, d), lambda i, j: (i, j, 0)),
            pl.BlockSpec((1, tm, NA_W), lambda i, j: (i, j, 0)),
            pl.BlockSpec((1, tm, HG_W), lambda i, j: (i, j, 0)),
            pl.BlockSpec((1, SUBLANES, d), _mod_index(mod)),
            pl.BlockSpec(w.shape, lambda i, j: (0, 0)),
        ],
        out_specs=pl.BlockSpec((1, tm, d), lambda i, j: (i, j, 0)),
        out_shape=jax.ShapeDtypeStruct(x.shape, F32),
        compiler_params=_cparams(("parallel", "parallel")),
        name="outproj",
    )(x, att, hg, mod, w)


def _ffn_kernel(x_ref, nw_ref, mod_ref, w1_ref, w3_ref, w2_ref, o_ref, h_ref, acc_ref):
    f = pl.program_id(2)

    @pl.when(f == 0)
    def _():
        h = _norm_mod(x_ref[0], nw_ref[...], mod_ref[0, 3:4, :], mod_ref[0, 4:5, :])
        h_ref[...] = h.astype(BF16)
        acc_ref[...] = jnp.zeros_like(acc_ref)

    h = h_ref[...]
    a = _dot(h, w1_ref[...])
    g = (a * jax.nn.sigmoid(a)) * _dot(h, w3_ref[...])
    acc_ref[...] += _dot(g.astype(BF16), w2_ref[...])

    @pl.when(f == pl.num_programs(2) - 1)
    def _():
        o_ref[0] = x_ref[0] + mod_ref[0, 5:6, :] * acc_ref[...]


def _ffn(x, nw, mod, w1, w3, w2, tm=512, tf=1408):
    b, l, d = x.shape
    tm = min(tm, l)
    ff = w1.shape[1]
    tf = min(tf, ff)
    return pl.pallas_call(
        _ffn_kernel,
        grid=(b, l // tm, ff // tf),
        in_specs=[
            pl.BlockSpec((1, tm, d), lambda i, j, f: (i, j, 0)),
            pl.BlockSpec((1, d), lambda i, j, f: (0, 0)),
            pl.BlockSpec((1, SUBLANES, d), _mod_index(mod)),
            pl.BlockSpec((d, tf), lambda i, j, f: (0, f)),
            pl.BlockSpec((d, tf), lambda i, j, f: (0, f)),
            pl.BlockSpec((tf, d), lambda i, j, f: (f, 0)),
        ],
        out_specs=pl.BlockSpec((1, tm, d), lambda i, j, f: (i, j, 0)),
        out_shape=jax.ShapeDtypeStruct(x.shape, F32),
        scratch_shapes=[pltpu.VMEM((tm, d), BF16), pltpu.VMEM((tm, d), F32)],
        compiler_params=_cparams(("parallel", "parallel", "arbitrary")),
        name="ffn",
    )(x, nw.reshape(1, d), mod, w1, w3, w2)


def _router_kernel(x_ref, nw_ref, mod_ref, r_ref, hp_ref, idx_ref, wt_ref):
    h = _norm_mod(x_ref[0], nw_ref[...], mod_ref[0, 3:4, :], mod_ref[0, 4:5, :])
    hh = h.astype(BF16)
    hl = (h - hh.astype(F32)).astype(BF16)
    r = r_ref[...]
    rh = r.astype(BF16)
    rl = (r - rh.astype(F32)).astype(BF16)
    logits = _dot(hh, rh) + _dot(hh, rl) + _dot(hl, rh)
    lane = lax.broadcasted_iota(jnp.int32, logits.shape, 1)
    logits = jnp.where(lane < N_EXPERTS, logits, -jnp.inf)
    m1 = jnp.max(logits, axis=-1, keepdims=True)
    i1 = jnp.min(jnp.where(logits == m1, lane, LANES), axis=-1, keepdims=True)
    rest = jnp.where(lane == i1, -jnp.inf, logits)
    m2 = jnp.max(rest, axis=-1, keepdims=True)
    i2 = jnp.min(jnp.where(rest == m2, lane, LANES), axis=-1, keepdims=True)
    e2 = jnp.exp(m2 - m1)
    den = 1.0 + e2
    idx_ref[0] = jnp.where(lane == 0, i1, jnp.where(lane == 1, i2, 0))
    wt_ref[0] = jnp.where(lane == 0, 1.0 / den, jnp.where(lane == 1, e2 / den, 0.0))
    hp_ref[0] = h


def _router(x, nw, mod, router, tm=512):
    b, l, d = x.shape
    tm = min(tm, l)
    rp = jnp.zeros((d, LANES), F32).at[:, :router.shape[1]].set(router)
    tile = lambda w: pl.BlockSpec((1, tm, w), lambda i, j: (i, j, 0))
    return pl.pallas_call(
        _router_kernel,
        grid=(b, l // tm),
        in_specs=[
            tile(d),
            pl.BlockSpec((1, d), lambda i, j: (0, 0)),
            pl.BlockSpec((1, SUBLANES, d), _mod_index(mod)),
            pl.BlockSpec((d, LANES), lambda i, j: (0, 0)),
        ],
        out_specs=[tile(d), tile(LANES), tile(LANES)],
        out_shape=[jax.ShapeDtypeStruct((b, l, d), F32),
                   jax.ShapeDtypeStruct((b, l, LANES), jnp.int32),
                   jax.ShapeDtypeStruct((b, l, LANES), F32)],
        compiler_params=_cparams(("parallel", "parallel")),
        name="router",
    )(x, nw.reshape(1, d), mod, rp)


MOE_TM = 512
MOE_TG = 1024
TOP_K = 2


def _route_plan(idx, n_experts, tm):
    t = idx.shape[0]
    chosen = idx[:, :, None] == jnp.arange(n_experts, dtype=jnp.int32)[None, None, :]
    sel = jnp.any(chosen, axis=1)
    rank = jnp.cumsum(sel.astype(jnp.int32), axis=0) - 1
    count = rank[-1] + 1
    tiles = (count + tm - 1) // tm
    tile_end = jnp.cumsum(tiles)
    start = (tile_end - tiles) * tm
    dest = jnp.sum(jnp.where(chosen, (start[None, :] + rank)[:, None, :], 0), axis=-1)
    n_tiles_max = (t * TOP_K) // tm + n_experts
    used = jnp.minimum(jnp.arange(n_tiles_max, dtype=jnp.int32), tile_end[-1] - 1)
    tile_expert = jnp.sum((used[:, None] >= tile_end[None, :]).astype(jnp.int32), axis=1)
    return dest.astype(jnp.int32), tile_expert.astype(jnp.int32), tile_end[-1:].astype(jnp.int32)


def _row_copies(dest_ref, src_of, dst_of, sem, n):
    def each(group, fn):
        base = pl.multiple_of(group * SUBLANES, SUBLANES)
        for j in range(SUBLANES):
            for s in range(TOP_K):
                r = base + j
                row = dest_ref[0, s, r]
                fn(pltpu.make_async_copy(src_of(s, r, row), dst_of(s, r, row), sem))

    def start(group, carry):
        each(group, lambda copy: copy.start())
        return carry

    def wait(group, carry):
        each(group, lambda copy: copy.wait())
        return carry

    lax.fori_loop(0, n // SUBLANES, start, 0)
    lax.fori_loop(0, n // SUBLANES, wait, 0)


def _dispatch_kernel(dest_ref, hp_ref, init_ref, xs_ref, sem):
    del init_ref
    _row_copies(dest_ref,
                lambda s, r, row: hp_ref.at[pl.ds(r, 1)],
                lambda s, r, row: xs_ref.at[pl.ds(row, 1)],
                sem, hp_ref.shape[0])


def _dispatch(hp, dest_tiles, n_rows):
    t, w = hp.shape
    return pl.pallas_call(
        _dispatch_kernel,
        grid=(t // MOE_TG,),
        in_specs=[
            pl.BlockSpec((1, TOP_K, MOE_TG), lambda i: (i, 0, 0), memory_space=pltpu.SMEM),
            pl.BlockSpec((MOE_TG, w), lambda i: (i, 0)),
            pl.BlockSpec(memory_space=pl.ANY),
        ],
        out_specs=pl.BlockSpec(memory_space=pl.ANY),
        out_shape=jax.ShapeDtypeStruct((n_rows, w), hp.dtype),
        scratch_shapes=[pltpu.SemaphoreType.DMA],
        input_output_aliases={2: 0},
        compiler_params=_cparams(("arbitrary",)),
        name="moe_dispatch",
    )(dest_tiles, hp, jnp.zeros((n_rows, w), hp.dtype))


def _expert_kernel(te_ref, nt_ref, xs_ref, w1_ref, w3_ref, w2_ref, y_ref):
    del te_ref
    i = pl.program_id(0)

    @pl.when(i < nt_ref[0])
    def _():
        h = xs_ref[...].astype(BF16)
        a = _dot(h, w1_ref[0])
        g = (a * jax.nn.sigmoid(a)) * _dot(h, w3_ref[0])
        y_ref[...] = _dot(g.astype(BF16), w2_ref[0])

    @pl.when(i >= nt_ref[0])
    def _():
        y_ref[...] = jnp.zeros_like(y_ref)


def _experts(xs, tile_expert, n_tiles, w1, w3, w2):
    n_rows, w = xs.shape
    _, d, ff = w1.shape
    return pl.pallas_call(
        _expert_kernel,
        grid_spec=pltpu.PrefetchScalarGridSpec(
            num_scalar_prefetch=2,
            grid=(n_rows // MOE_TM,),
            in_specs=[
                pl.BlockSpec((MOE_TM, w), lambda i, te, nt: (i, 0)),
                pl.BlockSpec((1, d, ff), lambda i, te, nt: (te[i], 0, 0)),
                pl.BlockSpec((1, d, ff), lambda i, te, nt: (te[i], 0, 0)),
                pl.BlockSpec((1, ff, d), lambda i, te, nt: (te[i], 0, 0)),
            ],
            out_specs=pl.BlockSpec((MOE_TM, d), lambda i, te, nt: (i, 0)),
        ),
        out_shape=jax.ShapeDtypeStruct((n_rows, d), F32),
        compiler_params=_cparams(("arbitrary",)),
        name="moe_experts",
    )(tile_expert, n_tiles, xs, w1, w3, w2)


def _combine_kernel(dest_ref, x_ref, mod_ref, wt_ref, y_ref, o_ref, buf_ref, sem):
    _row_copies(dest_ref,
                lambda s, r, row: y_ref.at[pl.ds(row, 1)],
                lambda s, r, row: buf_ref.at[s, pl.ds(r, 1)],
                sem, x_ref.shape[1])
    wt = wt_ref[0]
    mixed = wt[:, 0:1] * buf_ref[0] + wt[:, 1:2] * buf_ref[1]
    o_ref[0] = x_ref[0] + mod_ref[0, 5:6, :] * mixed


def _combine(x, mod, wts, dest_tiles, y):
    b, l, d = x.shape
    per_b = l // MOE_TG
    return pl.pallas_call(
        _combine_kernel,
        grid=(b, per_b),
        in_specs=[
            pl.BlockSpec((1, TOP_K, MOE_TG), lambda i, j: (i * per_b + j, 0, 0), memory_space=pltpu.SMEM),
            pl.BlockSpec((1, MOE_TG, d), lambda i, j: (i, j, 0)),
            pl.BlockSpec((1, SUBLANES, d), _mod_index(mod)),
            pl.BlockSpec((1, MOE_TG, LANES), lambda i, j: (i, j, 0)),
            pl.BlockSpec(memory_space=pl.ANY),
        ],
        out_specs=pl.BlockSpec((1, MOE_TG, d), lambda i, j: (i, j, 0)),
        out_shape=jax.ShapeDtypeStruct(x.shape, F32),
        scratch_shapes=[pltpu.VMEM((TOP_K, MOE_TG, d), F32), pltpu.SemaphoreType.DMA],
        compiler_params=_cparams(("arbitrary", "arbitrary")),
        name="moe_combine",
    )(dest_tiles, x, mod, wts, y)


def _moe(x, nw, mod, router, w1, w3, w2):
    b, l, d = x.shape
    t = b * l
    hp, idx, wts = _router(x, nw, mod, router)
    dest, tile_expert, n_tiles = _route_plan(idx.reshape(t, LANES)[:, :TOP_K], w1.shape[0], MOE_TM)
    dest_tiles = dest.reshape(t // MOE_TG, MOE_TG, TOP_K).transpose(0, 2, 1)
    n_rows = t * TOP_K + w1.shape[0] * MOE_TM
    xs = _dispatch(hp.reshape(t, d), dest_tiles, n_rows)
    y = _experts(xs, tile_expert, n_tiles, w1, w3, w2)
    return _combine(x, mod, wts, dest_tiles, y)


def _final_norm_kernel(x_ref, nw_ref, o_ref):
    x = x_ref[0]
    o_ref[0] = x * lax.rsqrt(jnp.mean(x * x, axis=-1, keepdims=True) + EPS) * nw_ref[...]


def _final_norm(x, nw, tm=512):
    b, l, d = x.shape
    tm = min(tm, l)
    return pl.pallas_call(
        _final_norm_kernel,
        grid=(b, l // tm),
        in_specs=[pl.BlockSpec((1, tm, d), lambda i, j: (i, j, 0)),
                  pl.BlockSpec((1, d), lambda i, j: (0, 0))],
        out_specs=pl.BlockSpec((1, tm, d), lambda i, j: (i, j, 0)),
        out_shape=jax.ShapeDtypeStruct(x.shape, F32),
        compiler_params=_cparams(("parallel", "parallel")),
        name="final_norm",
    )(x, nw.reshape(1, d))


def _pad_rows(a, rows):
    return jnp.zeros((rows,) + a.shape[1:], a.dtype).at[:a.shape[0]].set(a)


def kernel(x_prompt, x_sample, cache_k, cache_v, state_hgrn, c, c_ctx, norm1_w, norm2_w, w_ada, b_ada,
           w_in, rpb, hg_lower, hg_norm_w, w_out, ffn_w1, ffn_w3, ffn_w2, moe_router, moe_w1, moe_w3,
           moe_w2, final_norm_w):
    depth = w_in.shape[0]
    d = x_prompt.shape[-1]
    nb = x_prompt.shape[0]
    nd = x_sample.shape[0]
    rows = x_sample.shape[1] // GRID_W

    lbs = jnp.cumsum(jax.nn.softmax(hg_lower.astype(F32), axis=0), axis=0)
    lbs = lbs - lbs[0:1]
    lbc = _pad_rows(jnp.stack([jnp.log(lbs), jnp.log1p(-lbs), 1.0 - lbs], axis=1).transpose(1, 0, 2),
                    SUBLANES).transpose(1, 0, 2)

    cond = _pad_rows(jnp.concatenate([c, c_ctx[None, :]], axis=0), SUBLANES)
    mods = _modulation(cond, w_ada, b_ada)
    mods = _pad_rows(mods.reshape(depth, SUBLANES, 6, d).transpose(2, 0, 1, 3),
                     SUBLANES).transpose(1, 2, 0, 3)

    w_in_b = w_in.astype(BF16)
    w_out_b = w_out.astype(BF16)
    ffn_w1_b, ffn_w3_b, ffn_w2_b = ffn_w1.astype(BF16), ffn_w3.astype(BF16), ffn_w2.astype(BF16)
    moe_w1_b, moe_w3_b, moe_w2_b = moe_w1.astype(BF16), moe_w3.astype(BF16), moe_w2.astype(BF16)
    wkt_b = jnp.swapaxes(w_in[:, :, NA_W:2 * NA_W], 1, 2).astype(BF16)
    lc = cache_k.shape[2]
    ckt_b = jnp.swapaxes(cache_k.astype(BF16).reshape(nd, depth, lc, NA_W), 2, 3)
    cv_b = cache_v.astype(BF16).reshape(nd, depth, lc, NA_W)
    zero_state = jnp.zeros((nb, 2, HG_HEADS, HG_D, HG_D), F32)

    def layer(x, seq_shape, mod, l, attend, s0, wkt):
        tok = lambda a: a.reshape(x.shape[:2] + a.shape[-1:])
        seq = lambda a: a.reshape(seq_shape + a.shape[-1:])
        want_kv = wkt is None
        outs = _inproj(x, norm1_w[l], mod, w_in_b[l], wkt)
        att = attend(seq(outs[0]), outs[2])
        hg, s_fin = _hgrn(seq(outs[1]), lbc[l], hg_norm_w[l], s0)
        x = _outproj(x, tok(att), tok(hg), mod, w_out_b[l])
        i = l // 2
        if l % 2 == 0:
            x = _ffn(x, norm2_w[l], mod, ffn_w1_b[i], ffn_w3_b[i], ffn_w2_b[i])
        else:
            x = _moe(x, norm2_w[l], mod, moe_router[i], moe_w1_b[i], moe_w3_b[i], moe_w2_b[i])
        return x, (seq(outs[2]) if want_kv else None), s_fin

    seq = x_prompt.shape[1]
    xp, xs = x_prompt.reshape(1, nb * seq, d), x_sample
    ks_out, vs_out, ss_out = [], [], []
    for l in range(depth):
        mod_p = mods[l, nd:nd + 1]
        mod_s = mods[l, :nd]
        xp, kv, s_fin = layer(xp, (nb, seq), mod_p, l, lambda qkv, _: _ctx_attention(qkv),
                              zero_state, None)
        ks_out.append(kv[..., :NA_W])
        vs_out.append(kv[..., NA_W:])
        ss_out.append(s_fin)
        bias = _na_bias_slabs(rpb[l], rows)
        attend_lat = lambda qkv, kt, l=l, bias=bias: _na_attention(qkv, kt, ckt_b[:, l], cv_b[:, l], bias)
        xs, _, _ = layer(xs, x_sample.shape[:2], mod_s, l, attend_lat,
                         state_hgrn[:, l].astype(F32), wkt_b[l])

    y_prompt = _final_norm(xp, final_norm_w).reshape(x_prompt.shape)
    y_sample = _final_norm(xs, final_norm_w)
    new_cache_k = jnp.stack(ks_out, axis=1).reshape(nb, depth, seq, NA_HEADS, NA_DH)
    new_cache_v = jnp.stack(vs_out, axis=1).reshape(nb, depth, seq, NA_HEADS, NA_DH)
    new_state = jnp.stack(ss_out, axis=1).astype(x_prompt.dtype)
    return (y_prompt, y_sample, new_cache_k, new_cache_v, new_state)
```

```python
import functools

import numpy as np
import jax
import jax.numpy as jnp
from jax import lax
from jax.experimental import pallas as pl
from jax.experimental.pallas import tpu as pltpu

F32 = jnp.float32
BF16 = jnp.bfloat16

EPS = 1e-6
NA_HEADS = 8
NA_DH = 64
NA_W = NA_HEADS * NA_DH
GRID_W = 64
WIN_R = 8
WIN_C = 16
HG_HEADS = 4
HG_D = 128
HG_W = HG_HEADS * HG_D
N_EXPERTS = 8
LANES = 128
SUBLANES = 8
VMEM_LIMIT = 56 * 1024 * 1024

HG_CHUNK = 128
HG_SUB = 16
HG_UNROLL = 4
HG_FLAT = 32
HG_FLAT_RANGE = 60.0


def _cparams(sem):
    return pltpu.CompilerParams(dimension_semantics=sem, vmem_limit_bytes=VMEM_LIMIT)


def _dot(a, b):
    return jnp.dot(a, b, preferred_element_type=F32)


def _dot_nt(a, b):
    return lax.dot_general(a, b, (((1,), (1,)), ((), ())), preferred_element_type=F32)


def _dot_tn(a, b):
    return lax.dot_general(a, b, (((0,), (0,)), ((), ())), preferred_element_type=F32)


def _norm_mod(x, nw, shift, scale):
    ms = jnp.mean(x * x, axis=-1, keepdims=True)
    y = x * lax.rsqrt(ms + EPS) * nw
    return y * (1.0 + scale) + shift


def _mod_kernel(cond_ref, w_ref, b_ref, o_ref):
    c = cond_ref[...]
    s = c * jax.nn.sigmoid(c)
    o_ref[0] = _dot(s.astype(BF16), w_ref[0].astype(BF16)) + b_ref[0]


def _modulation(cond, w_ada, b_ada, tn=1536):
    depth, d, n = w_ada.shape
    rows = cond.shape[0]
    return pl.pallas_call(
        _mod_kernel,
        grid=(depth, n // tn),
        in_specs=[
            pl.BlockSpec((rows, d), lambda l, j: (0, 0)),
            pl.BlockSpec((1, d, tn), lambda l, j: (l, 0, j)),
            pl.BlockSpec((1, 1, tn), lambda l, j: (l, 0, j)),
        ],
        out_specs=pl.BlockSpec((1, rows, tn), lambda l, j: (l, 0, j)),
        out_shape=jax.ShapeDtypeStruct((depth, rows, n), F32),
        compiler_params=_cparams(("parallel", "parallel")),
        name="modulation",
    )(cond, w_ada, b_ada.reshape(depth, 1, n))


def _mod_index(mod):
    if mod.shape[0] == 1:
        return lambda b, *_: (0, 0, 0)
    return lambda b, *_: (b, 0, 0)


def _inproj_kernel(x_ref, nw_ref, mod_ref, w_ref, *refs, want_kt):
    if want_kt:
        wkt_ref, qkv_ref, hg_ref, extra_ref = refs
    else:
        qkv_ref, hg_ref, extra_ref = refs
    h = _norm_mod(x_ref[0], nw_ref[...], mod_ref[0, 0:1, :], mod_ref[0, 1:2, :]).astype(BF16)
    p = _dot(h, w_ref[...])
    qkv_ref[0] = p[:, :3 * NA_W].astype(BF16)
    hg_ref[0] = p[:, 3 * NA_W:]
    if want_kt:
        extra_ref[0] = _dot_nt(wkt_ref[...], h).astype(BF16)
    else:
        extra_ref[0] = p[:, NA_W:3 * NA_W]


def _inproj(x, nw, mod, w, wkt=None, tm=512):
    b, l, d = x.shape
    n = w.shape[1]
    want_kt = wkt is not None
    out_shape = [jax.ShapeDtypeStruct((b, l, 3 * NA_W), BF16),
                 jax.ShapeDtypeStruct((b, l, n - 3 * NA_W), F32)]
    out_specs = [pl.BlockSpec((1, tm, 3 * NA_W), lambda i, j: (i, j, 0)),
                 pl.BlockSpec((1, tm, n - 3 * NA_W), lambda i, j: (i, j, 0))]
    in_specs = [pl.BlockSpec((1, tm, d), lambda i, j: (i, j, 0)),
                pl.BlockSpec((1, d), lambda i, j: (0, 0)),
                pl.BlockSpec((1, SUBLANES, d), _mod_index(mod)),
                pl.BlockSpec((d, n), lambda i, j: (0, 0))]
    args = [x, nw.reshape(1, d), mod, w]
    if want_kt:
        in_specs.append(pl.BlockSpec((NA_W, d), lambda i, j: (0, 0)))
        args.append(wkt)
        out_shape.append(jax.ShapeDtypeStruct((b, NA_W, l), BF16))
        out_specs.append(pl.BlockSpec((1, NA_W, tm), lambda i, j: (i, 0, j)))
    else:
        out_shape.append(jax.ShapeDtypeStruct((b, l, 2 * NA_W), F32))
        out_specs.append(pl.BlockSpec((1, tm, 2 * NA_W), lambda i, j: (i, j, 0)))
    return pl.pallas_call(
        functools.partial(_inproj_kernel, want_kt=want_kt),
        grid=(b, l // tm),
        in_specs=in_specs,
        out_specs=out_specs,
        out_shape=out_shape,
        compiler_params=_cparams(("parallel", "parallel")),
        name="inproj",
    )(*args)


def _ctx_attn_kernel(q_ref, k_ref, v_ref, o_ref, s_ref):
    scale = NA_DH ** -0.5
    maxes = []
    for h in range(NA_HEADS):
        sl = slice(h * NA_DH, (h + 1) * NA_DH)
        q = q_ref[0, :, sl] * scale
        s = _dot_nt(q, k_ref[0, :, sl])
        s_ref[h] = s
        maxes.append(jnp.max(_lane_fold(jnp.maximum, s), axis=-1, keepdims=True))
    for h in range(NA_HEADS):
        sl = slice(h * NA_DH, (h + 1) * NA_DH)
        p = jnp.exp(s_ref[h] - maxes[h])
        den = jnp.sum(_lane_fold(jnp.add, p), axis=-1, keepdims=True)
        o = _dot(p.astype(BF16), v_ref[0, :, sl]) / den
        o_ref[0, :, sl] = o.astype(BF16)


def _ctx_attention(qkv):
    b, l, _ = qkv.shape
    spec = lambda c: pl.BlockSpec((1, l, NA_W), lambda i, c=c: (i, 0, c))
    return pl.pallas_call(
        _ctx_attn_kernel,
        grid=(b,),
        in_specs=[spec(0), spec(1), spec(2)],
        out_specs=pl.BlockSpec((1, l, NA_W), lambda i: (i, 0, 0)),
        out_shape=jax.ShapeDtypeStruct((b, l, NA_W), BF16),
        scratch_shapes=[pltpu.VMEM((NA_HEADS, l, l), F32)],
        compiler_params=_cparams(("parallel",)),
        name="ctx_attention",
    )(qkv, qkv, qkv)


NA_QROWS = 2
NA_KROWS = WIN_R + NA_QROWS


def _na_window_start(p, rows):
    return jnp.clip(NA_QROWS * p - WIN_R // 2, 0, rows - NA_KROWS)


def _lane_fold(op, *arrays):
    tiles = [a[:, i:i + LANES] for a in arrays for i in range(0, a.shape[1], LANES)]
    while len(tiles) > 1:
        tiles = [op(tiles[i], tiles[i + 1]) if i + 1 < len(tiles) else tiles[i]
                 for i in range(0, len(tiles), 2)]
    return tiles[0]


def _na_kernel(q_ref, kt_ref, v_ref, ckt_ref, cv_ref, bias_ref, o_ref, s_ref, *, rows):
    ws = _na_window_start(pl.program_id(1), rows)
    start = pl.multiple_of(ws * GRID_W, NA_QROWS * GRID_W)
    n_loc = NA_KROWS * GRID_W
    scale = NA_DH ** -0.5
    maxes = []
    for h in range(NA_HEADS):
        sl = slice(h * NA_DH, (h + 1) * NA_DH)
        q = q_ref[0, :, sl] * scale
        s_loc = _dot(q, kt_ref[0, sl, pl.ds(start, n_loc)]) + bias_ref[0, h]
        s_ctx = _dot(q, ckt_ref[0, sl, :])
        s_ref[h, :, :n_loc] = s_loc
        s_ref[h, :, n_loc:] = s_ctx
        maxes.append(jnp.max(_lane_fold(jnp.maximum, s_loc, s_ctx), axis=-1, keepdims=True))
    for h in range(NA_HEADS):
        sl = slice(h * NA_DH, (h + 1) * NA_DH)
        p_loc = jnp.exp(s_ref[h, :, :n_loc] - maxes[h])
        p_ctx = jnp.exp(s_ref[h, :, n_loc:] - maxes[h])
        den = jnp.sum(_lane_fold(jnp.add, p_loc, p_ctx), axis=-1, keepdims=True)
        o = (_dot(p_loc.astype(BF16), v_ref[0, pl.ds(start, n_loc), sl])
             + _dot(p_ctx.astype(BF16), cv_ref[0, :, sl])) / den
        o_ref[0, :, sl] = o.astype(BF16)


NA_EDGE = 2


def _na_variant(p, n_pairs):
    return jnp.where(p < NA_EDGE, p, jnp.where(p >= n_pairs - NA_EDGE, p - (n_pairs - 2 * NA_EDGE - 1), NA_EDGE))


def _na_bias_slabs(rpb, rows):
    n_pairs = rows // NA_QROWS
    ps = np.array([0, 1, 2, n_pairs - 2, n_pairs - 1])
    r = NA_QROWS * ps[:, None] + np.arange(NA_QROWS)[None, :]
    rs = np.clip(r - WIN_R // 2, 0, rows - WIN_R)
    ws = np.clip(NA_QROWS * ps - WIN_R // 2, 0, rows - NA_KROWS)
    krow = ws[:, None] + np.arange(NA_KROWS)[None, :]
    row_ok = (krow[:, None, :] >= rs[:, :, None]) & (krow[:, None, :] < rs[:, :, None] + WIN_R)
    dr = np.clip(krow[:, None, :] - r[:, :, None] + (WIN_R - 1), 0, 2 * WIN_R - 2)
    j = np.arange(GRID_W)[:, None]
    kc = np.arange(GRID_W)[None, :]
    cs = np.clip(j - WIN_C // 2, 0, GRID_W - WIN_C)
    col_ok = (kc >= cs) & (kc < cs + WIN_C)
    dc = np.clip(kc - j + (WIN_C - 1), 0, 2 * WIN_C - 2)
    planes = jnp.where(jnp.asarray(col_ok)[None, None], rpb.astype(F32)[:, :, dc], -jnp.inf)
    planes = jnp.concatenate([planes, jnp.full_like(planes[:, :1], -jnp.inf)], axis=1)
    dr = np.where(row_ok, dr, 2 * WIN_R - 1)
    tab = jnp.take(planes, jnp.asarray(dr.reshape(-1)), axis=1)
    tab = tab.reshape((NA_HEADS,) + dr.shape + (GRID_W, GRID_W))
    tab = jnp.transpose(tab, (1, 0, 2, 4, 3, 5))
    return tab.reshape(len(ps), NA_HEADS, NA_QROWS * GRID_W, NA_KROWS * GRID_W)


def _na_attention(qkv, kt, ckt, cv, bias):
    b, l, _ = qkv.shape
    rows = l // GRID_W
    n_pairs = rows // NA_QROWS
    lc = cv.shape[1]
    tq = NA_QROWS * GRID_W
    n_loc = NA_KROWS * GRID_W
    return pl.pallas_call(
        functools.partial(_na_kernel, rows=rows),
        grid=(b, n_pairs),
        in_specs=[
            pl.BlockSpec((1, tq, NA_W), lambda i, p: (i, p, 0)),
            pl.BlockSpec((1, NA_W, l), lambda i, p: (i, 0, 0)),
            pl.BlockSpec((1, l, NA_W), lambda i, p: (i, 0, 2)),
            pl.BlockSpec((1, NA_W, lc), lambda i, p: (i, 0, 0)),
            pl.BlockSpec((1, lc, NA_W), lambda i, p: (i, 0, 0)),
            pl.BlockSpec((1, NA_HEADS, tq, n_loc), lambda i, p: (_na_variant(p, n_pairs), 0, 0, 0)),
        ],
        out_specs=pl.BlockSpec((1, tq, NA_W), lambda i, p: (i, p, 0)),
        out_shape=jax.ShapeDtypeStruct((b, l, NA_W), BF16),
        scratch_shapes=[pltpu.VMEM((NA_HEADS, tq, n_loc + lc), F32)],
        compiler_params=_cparams(("parallel", "arbitrary")),
        name="na_attention",
    )(qkv, kt, qkv, ckt, cv, bias)


def _split3(x):
    hi = x.astype(BF16)
    r1 = x - hi.astype(F32)
    mid = r1.astype(BF16)
    lo = (r1 - mid.astype(F32)).astype(BF16)
    return hi, mid, lo


def _block_ref(cum, block, ref_row):
    c = cum.shape[0]
    ref = cum.reshape(c // block, block, HG_D)[:, ref_row:ref_row + 1, :]
    return jnp.broadcast_to(ref, (c // block, block, HG_D)).reshape(c, HG_D)


def _hgrn_level(cum_f, cum_b, qs, k_f, k_b, m, ti, si):
    c = cum_f.shape[0]
    ref_f = _block_ref(cum_f, 2 * m, m - 1)
    ref_b = _block_ref(cum_b, 2 * m, m)
    row = lax.broadcasted_iota(jnp.int32, (c, 1), 0)
    upper = jnp.bitwise_and(row, m) != 0
    dec_f = jnp.exp(jnp.where(upper, cum_f - ref_f, ref_f - cum_f))
    dec_b = jnp.exp(jnp.where(upper, ref_b - cum_b, cum_b - ref_b))
    qm = jnp.concatenate([jnp.where(upper, qs * dec_f, 0.0), jnp.where(upper, 0.0, qs * dec_b)], axis=1)
    km = jnp.concatenate([jnp.where(upper, 0.0, k_f * dec_f), jnp.where(upper, k_b * dec_b, 0.0)], axis=1)
    am = _dot_nt(qm.astype(BF16), km.astype(BF16))
    if 2 * m < c:
        am = jnp.where(jnp.bitwise_xor(ti, si) < 2 * m, am, 0.0)
    return am


def _hgrn_pairwise(cum, qs, k, v, reverse):
    c = cum.shape[0]
    sub_row = lax.broadcasted_iota(jnp.int32, (SUBLANES, 1), 0)
    tiles_per_sub = HG_SUB // SUBLANES
    o_tiles = []
    for blk in range(c // HG_SUB):
        accs = [jnp.zeros((SUBLANES, HG_D), F32) for _ in range(tiles_per_sub)]
        for j in range(HG_SUB):
            s = blk * HG_SUB + j
            cs, ks, vs = cum[s:s + 1, :], k[s:s + 1, :], v[s:s + 1, :]
            for t in range(tiles_per_sub):
                lo_r, hi_r = t * SUBLANES, t * SUBLANES + SUBLANES - 1
                if (lo_r > j) if reverse else (hi_r < j):
                    continue
                base = blk * HG_SUB + lo_r
                x = qs[base:base + SUBLANES, :] * ks * jnp.exp(
                    jnp.minimum(cum[base:base + SUBLANES, :] - cs, 0.0))
                a = jnp.sum(x, axis=-1, keepdims=True)
                if reverse and hi_r > j:
                    a = jnp.where(sub_row + lo_r <= j, a, 0.0)
                elif (not reverse) and lo_r < j:
                    a = jnp.where(sub_row + lo_r >= j, a, 0.0)
                accs[t] = accs[t] + a * vs
        o_tiles.extend(accs)
    return jnp.concatenate(o_tiles, axis=0)


def _hgrn_gates(z, consts):
    loglb, log1mlb, oml = consts
    e = jnp.exp(-jnp.abs(z))
    one_e = 1.0 + e
    logsig = jnp.minimum(z, 0.0) - jnp.log(one_e)
    bb = log1mlb + logsig
    logf = jnp.maximum(loglb, bb) + jnp.log(1.0 + jnp.exp(-jnp.abs(loglb - bb)))
    k = oml * (jnp.where(z >= 0.0, e, 1.0) / one_e)
    return logf, k


def _flat_offsets(cum_f, cum_b):
    return (cum_f - _block_ref(cum_f, HG_FLAT, HG_FLAT // 2 - 1),
            cum_b - _block_ref(cum_b, HG_FLAT, HG_FLAT // 2))


def _hgrn_kernel(q_ref, i_ref, zf_ref, zb_ref, g_ref, lbc_ref, nw_ref, s0_ref,
                 o_ref, s_ref, cum_ref, k_ref, qs_ref, acc_ref, qt_ref, u_ref, de_ref, *, nh):
    l = q_ref.shape[1]
    c = HG_CHUNK
    nc = l // c
    nu = nh * nc
    ti = lax.broadcasted_iota(jnp.int32, (c, c), 0)
    si = lax.broadcasted_iota(jnp.int32, (c, c), 1)

    def unit(u):
        if nh == 1:
            return pl.ds(pl.multiple_of(u * c, c), c), slice(None), slice(None)
        h = u // nc
        rows = pl.ds(pl.multiple_of((u - h * nc) * c, c), c)
        return (rows, pl.ds(pl.multiple_of(h * HG_D, HG_D), HG_D),
                pl.ds(pl.multiple_of(h * 2 * HG_D, 2 * HG_D), 2 * HG_D))

    def prepare(u, gmax):
        rows, lanes, _ = unit(u)
        consts = tuple(lbc_ref[r:r + 1, lanes] for r in range(3))
        q = q_ref[0, rows, lanes]
        qs_ref[rows, lanes] = q * jax.nn.sigmoid(q)
        cums = []
        for d, z_ref in enumerate((zf_ref, zb_ref)):
            logf, k = _hgrn_gates(z_ref[0, rows, lanes], consts)
            k_ref[d, rows, lanes] = k
            tri = jnp.where((si >= ti) if d else (si <= ti), 1.0, 0.0).astype(BF16)
            parts = _dot(tri, jnp.concatenate(_split3(logf), axis=1))
            cum = parts[:, :HG_D] + parts[:, HG_D:2 * HG_D] + parts[:, 2 * HG_D:]
            cum_ref[d, rows, lanes] = cum
            cums.append(cum)
        dq_f, dq_b = _flat_offsets(*cums)
        g = jnp.maximum(jnp.abs(dq_f), jnp.abs(dq_b))
        return jnp.maximum(gmax, jnp.max(g.reshape(c // SUBLANES, SUBLANES, HG_D), axis=0))

    gmax = lax.fori_loop(0, nu, prepare, jnp.zeros((SUBLANES, HG_D), F32), unroll=HG_UNROLL)
    flat_ok = jnp.max(gmax) <= HG_FLAT_RANGE

    def intra(u, flat):
        rows, lanes, lanes2 = unit(u)
        cum_f, cum_b = cum_ref[0, rows, lanes], cum_ref[1, rows, lanes]
        k_f, k_b = k_ref[0, rows, lanes], k_ref[1, rows, lanes]
        qs = qs_ref[rows, lanes]
        v = i_ref[0, rows, lanes]
        vb = v.astype(BF16)
        a = jnp.zeros((c, c), F32)
        m = c // 2
        while m >= (HG_FLAT if flat else HG_SUB):
            a = a + _hgrn_level(cum_f, cum_b, qs, k_f, k_b, m, ti, si)
            m //= 2
        if flat:
            dq_f, dq_b = _flat_offsets(cum_f, cum_b)
            same = jnp.bitwise_xor(ti, si) < HG_FLAT
            a_f = _dot_nt((qs * jnp.exp(dq_f)).astype(BF16), (k_f * jnp.exp(-dq_f)).astype(BF16))
            a_b = _dot_nt((qs * jnp.exp(dq_b)).astype(BF16), (k_b * jnp.exp(-dq_b)).astype(BF16))
            a = a + jnp.where(same & (si <= ti), a_f, 0.0) + jnp.where(same & (si >= ti), a_b, 0.0)
            o = _dot(a.astype(BF16), vb)
        else:
            o = (_dot(a.astype(BF16), vb) + _hgrn_pairwise(cum_f, qs, k_f, v, False)
                 + _hgrn_pairwise(cum_b, qs, k_b, v, True))
        acc_ref[rows, lanes] = o
        edge_f, edge_b = cum_f[c - 1:c, :], cum_b[0:1, :]
        qt_ref[rows, lanes2] = jnp.concatenate([(qs * jnp.exp(cum_f)).astype(BF16),
                                                (qs * jnp.exp(cum_b)).astype(BF16)], axis=1)
        kd = jnp.concatenate([(k_f * jnp.exp(edge_f - cum_f)).astype(BF16),
                              (k_b * jnp.exp(edge_b - cum_b)).astype(BF16)], axis=1)
        u_ref[u] = _dot_tn(vb, kd)
        de_ref[u] = jnp.broadcast_to(jnp.concatenate([jnp.exp(edge_f), jnp.exp(edge_b)], axis=1),
                                     (SUBLANES, 2 * HG_D))

    def run_intra(flat):
        def body(u, carry):
            intra(u, flat)
            return carry
        lax.fori_loop(0, nu, body, 0, unroll=HG_UNROLL if flat else 1)

    pl.when(flat_ok)(lambda: run_intra(True))
    pl.when(jnp.logical_not(flat_ok))(lambda: run_intra(False))

    for h in range(nh):
        def sweep(i, carry, h=h):
            st_f, st_b = carry
            uf, ub = h * nc + i, h * nc + nc - 1 - i
            inc = u_ref[uf, :, :HG_D]
            u_ref[uf, :, :HG_D] = st_f
            st_f = st_f * de_ref[uf, 0:1, :HG_D] + inc
            inc = u_ref[ub, :, HG_D:]
            u_ref[ub, :, HG_D:] = st_b
            st_b = st_b * de_ref[ub, 0:1, HG_D:] + inc
            return st_f, st_b

        st_f, st_b = lax.fori_loop(0, nc, sweep, (s0_ref[0, 0, h].T, s0_ref[0, 1, h].T))
        s_ref[0, 0, h] = st_f.T
        s_ref[0, 1, h] = st_b.T

    def finish(u, carry):
        rows, lanes, lanes2 = unit(u)
        o = acc_ref[rows, lanes] + _dot_nt(qt_ref[rows, lanes2], u_ref[u].astype(BF16))
        o = o * lax.rsqrt(jnp.mean(o * o, axis=-1, keepdims=True) + EPS) * nw_ref[...]
        g = g_ref[0, rows, lanes]
        o_ref[0, rows, lanes] = (o * (g * jax.nn.sigmoid(g))).astype(BF16)
        return carry

    lax.fori_loop(0, nu, finish, 0, unroll=HG_UNROLL)


HG_VMEM_BUDGET = 44 * 1024 * 1024


def _hgrn_heads_per_step(l):
    per_head = l * HG_D * (5 * 2 * 4 + 6 * 4 + 2 * 2 + 8)
    nh = HG_HEADS
    while nh > 1 and nh * per_head > HG_VMEM_BUDGET:
        nh //= 2
    return nh


def _hgrn(hgp, lbc, norm_w, s0):
    b, l, _ = hgp.shape
    nh = _hgrn_heads_per_step(l)
    groups = HG_HEADS // nh
    w = nh * HG_D
    nu = nh * (l // HG_CHUNK)
    col = lambda sec: pl.BlockSpec((1, l, w), lambda i, h, sec=sec: (i, 0, sec * groups + h))
    state_spec = pl.BlockSpec((1, 2, nh, HG_D, HG_D), lambda i, h: (i, 0, h, 0, 0))
    return pl.pallas_call(
        functools.partial(_hgrn_kernel, nh=nh),
        grid=(b, groups),
        in_specs=[col(0), col(1), col(2), col(3), col(4),
                  pl.BlockSpec((SUBLANES, w), lambda i, h: (0, h)),
                  pl.BlockSpec((1, HG_D), lambda i, h: (0, 0)),
                  state_spec],
        out_specs=[pl.BlockSpec((1, l, w), lambda i, h: (i, 0, h)), state_spec],
        out_shape=[jax.ShapeDtypeStruct((b, l, HG_W), BF16),
                   jax.ShapeDtypeStruct((b, 2, HG_HEADS, HG_D, HG_D), F32)],
        scratch_shapes=[pltpu.VMEM((2, l, w), F32),
                        pltpu.VMEM((2, l, w), F32),
                        pltpu.VMEM((l, w), F32),
                        pltpu.VMEM((l, w), F32),
                        pltpu.VMEM((l, 2 * w), BF16),
                        pltpu.VMEM((nu, HG_D, 2 * HG_D), F32),
                        pltpu.VMEM((nu, SUBLANES, 2 * HG_D), F32)],
        compiler_params=_cparams(("parallel", "parallel")),
        name="hgrn",
    )(hgp, hgp, hgp, hgp, hgp, lbc, norm_w.reshape(1, HG_D), s0)


def _mixer_residual(x_ref, att_ref, hg_ref, wo_ref, mod_ref):
    y = _dot(att_ref[0], wo_ref[:NA_W, :]) + _dot(hg_ref[0], wo_ref[NA_W:, :])
    return x_ref[0] + mod_ref[0, 2:3, :] * y


def _mixer_specs(tm, d, mod):
    tile = lambda w: pl.BlockSpec((1, tm, w), lambda i, j, *_: (i, j, 0))
    const = lambda shape: pl.BlockSpec(shape, lambda *_: (0,) * len(shape))
    return [tile(d), tile(NA_W), tile(HG_W), const((NA_W + HG_W, d)), const((1, d)),
            pl.BlockSpec((1, SUBLANES, d), _mod_index(mod))]


def _ffn_kernel(x_ref, att_ref, hg_ref, wo_ref, nw_ref, mod_ref, w1_ref, w3_ref, w2_ref, o_ref,
                h_ref, acc_ref, xn_ref):
    f = pl.program_id(2)

    @pl.when(f == 0)
    def _():
        xn = _mixer_residual(x_ref, att_ref, hg_ref, wo_ref, mod_ref)
        xn_ref[...] = xn
        h = _norm_mod(xn, nw_ref[...], mod_ref[0, 3:4, :], mod_ref[0, 4:5, :])
        h_ref[...] = h.astype(BF16)
        acc_ref[...] = jnp.zeros_like(acc_ref)

    h = h_ref[...]
    a = _dot(h, w1_ref[...])
    g = (a * jax.nn.sigmoid(a)) * _dot(h, w3_ref[...])
    acc_ref[...] += _dot(g.astype(BF16), w2_ref[...])

    @pl.when(f == pl.num_programs(2) - 1)
    def _():
        o_ref[0] = xn_ref[...] + mod_ref[0, 5:6, :] * acc_ref[...]


def _ffn(x, att, hg, w_out, nw, mod, w1, w3, w2, tm=512, tf=1408):
    b, l, d = x.shape
    tm = min(tm, l)
    ff = w1.shape[1]
    tf = min(tf, ff)
    return pl.pallas_call(
        _ffn_kernel,
        grid=(b, l // tm, ff // tf),
        in_specs=_mixer_specs(tm, d, mod) + [
            pl.BlockSpec((d, tf), lambda i, j, f: (0, f)),
            pl.BlockSpec((d, tf), lambda i, j, f: (0, f)),
            pl.BlockSpec((tf, d), lambda i, j, f: (f, 0)),
        ],
        out_specs=pl.BlockSpec((1, tm, d), lambda i, j, f: (i, j, 0)),
        out_shape=jax.ShapeDtypeStruct(x.shape, F32),
        scratch_shapes=[pltpu.VMEM((tm, d), BF16), pltpu.VMEM((tm, d), F32), pltpu.VMEM((tm, d), F32)],
        compiler_params=_cparams(("parallel", "parallel", "arbitrary")),
        name="ffn",
    )(x, att, hg, w_out, nw.reshape(1, d), mod, w1, w3, w2)


def _router_kernel(x_ref, att_ref, hg_ref, wo_ref, nw_ref, mod_ref, r_ref, xn_ref, hp_ref, idx_ref, wt_ref):
    xn = _mixer_residual(x_ref, att_ref, hg_ref, wo_ref, mod_ref)
    xn_ref[0] = xn
    h = _norm_mod(xn, nw_ref[...], mod_ref[0, 3:4, :], mod_ref[0, 4:5, :])
    hh = h.astype(BF16)
    hl = (h - hh.astype(F32)).astype(BF16)
    r = r_ref[...]
    rh = r.astype(BF16)
    rl = (r - rh.astype(F32)).astype(BF16)
    logits = _dot(hh, rh) + _dot(hh, rl) + _dot(hl, rh)
    lane = lax.broadcasted_iota(jnp.int32, logits.shape, 1)
    logits = jnp.where(lane < N_EXPERTS, logits, -jnp.inf)
    m1 = jnp.max(logits, axis=-1, keepdims=True)
    i1 = jnp.min(jnp.where(logits == m1, lane, LANES), axis=-1, keepdims=True)
    rest = jnp.where(lane == i1, -jnp.inf, logits)
    m2 = jnp.max(rest, axis=-1, keepdims=True)
    i2 = jnp.min(jnp.where(rest == m2, lane, LANES), axis=-1, keepdims=True)
    e2 = jnp.exp(m2 - m1)
    den = 1.0 + e2
    idx_ref[0] = jnp.where(lane == 0, i1, jnp.where(lane == 1, i2, 0))
    wt_ref[0] = jnp.where(lane == 0, 1.0 / den, jnp.where(lane == 1, e2 / den, 0.0))
    hp_ref[0] = h


def _router(x, att, hg, w_out, nw, mod, router, tm=512):
    b, l, d = x.shape
    tm = min(tm, l)
    rp = jnp.zeros((d, LANES), F32).at[:, :router.shape[1]].set(router)
    tile = lambda w: pl.BlockSpec((1, tm, w), lambda i, j: (i, j, 0))
    return pl.pallas_call(
        _router_kernel,
        grid=(b, l // tm),
        in_specs=_mixer_specs(tm, d, mod) + [pl.BlockSpec((d, LANES), lambda i, j: (0, 0))],
        out_specs=[tile(d), tile(d), tile(LANES), tile(LANES)],
        out_shape=[jax.ShapeDtypeStruct((b, l, d), F32),
                   jax.ShapeDtypeStruct((b, l, d), F32),
                   jax.ShapeDtypeStruct((b, l, LANES), jnp.int32),
                   jax.ShapeDtypeStruct((b, l, LANES), F32)],
        compiler_params=_cparams(("parallel", "parallel")),
        name="router",
    )(x, att, hg, w_out, nw.reshape(1, d), mod, rp)


MOE_TM = 512
MOE_TG = 1024
TOP_K = 2


def _route_plan(idx, n_experts, tm):
    t = idx.shape[0]
    chosen = idx[:, :, None] == jnp.arange(n_experts, dtype=jnp.int32)[None, None, :]
    sel = jnp.any(chosen, axis=1)
    rank = jnp.cumsum(sel.astype(jnp.int32), axis=0) - 1
    count = rank[-1] + 1
    tiles = (count + tm - 1) // tm
    tile_end = jnp.cumsum(tiles)
    start = (tile_end - tiles) * tm
    dest = jnp.sum(jnp.where(chosen, (start[None, :] + rank)[:, None, :], 0), axis=-1)
    n_tiles_max = (t * TOP_K) // tm + n_experts
    used = jnp.minimum(jnp.arange(n_tiles_max, dtype=jnp.int32), tile_end[-1] - 1)
    tile_expert = jnp.sum((used[:, None] >= tile_end[None, :]).astype(jnp.int32), axis=1)
    return dest.astype(jnp.int32), tile_expert.astype(jnp.int32), tile_end[-1:].astype(jnp.int32)


def _row_copies(dest_ref, src_of, dst_of, sem, n):
    def each(group, fn):
        base = pl.multiple_of(group * SUBLANES, SUBLANES)
        for j in range(SUBLANES):
            for s in range(TOP_K):
                r = base + j
                row = dest_ref[0, s, r]
                fn(pltpu.make_async_copy(src_of(s, r, row), dst_of(s, r, row), sem))

    def start(group, carry):
        each(group, lambda copy: copy.start())
        return carry

    def wait(group, carry):
        each(group, lambda copy: copy.wait())
        return carry

    lax.fori_loop(0, n // SUBLANES, start, 0)
    lax.fori_loop(0, n // SUBLANES, wait, 0)


def _dispatch_kernel(dest_ref, hp_ref, init_ref, xs_ref, sem):
    del init_ref
    _row_copies(dest_ref,
                lambda s, r, row: hp_ref.at[pl.ds(r, 1)],
                lambda s, r, row: xs_ref.at[pl.ds(row, 1)],
                sem, hp_ref.shape[0])


def _dispatch(hp, dest_tiles, n_rows):
    t, w = hp.shape
    return pl.pallas_call(
        _dispatch_kernel,
        grid=(t // MOE_TG,),
        in_specs=[
            pl.BlockSpec((1, TOP_K, MOE_TG), lambda i: (i, 0, 0), memory_space=pltpu.SMEM),
            pl.BlockSpec((MOE_TG, w), lambda i: (i, 0)),
            pl.BlockSpec(memory_space=pl.ANY),
        ],
        out_specs=pl.BlockSpec(memory_space=pl.ANY),
        out_shape=jax.ShapeDtypeStruct((n_rows, w), hp.dtype),
        scratch_shapes=[pltpu.SemaphoreType.DMA],
        input_output_aliases={2: 0},
        compiler_params=_cparams(("arbitrary",)),
        name="moe_dispatch",
    )(dest_tiles, hp, jnp.zeros((n_rows, w), hp.dtype))


def _expert_kernel(te_ref, nt_ref, xs_ref, w1_ref, w3_ref, w2_ref, y_ref):
    del te_ref
    i = pl.program_id(0)

    @pl.when(i < nt_ref[0])
    def _():
        h = xs_ref[...].astype(BF16)
        a = _dot(h, w1_ref[0])
        g = (a * jax.nn.sigmoid(a)) * _dot(h, w3_ref[0])
        y_ref[...] = _dot(g.astype(BF16), w2_ref[0])

    @pl.when(i >= nt_ref[0])
    def _():
        y_ref[...] = jnp.zeros_like(y_ref)


def _experts(xs, tile_expert, n_tiles, w1, w3, w2):
    n_rows, w = xs.shape
    _, d, ff = w1.shape
    return pl.pallas_call(
        _expert_kernel,
        grid_spec=pltpu.PrefetchScalarGridSpec(
            num_scalar_prefetch=2,
            grid=(n_rows // MOE_TM,),
            in_specs=[
                pl.BlockSpec((MOE_TM, w), lambda i, te, nt: (i, 0)),
                pl.BlockSpec((1, d, ff), lambda i, te, nt: (te[i], 0, 0)),
                pl.BlockSpec((1, d, ff), lambda i, te, nt: (te[i], 0, 0)),
                pl.BlockSpec((1, ff, d), lambda i, te, nt: (te[i], 0, 0)),
            ],
            out_specs=pl.BlockSpec((MOE_TM, d), lambda i, te, nt: (i, 0)),
        ),
        out_shape=jax.ShapeDtypeStruct((n_rows, d), F32),
        compiler_params=_cparams(("arbitrary",)),
        name="moe_experts",
    )(tile_expert, n_tiles, xs, w1, w3, w2)


def _combine_kernel(dest_ref, x_ref, mod_ref, wt_ref, y_ref, *refs):
    o_ref, buf_ref, sem = refs[-3:]
    _row_copies(dest_ref,
                lambda s, r, row: y_ref.at[pl.ds(row, 1)],
                lambda s, r, row: buf_ref.at[s, pl.ds(r, 1)],
                sem, x_ref.shape[1])
    wt = wt_ref[0]
    mixed = wt[:, 0:1] * buf_ref[0] + wt[:, 1:2] * buf_ref[1]
    o = x_ref[0] + mod_ref[0, 5:6, :] * mixed
    if len(refs) > 3:
        o = o * lax.rsqrt(jnp.mean(o * o, axis=-1, keepdims=True) + EPS) * refs[0][...]
    o_ref[0] = o


def _combine(x, mod, wts, dest_tiles, y, final_nw=None):
    b, l, d = x.shape
    per_b = l // MOE_TG
    in_specs = [
        pl.BlockSpec((1, TOP_K, MOE_TG), lambda i, j: (i * per_b + j, 0, 0), memory_space=pltpu.SMEM),
        pl.BlockSpec((1, MOE_TG, d), lambda i, j: (i, j, 0)),
        pl.BlockSpec((1, SUBLANES, d), _mod_index(mod)),
        pl.BlockSpec((1, MOE_TG, LANES), lambda i, j: (i, j, 0)),
        pl.BlockSpec(memory_space=pl.ANY),
    ]
    args = [dest_tiles, x, mod, wts, y]
    if final_nw is not None:
        in_specs.append(pl.BlockSpec((1, d), lambda i, j: (0, 0)))
        args.append(final_nw.reshape(1, d))
    return pl.pallas_call(
        _combine_kernel,
        grid=(b, per_b),
        in_specs=in_specs,
        out_specs=pl.BlockSpec((1, MOE_TG, d), lambda i, j: (i, j, 0)),
        out_shape=jax.ShapeDtypeStruct(x.shape, F32),
        scratch_shapes=[pltpu.VMEM((TOP_K, MOE_TG, d), F32), pltpu.SemaphoreType.DMA],
        compiler_params=_cparams(("arbitrary", "arbitrary")),
        name="moe_combine",
    )(*args)


def _moe(x, att, hg, w_out, nw, mod, router, w1, w3, w2, final_nw=None):
    b, l, d = x.shape
    t = b * l
    x, hp, idx, wts = _router(x, att, hg, w_out, nw, mod, router)
    dest, tile_expert, n_tiles = _route_plan(idx.reshape(t, LANES)[:, :TOP_K], w1.shape[0], MOE_TM)
    dest_tiles = dest.reshape(t // MOE_TG, MOE_TG, TOP_K).transpose(0, 2, 1)
    n_rows = t * TOP_K + w1.shape[0] * MOE_TM
    xs = _dispatch(hp.reshape(t, d), dest_tiles, n_rows)
    y = _experts(xs, tile_expert, n_tiles, w1, w3, w2)
    return _combine(x, mod, wts, dest_tiles, y, final_nw)


def _final_norm_kernel(x_ref, nw_ref, o_ref):
    x = x_ref[0]
    o_ref[0] = x * lax.rsqrt(jnp.mean(x * x, axis=-1, keepdims=True) + EPS) * nw_ref[...]


def _final_norm(x, nw, tm=512):
    b, l, d = x.shape
    tm = min(tm, l)
    return pl.pallas_call(
        _final_norm_kernel,
        grid=(b, l // tm),
        in_specs=[pl.BlockSpec((1, tm, d), lambda i, j: (i, j, 0)),
                  pl.BlockSpec((1, d), lambda i, j: (0, 0))],
        out_specs=pl.BlockSpec((1, tm, d), lambda i, j: (i, j, 0)),
        out_shape=jax.ShapeDtypeStruct(x.shape, F32),
        compiler_params=_cparams(("parallel", "parallel")),
        name="final_norm",
    )(x, nw.reshape(1, d))


def _pad_rows(a, rows):
    return jnp.zeros((rows,) + a.shape[1:], a.dtype).at[:a.shape[0]].set(a)


def kernel(x_prompt, x_sample, cache_k, cache_v, state_hgrn, c, c_ctx, norm1_w, norm2_w, w_ada, b_ada,
           w_in, rpb, hg_lower, hg_norm_w, w_out, ffn_w1, ffn_w3, ffn_w2, moe_router, moe_w1, moe_w3,
           moe_w2, final_norm_w):
    depth = w_in.shape[0]
    d = x_prompt.shape[-1]
    nb = x_prompt.shape[0]
    nd = x_sample.shape[0]
    rows = x_sample.shape[1] // GRID_W

    lbs = jnp.cumsum(jax.nn.softmax(hg_lower.astype(F32), axis=0), axis=0)
    lbs = lbs - lbs[0:1]
    lbc = _pad_rows(jnp.stack([jnp.log(lbs), jnp.log1p(-lbs), 1.0 - lbs], axis=1).transpose(1, 0, 2),
                    SUBLANES).transpose(1, 0, 2)

    cond = _pad_rows(jnp.concatenate([c, c_ctx[None, :]], axis=0), SUBLANES)
    mods = _modulation(cond, w_ada, b_ada)
    mods = _pad_rows(mods.reshape(depth, SUBLANES, 6, d).transpose(2, 0, 1, 3),
                     SUBLANES).transpose(1, 2, 0, 3)

    w_in_b = w_in.astype(BF16)
    w_out_b = w_out.astype(BF16)
    ffn_w1_b, ffn_w3_b, ffn_w2_b = ffn_w1.astype(BF16), ffn_w3.astype(BF16), ffn_w2.astype(BF16)
    moe_w1_b, moe_w3_b, moe_w2_b = moe_w1.astype(BF16), moe_w3.astype(BF16), moe_w2.astype(BF16)
    wkt_b = jnp.swapaxes(w_in[:, :, NA_W:2 * NA_W], 1, 2).astype(BF16)
    lc = cache_k.shape[2]
    ckt_b = jnp.swapaxes(cache_k.astype(BF16).reshape(nd, depth, lc, NA_W), 2, 3)
    cv_b = cache_v.astype(BF16).reshape(nd, depth, lc, NA_W)
    zero_state = jnp.zeros((nb, 2, HG_HEADS, HG_D, HG_D), F32)

    def layer(x, seq_shape, mod, l, attend, s0, wkt):
        tok = lambda a: a.reshape(x.shape[:2] + a.shape[-1:])
        seq = lambda a: a.reshape(seq_shape + a.shape[-1:])
        want_kv = wkt is None
        outs = _inproj(x, norm1_w[l], mod, w_in_b[l], wkt)
        att = attend(seq(outs[0]), outs[2])
        hg, s_fin = _hgrn(seq(outs[1]), lbc[l], hg_norm_w[l], s0)
        mixed = (x, tok(att), tok(hg), w_out_b[l], norm2_w[l], mod)
        i = l // 2
        if l % 2 == 0:
            x = _ffn(*mixed, ffn_w1_b[i], ffn_w3_b[i], ffn_w2_b[i])
        else:
            x = _moe(*mixed, moe_router[i], moe_w1_b[i], moe_w3_b[i], moe_w2_b[i],
                     final_norm_w if (l == depth - 1) else None)
        return x, (seq(outs[2]) if want_kv else None), s_fin

    seq = x_prompt.shape[1]
    xp, xs = x_prompt.reshape(1, nb * seq, d), x_sample
    ks_out, vs_out, ss_out = [], [], []
    for l in range(depth):
        mod_p = mods[l, nd:nd + 1]
        mod_s = mods[l, :nd]
        xp, kv, s_fin = layer(xp, (nb, seq), mod_p, l, lambda qkv, _: _ctx_attention(qkv),
                              zero_state, None)
        ks_out.append(kv[..., :NA_W])
        vs_out.append(kv[..., NA_W:])
        ss_out.append(s_fin)
        bias = _na_bias_slabs(rpb[l], rows)
        attend_lat = lambda qkv, kt, l=l, bias=bias: _na_attention(qkv, kt, ckt_b[:, l], cv_b[:, l], bias)
        xs, _, _ = layer(xs, x_sample.shape[:2], mod_s, l, attend_lat,
                         state_hgrn[:, l].astype(F32), wkt_b[l])

    if depth % 2:
        xp, xs = _final_norm(xp, final_norm_w), _final_norm(xs, final_norm_w)
    y_prompt, y_sample = xp.reshape(x_prompt.shape), xs
    new_cache_k = jnp.stack(ks_out, axis=1).reshape(nb, depth, seq, NA_HEADS, NA_DH)
    new_cache_v = jnp.stack(vs_out, axis=1).reshape(nb, depth, seq, NA_HEADS, NA_DH)
    new_state = jnp.stack(ss_out, axis=1).astype(x_prompt.dtype)
    return (y_prompt, y_sample, new_cache_k, new_cache_v, new_state)
```

```python
import functools

import numpy as np
import jax
import jax.numpy as jnp
from jax import lax
from jax.experimental import pallas as pl
from jax.experimental.pallas import tpu as pltpu

F32 = jnp.float32
BF16 = jnp.bfloat16

EPS = 1e-6
NA_HEADS = 8
NA_DH = 64
NA_W = NA_HEADS * NA_DH
GRID_W = 64
WIN_R = 8
WIN_C = 16
HG_HEADS = 4
HG_D = 128
HG_W = HG_HEADS * HG_D
N_EXPERTS = 8
LANES = 128
SUBLANES = 8
VMEM_LIMIT = 56 * 1024 * 1024

HG_CHUNK = 128
HG_SUB = 16
HG_UNROLL = 4
HG_FLAT = 32
HG_FLAT_RANGE = 60.0


def _cparams(sem):
    return pltpu.CompilerParams(dimension_semantics=sem, vmem_limit_bytes=VMEM_LIMIT)


def _dot(a, b):
    return jnp.dot(a, b, preferred_element_type=F32)


def _dot_nt(a, b):
    return lax.dot_general(a, b, (((1,), (1,)), ((), ())), preferred_element_type=F32)


def _dot_tn(a, b):
    return lax.dot_general(a, b, (((0,), (0,)), ((), ())), preferred_element_type=F32)


def _norm_mod(x, nw, shift, scale):
    ms = jnp.mean(x * x, axis=-1, keepdims=True)
    y = x * lax.rsqrt(ms + EPS) * nw
    return y * (1.0 + scale) + shift


def _mod_kernel(cond_ref, w_ref, b_ref, o_ref):
    c = cond_ref[...]
    s = c * jax.nn.sigmoid(c)
    o_ref[0] = _dot(s.astype(BF16), w_ref[0].astype(BF16)) + b_ref[0]


def _modulation(cond, w_ada, b_ada, tn=1536):
    depth, d, n = w_ada.shape
    rows = cond.shape[0]
    return pl.pallas_call(
        _mod_kernel,
        grid=(depth, n // tn),
        in_specs=[
            pl.BlockSpec((rows, d), lambda l, j: (0, 0)),
            pl.BlockSpec((1, d, tn), lambda l, j: (l, 0, j)),
            pl.BlockSpec((1, 1, tn), lambda l, j: (l, 0, j)),
        ],
        out_specs=pl.BlockSpec((1, rows, tn), lambda l, j: (l, 0, j)),
        out_shape=jax.ShapeDtypeStruct((depth, rows, n), F32),
        compiler_params=_cparams(("parallel", "parallel")),
        name="modulation",
    )(cond, w_ada, b_ada.reshape(depth, 1, n))


def _mod_index(mod):
    if mod.shape[0] == 1:
        return lambda b, *_: (0, 0, 0)
    return lambda b, *_: (b, 0, 0)


def _inproj_kernel(x_ref, nw_ref, mod_ref, w_ref, *refs, want_kt):
    if want_kt:
        wkt_ref, qkv_ref, hg_ref, kt_ref = refs
    else:
        qkv_ref, hg_ref, kc_ref, vc_ref = refs[2:]
    h = _norm_mod(x_ref[0], nw_ref[...], mod_ref[0, 0:1, :], mod_ref[0, 1:2, :]).astype(BF16)
    p = _dot(h, w_ref[...])
    qkv_ref[0] = p[:, :3 * NA_W].astype(BF16)
    hg_ref[0] = p[:, 3 * NA_W:]
    if want_kt:
        kt_ref[0] = _dot_nt(wkt_ref[...], h).astype(BF16)
    else:
        kc_ref[...] = p[:, NA_W:2 * NA_W].reshape(kc_ref.shape)
        vc_ref[...] = p[:, 2 * NA_W:3 * NA_W].reshape(vc_ref.shape)


def _inproj(x, nw, mod, w, wkt=None, caches=None, layer=0, tm=512):
    b, l, d = x.shape
    n = w.shape[1]
    want_kt = wkt is not None
    out_shape = [jax.ShapeDtypeStruct((b, l, 3 * NA_W), BF16),
                 jax.ShapeDtypeStruct((b, l, n - 3 * NA_W), F32)]
    out_specs = [pl.BlockSpec((1, tm, 3 * NA_W), lambda i, j: (i, j, 0)),
                 pl.BlockSpec((1, tm, n - 3 * NA_W), lambda i, j: (i, j, 0))]
    in_specs = [pl.BlockSpec((1, tm, d), lambda i, j: (i, j, 0)),
                pl.BlockSpec((1, d), lambda i, j: (0, 0)),
                pl.BlockSpec((1, SUBLANES, d), _mod_index(mod)),
                pl.BlockSpec((d, n), lambda i, j: (0, 0))]
    args = [x, nw.reshape(1, d), mod, w]
    if want_kt:
        in_specs.append(pl.BlockSpec((NA_W, d), lambda i, j: (0, 0)))
        args.append(wkt)
        out_shape.append(jax.ShapeDtypeStruct((b, NA_W, l), BF16))
        out_specs.append(pl.BlockSpec((1, NA_W, tm), lambda i, j: (i, 0, j)))
        aliases = {}
    else:
        seq = caches[0].shape[2]
        per_tile = tm // seq
        assert b == 1 and per_tile * seq == tm
        for cache in caches:
            in_specs.append(pl.BlockSpec(memory_space=pl.ANY))
            args.append(cache)
            out_shape.append(jax.ShapeDtypeStruct(cache.shape, cache.dtype))
            out_specs.append(pl.BlockSpec((per_tile, 1, seq, NA_W), lambda i, j: (j, layer, 0, 0)))
        aliases = {4: 2, 5: 3}
    return pl.pallas_call(
        functools.partial(_inproj_kernel, want_kt=want_kt),
        grid=(b, l // tm),
        in_specs=in_specs,
        out_specs=out_specs,
        out_shape=out_shape,
        input_output_aliases=aliases,
        compiler_params=_cparams(("parallel", "parallel")),
        name="inproj",
    )(*args)


def _ctx_attn_kernel(q_ref, k_ref, v_ref, o_ref, s_ref):
    scale = NA_DH ** -0.5
    maxes = []
    for h in range(NA_HEADS):
        sl = slice(h * NA_DH, (h + 1) * NA_DH)
        q = q_ref[0, :, sl] * scale
        s = _dot_nt(q, k_ref[0, :, sl])
        s_ref[h] = s
        maxes.append(jnp.max(_lane_fold(jnp.maximum, s), axis=-1, keepdims=True))
    for h in range(NA_HEADS):
        sl = slice(h * NA_DH, (h + 1) * NA_DH)
        p = jnp.exp(s_ref[h] - maxes[h])
        den = jnp.sum(_lane_fold(jnp.add, p), axis=-1, keepdims=True)
        o = _dot(p.astype(BF16), v_ref[0, :, sl]) / den
        o_ref[0, :, sl] = o.astype(BF16)


def _ctx_attention(qkv):
    b, l, _ = qkv.shape
    spec = lambda c: pl.BlockSpec((1, l, NA_W), lambda i, c=c: (i, 0, c))
    return pl.pallas_call(
        _ctx_attn_kernel,
        grid=(b,),
        in_specs=[spec(0), spec(1), spec(2)],
        out_specs=pl.BlockSpec((1, l, NA_W), lambda i: (i, 0, 0)),
        out_shape=jax.ShapeDtypeStruct((b, l, NA_W), BF16),
        scratch_shapes=[pltpu.VMEM((NA_HEADS, l, l), F32)],
        compiler_params=_cparams(("parallel",)),
        name="ctx_attention",
    )(qkv, qkv, qkv)


NA_QROWS = 2
NA_KROWS = WIN_R + NA_QROWS


def _na_window_start(p, rows):
    return jnp.clip(NA_QROWS * p - WIN_R // 2, 0, rows - NA_KROWS)


def _lane_fold(op, *arrays):
    tiles = [a[:, i:i + LANES] for a in arrays for i in range(0, a.shape[1], LANES)]
    while len(tiles) > 1:
        tiles = [op(tiles[i], tiles[i + 1]) if i + 1 < len(tiles) else tiles[i]
                 for i in range(0, len(tiles), 2)]
    return tiles[0]


def _na_kernel(q_ref, kt_ref, v_ref, ckt_ref, cv_ref, bias_ref, o_ref, s_ref, *, rows):
    ws = _na_window_start(pl.program_id(1), rows)
    start = pl.multiple_of(ws * GRID_W, NA_QROWS * GRID_W)
    n_loc = NA_KROWS * GRID_W
    scale = NA_DH ** -0.5
    maxes = []
    for h in range(NA_HEADS):
        sl = slice(h * NA_DH, (h + 1) * NA_DH)
        q = q_ref[0, :, sl] * scale
        s_loc = _dot(q, kt_ref[0, sl, pl.ds(start, n_loc)]) + bias_ref[0, h]
        s_ctx = _dot(q, ckt_ref[0, sl, :])
        s_ref[h, :, :n_loc] = s_loc
        s_ref[h, :, n_loc:] = s_ctx
        maxes.append(jnp.max(_lane_fold(jnp.maximum, s_loc, s_ctx), axis=-1, keepdims=True))
    for h in range(NA_HEADS):
        sl = slice(h * NA_DH, (h + 1) * NA_DH)
        p_loc = jnp.exp(s_ref[h, :, :n_loc] - maxes[h])
        p_ctx = jnp.exp(s_ref[h, :, n_loc:] - maxes[h])
        den = jnp.sum(_lane_fold(jnp.add, p_loc, p_ctx), axis=-1, keepdims=True)
        o = (_dot(p_loc.astype(BF16), v_ref[0, pl.ds(start, n_loc), sl])
             + _dot(p_ctx.astype(BF16), cv_ref[0, :, sl])) / den
        o_ref[0, :, sl] = o.astype(BF16)


NA_EDGE = 2


def _na_variant(p, n_pairs):
    return jnp.where(p < NA_EDGE, p, jnp.where(p >= n_pairs - NA_EDGE, p - (n_pairs - 2 * NA_EDGE - 1), NA_EDGE))


def _na_bias_slabs(rpb, rows):
    n_pairs = rows // NA_QROWS
    ps = np.array([0, 1, 2, n_pairs - 2, n_pairs - 1])
    r = NA_QROWS * ps[:, None] + np.arange(NA_QROWS)[None, :]
    rs = np.clip(r - WIN_R // 2, 0, rows - WIN_R)
    ws = np.clip(NA_QROWS * ps - WIN_R // 2, 0, rows - NA_KROWS)
    krow = ws[:, None] + np.arange(NA_KROWS)[None, :]
    row_ok = (krow[:, None, :] >= rs[:, :, None]) & (krow[:, None, :] < rs[:, :, None] + WIN_R)
    dr = np.clip(krow[:, None, :] - r[:, :, None] + (WIN_R - 1), 0, 2 * WIN_R - 2)
    j = np.arange(GRID_W)[:, None]
    kc = np.arange(GRID_W)[None, :]
    cs = np.clip(j - WIN_C // 2, 0, GRID_W - WIN_C)
    col_ok = (kc >= cs) & (kc < cs + WIN_C)
    dc = np.clip(kc - j + (WIN_C - 1), 0, 2 * WIN_C - 2)
    planes = jnp.where(jnp.asarray(col_ok)[None, None], rpb.astype(F32)[:, :, dc], -jnp.inf)
    planes = jnp.concatenate([planes, jnp.full_like(planes[:, :1], -jnp.inf)], axis=1)
    dr = np.where(row_ok, dr, 2 * WIN_R - 1)
    tab = jnp.take(planes, jnp.asarray(dr.reshape(-1)), axis=1)
    tab = tab.reshape((NA_HEADS,) + dr.shape + (GRID_W, GRID_W))
    tab = jnp.transpose(tab, (1, 0, 2, 4, 3, 5))
    return tab.reshape(len(ps), NA_HEADS, NA_QROWS * GRID_W, NA_KROWS * GRID_W)


def _na_attention(qkv, kt, ckt, cv, bias):
    b, l, _ = qkv.shape
    rows = l // GRID_W
    n_pairs = rows // NA_QROWS
    lc = cv.shape[1]
    tq = NA_QROWS * GRID_W
    n_loc = NA_KROWS * GRID_W
    return pl.pallas_call(
        functools.partial(_na_kernel, rows=rows),
        grid=(b, n_pairs),
        in_specs=[
            pl.BlockSpec((1, tq, NA_W), lambda i, p: (i, p, 0)),
            pl.BlockSpec((1, NA_W, l), lambda i, p: (i, 0, 0)),
            pl.BlockSpec((1, l, NA_W), lambda i, p: (i, 0, 2)),
            pl.BlockSpec((1, NA_W, lc), lambda i, p: (i, 0, 0)),
            pl.BlockSpec((1, lc, NA_W), lambda i, p: (i, 0, 0)),
            pl.BlockSpec((1, NA_HEADS, tq, n_loc), lambda i, p: (_na_variant(p, n_pairs), 0, 0, 0)),
        ],
        out_specs=pl.BlockSpec((1, tq, NA_W), lambda i, p: (i, p, 0)),
        out_shape=jax.ShapeDtypeStruct((b, l, NA_W), BF16),
        scratch_shapes=[pltpu.VMEM((NA_HEADS, tq, n_loc + lc), F32)],
        compiler_params=_cparams(("parallel", "arbitrary")),
        name="na_attention",
    )(qkv, kt, qkv, ckt, cv, bias)


def _split2(x):
    hi = x.astype(BF16)
    lo = (x - hi.astype(F32)).astype(BF16)
    return hi, lo


def _block_ref(cum, block, ref_row):
    c = cum.shape[0]
    ref = cum.reshape(c // block, block, HG_D)[:, ref_row:ref_row + 1, :]
    return jnp.broadcast_to(ref, (c // block, block, HG_D)).reshape(c, HG_D)


def _hgrn_level(cum_f, cum_b, qs, k_f, k_b, m, ti, si):
    c = cum_f.shape[0]
    ref_f = _block_ref(cum_f, 2 * m, m - 1)
    ref_b = _block_ref(cum_b, 2 * m, m)
    row = lax.broadcasted_iota(jnp.int32, (c, 1), 0)
    upper = jnp.bitwise_and(row, m) != 0
    dec_f = jnp.exp(jnp.where(upper, cum_f - ref_f, ref_f - cum_f))
    dec_b = jnp.exp(jnp.where(upper, ref_b - cum_b, cum_b - ref_b))
    qm = jnp.concatenate([jnp.where(upper, qs * dec_f, 0.0), jnp.where(upper, 0.0, qs * dec_b)], axis=1)
    km = jnp.concatenate([jnp.where(upper, 0.0, k_f * dec_f), jnp.where(upper, k_b * dec_b, 0.0)], axis=1)
    am = _dot_nt(qm.astype(BF16), km.astype(BF16))
    if 2 * m < c:
        am = jnp.where(jnp.bitwise_xor(ti, si) < 2 * m, am, 0.0)
    return am


def _hgrn_pairwise(cum, qs, k, v, reverse):
    c = cum.shape[0]
    sub_row = lax.broadcasted_iota(jnp.int32, (SUBLANES, 1), 0)
    tiles_per_sub = HG_SUB // SUBLANES
    o_tiles = []
    for blk in range(c // HG_SUB):
        accs = [jnp.zeros((SUBLANES, HG_D), F32) for _ in range(tiles_per_sub)]
        for j in range(HG_SUB):
            s = blk * HG_SUB + j
            cs, ks, vs = cum[s:s + 1, :], k[s:s + 1, :], v[s:s + 1, :]
            for t in range(tiles_per_sub):
                lo_r, hi_r = t * SUBLANES, t * SUBLANES + SUBLANES - 1
                if (lo_r > j) if reverse else (hi_r < j):
                    continue
                base = blk * HG_SUB + lo_r
                x = qs[base:base + SUBLANES, :] * ks * jnp.exp(
                    jnp.minimum(cum[base:base + SUBLANES, :] - cs, 0.0))
                a = jnp.sum(x, axis=-1, keepdims=True)
                if reverse and hi_r > j:
                    a = jnp.where(sub_row + lo_r <= j, a, 0.0)
                elif (not reverse) and lo_r < j:
                    a = jnp.where(sub_row + lo_r >= j, a, 0.0)
                accs[t] = accs[t] + a * vs
        o_tiles.extend(accs)
    return jnp.concatenate(o_tiles, axis=0)


def _hgrn_gates(z, consts):
    lb, log1mlb, oml = consts
    e = jnp.exp(-jnp.abs(z))
    inv = 1.0 / (1.0 + e)
    pos = z >= 0.0
    f = lb + oml * (jnp.where(pos, 1.0, e) * inv)
    logf = jnp.where(f > 0.0, jnp.log(f), log1mlb + z)
    k = oml * (jnp.where(pos, e, 1.0) * inv)
    return logf, k


def _flat_offsets(cum_f, cum_b):
    return (cum_f - _block_ref(cum_f, HG_FLAT, HG_FLAT // 2 - 1),
            cum_b - _block_ref(cum_b, HG_FLAT, HG_FLAT // 2))


def _hgrn_kernel(q_ref, i_ref, zf_ref, zb_ref, g_ref, lbc_ref, nw_ref, s0_ref,
                 o_ref, s_ref, cum_ref, k_ref, qs_ref, acc_ref, qt_ref, u_ref, de_ref, *, nh):
    l = q_ref.shape[1]
    c = HG_CHUNK
    nc = l // c
    nu = nh * nc
    ti = lax.broadcasted_iota(jnp.int32, (c, c), 0)
    si = lax.broadcasted_iota(jnp.int32, (c, c), 1)

    def unit(u):
        if nh == 1:
            return pl.ds(pl.multiple_of(u * c, c), c), slice(None), slice(None)
        h = u // nc
        rows = pl.ds(pl.multiple_of((u - h * nc) * c, c), c)
        return (rows, pl.ds(pl.multiple_of(h * HG_D, HG_D), HG_D),
                pl.ds(pl.multiple_of(h * 2 * HG_D, 2 * HG_D), 2 * HG_D))

    def prepare(u, gmax):
        rows, lanes, _ = unit(u)
        consts = tuple(lbc_ref[r:r + 1, lanes] for r in range(3))
        q = q_ref[0, rows, lanes]
        qs_ref[rows, lanes] = q * jax.nn.sigmoid(q)
        cums = []
        for d, z_ref in enumerate((zf_ref, zb_ref)):
            logf, k = _hgrn_gates(z_ref[0, rows, lanes], consts)
            k_ref[d, rows, lanes] = k
            tri = jnp.where((si >= ti) if d else (si <= ti), 1.0, 0.0).astype(BF16)
            parts = _dot(tri, jnp.concatenate(_split2(logf), axis=1))
            cum = parts[:, :HG_D] + parts[:, HG_D:]
            cum_ref[d, rows, lanes] = cum
            cums.append(cum)
        dq_f, dq_b = _flat_offsets(*cums)
        g = jnp.maximum(jnp.abs(dq_f), jnp.abs(dq_b))
        return jnp.maximum(gmax, jnp.max(g.reshape(c // SUBLANES, SUBLANES, HG_D), axis=0))

    gmax = lax.fori_loop(0, nu, prepare, jnp.zeros((SUBLANES, HG_D), F32), unroll=HG_UNROLL)
    flat_ok = jnp.max(gmax) <= HG_FLAT_RANGE

    def intra(u, flat):
        rows, lanes, lanes2 = unit(u)
        cum_f, cum_b = cum_ref[0, rows, lanes], cum_ref[1, rows, lanes]
        k_f, k_b = k_ref[0, rows, lanes], k_ref[1, rows, lanes]
        qs = qs_ref[rows, lanes]
        v = i_ref[0, rows, lanes]
        vb = v.astype(BF16)
        a = jnp.zeros((c, c), F32)
        m = c // 2
        while m >= (HG_FLAT if flat else HG_SUB):
            a = a + _hgrn_level(cum_f, cum_b, qs, k_f, k_b, m, ti, si)
            m //= 2
        if flat:
            dq_f, dq_b = _flat_offsets(cum_f, cum_b)
            same = jnp.bitwise_xor(ti, si) < HG_FLAT
            a_f = _dot_nt((qs * jnp.exp(dq_f)).astype(BF16), (k_f * jnp.exp(-dq_f)).astype(BF16))
            a_b = _dot_nt((qs * jnp.exp(dq_b)).astype(BF16), (k_b * jnp.exp(-dq_b)).astype(BF16))
            a = a + jnp.where(same & (si <= ti), a_f, 0.0) + jnp.where(same & (si >= ti), a_b, 0.0)
            o = _dot(a.astype(BF16), vb)
        else:
            o = (_dot(a.astype(BF16), vb) + _hgrn_pairwise(cum_f, qs, k_f, v, False)
                 + _hgrn_pairwise(cum_b, qs, k_b, v, True))
        acc_ref[rows, lanes] = o
        edge_f, edge_b = cum_f[c - 1:c, :], cum_b[0:1, :]
        qt_ref[rows, lanes2] = jnp.concatenate([(qs * jnp.exp(cum_f)).astype(BF16),
                                                (qs * jnp.exp(cum_b)).astype(BF16)], axis=1)
        kd = jnp.concatenate([(k_f * jnp.exp(edge_f - cum_f)).astype(BF16),
                              (k_b * jnp.exp(edge_b - cum_b)).astype(BF16)], axis=1)
        u_ref[u] = _dot_tn(vb, kd)
        de_ref[u] = jnp.broadcast_to(jnp.concatenate([jnp.exp(edge_f), jnp.exp(edge_b)], axis=1),
                                     (SUBLANES, 2 * HG_D))

    def run_intra(flat):
        def body(u, carry):
            intra(u, flat)
            return carry
        lax.fori_loop(0, nu, body, 0, unroll=HG_UNROLL if flat else 1)

    pl.when(flat_ok)(lambda: run_intra(True))
    pl.when(jnp.logical_not(flat_ok))(lambda: run_intra(False))

    for h in range(nh):
        def sweep(i, carry, h=h):
            st_f, st_b = carry
            uf, ub = h * nc + i, h * nc + nc - 1 - i
            inc = u_ref[uf, :, :HG_D]
            u_ref[uf, :, :HG_D] = st_f
            st_f = st_f * de_ref[uf, 0:1, :HG_D] + inc
            inc = u_ref[ub, :, HG_D:]
            u_ref[ub, :, HG_D:] = st_b
            st_b = st_b * de_ref[ub, 0:1, HG_D:] + inc
            return st_f, st_b

        st_f, st_b = lax.fori_loop(0, nc, sweep, (s0_ref[0, 0, h].T, s0_ref[0, 1, h].T))
        s_ref[0, 0, h] = st_f.T
        s_ref[0, 1, h] = st_b.T

    def finish(u, carry):
        rows, lanes, lanes2 = unit(u)
        o = acc_ref[rows, lanes] + _dot_nt(qt_ref[rows, lanes2], u_ref[u].astype(BF16))
        o = o * lax.rsqrt(jnp.mean(o * o, axis=-1, keepdims=True) + EPS) * nw_ref[...]
        g = g_ref[0, rows, lanes]
        o_ref[0, rows, lanes] = (o * (g * jax.nn.sigmoid(g))).astype(BF16)
        return carry

    lax.fori_loop(0, nu, finish, 0, unroll=HG_UNROLL)


HG_VMEM_BUDGET = 44 * 1024 * 1024


def _hgrn_heads_per_step(l):
    per_head = l * HG_D * (5 * 2 * 4 + 6 * 4 + 2 * 2 + 8)
    nh = HG_HEADS
    while nh > 1 and nh * per_head > HG_VMEM_BUDGET:
        nh //= 2
    return nh


def _hgrn(hgp, lbc, norm_w, s0):
    b, l, _ = hgp.shape
    nh = _hgrn_heads_per_step(l)
    groups = HG_HEADS // nh
    w = nh * HG_D
    nu = nh * (l // HG_CHUNK)
    col = lambda sec: pl.BlockSpec((1, l, w), lambda i, h, sec=sec: (i, 0, sec * groups + h))
    state_spec = pl.BlockSpec((1, 2, nh, HG_D, HG_D), lambda i, h: (i, 0, h, 0, 0))
    return pl.pallas_call(
        functools.partial(_hgrn_kernel, nh=nh),
        grid=(b, groups),
        in_specs=[col(0), col(1), col(2), col(3), col(4),
                  pl.BlockSpec((SUBLANES, w), lambda i, h: (0, h)),
                  pl.BlockSpec((1, HG_D), lambda i, h: (0, 0)),
                  state_spec],
        out_specs=[pl.BlockSpec((1, l, w), lambda i, h: (i, 0, h)), state_spec],
        out_shape=[jax.ShapeDtypeStruct((b, l, HG_W), BF16),
                   jax.ShapeDtypeStruct((b, 2, HG_HEADS, HG_D, HG_D), F32)],
        scratch_shapes=[pltpu.VMEM((2, l, w), F32),
                        pltpu.VMEM((2, l, w), F32),
                        pltpu.VMEM((l, w), F32),
                        pltpu.VMEM((l, w), F32),
                        pltpu.VMEM((l, 2 * w), BF16),
                        pltpu.VMEM((nu, HG_D, 2 * HG_D), F32),
                        pltpu.VMEM((nu, SUBLANES, 2 * HG_D), F32)],
        compiler_params=_cparams(("parallel", "parallel")),
        name="hgrn",
    )(hgp, hgp, hgp, hgp, hgp, lbc, norm_w.reshape(1, HG_D), s0)


def _mixer_residual(x_ref, att_ref, hg_ref, wo_ref, mod_ref):
    y = _dot(att_ref[0], wo_ref[:NA_W, :]) + _dot(hg_ref[0], wo_ref[NA_W:, :])
    return x_ref[0] + mod_ref[0, 2:3, :] * y


def _mixer_specs(tm, d, mod):
    tile = lambda w: pl.BlockSpec((1, tm, w), lambda i, j, *_: (i, j, 0))
    const = lambda shape: pl.BlockSpec(shape, lambda *_: (0,) * len(shape))
    return [tile(d), tile(NA_W), tile(HG_W), const((NA_W + HG_W, d)), const((1, d)),
            pl.BlockSpec((1, SUBLANES, d), _mod_index(mod))]


def _ffn_kernel(x_ref, att_ref, hg_ref, wo_ref, nw_ref, mod_ref, w1_ref, w3_ref, w2_ref, o_ref,
                h_ref, acc_ref, xn_ref):
    f = pl.program_id(2)

    @pl.when(f == 0)
    def _():
        xn = _mixer_residual(x_ref, att_ref, hg_ref, wo_ref, mod_ref)
        xn_ref[...] = xn
        h = _norm_mod(xn, nw_ref[...], mod_ref[0, 3:4, :], mod_ref[0, 4:5, :])
        h_ref[...] = h.astype(BF16)
        acc_ref[...] = jnp.zeros_like(acc_ref)

    h = h_ref[...]
    a = _dot(h, w1_ref[...])
    g = (a * jax.nn.sigmoid(a)) * _dot(h, w3_ref[...])
    acc_ref[...] += _dot(g.astype(BF16), w2_ref[...])

    @pl.when(f == pl.num_programs(2) - 1)
    def _():
        o_ref[0] = xn_ref[...] + mod_ref[0, 5:6, :] * acc_ref[...]


def _ffn(x, att, hg, w_out, nw, mod, w1, w3, w2, tm=512, tf=1408):
    b, l, d = x.shape
    tm = min(tm, l)
    ff = w1.shape[1]
    tf = min(tf, ff)
    return pl.pallas_call(
        _ffn_kernel,
        grid=(b, l // tm, ff // tf),
        in_specs=_mixer_specs(tm, d, mod) + [
            pl.BlockSpec((d, tf), lambda i, j, f: (0, f)),
            pl.BlockSpec((d, tf), lambda i, j, f: (0, f)),
            pl.BlockSpec((tf, d), lambda i, j, f: (f, 0)),
        ],
        out_specs=pl.BlockSpec((1, tm, d), lambda i, j, f: (i, j, 0)),
        out_shape=jax.ShapeDtypeStruct(x.shape, F32),
        scratch_shapes=[pltpu.VMEM((tm, d), BF16), pltpu.VMEM((tm, d), F32), pltpu.VMEM((tm, d), F32)],
        compiler_params=_cparams(("parallel", "parallel", "arbitrary")),
        name="ffn",
    )(x, att, hg, w_out, nw.reshape(1, d), mod, w1, w3, w2)


def _router_kernel(x_ref, att_ref, hg_ref, wo_ref, nw_ref, mod_ref, r_ref, xn_ref, hp_ref, idx_ref, wt_ref):
    xn = _mixer_residual(x_ref, att_ref, hg_ref, wo_ref, mod_ref)
    xn_ref[0] = xn
    h = _norm_mod(xn, nw_ref[...], mod_ref[0, 3:4, :], mod_ref[0, 4:5, :])
    hh = h.astype(BF16)
    hl = (h - hh.astype(F32)).astype(BF16)
    r = r_ref[...]
    rh = r.astype(BF16)
    rl = (r - rh.astype(F32)).astype(BF16)
    logits = _dot(hh, rh) + _dot(hh, rl) + _dot(hl, rh)
    lane = lax.broadcasted_iota(jnp.int32, logits.shape, 1)
    logits = jnp.where(lane < N_EXPERTS, logits, -jnp.inf)
    m1 = jnp.max(logits, axis=-1, keepdims=True)
    i1 = jnp.min(jnp.where(logits == m1, lane, LANES), axis=-1, keepdims=True)
    rest = jnp.where(lane == i1, -jnp.inf, logits)
    m2 = jnp.max(rest, axis=-1, keepdims=True)
    i2 = jnp.min(jnp.where(rest == m2, lane, LANES), axis=-1, keepdims=True)
    e2 = jnp.exp(m2 - m1)
    den = 1.0 + e2
    idx_ref[0] = jnp.where(lane == 0, i1, jnp.where(lane == 1, i2, 0))
    wt_ref[0] = jnp.where(lane == 0, 1.0 / den, jnp.where(lane == 1, e2 / den, 0.0))
    hp_ref[0] = h


def _router(x, att, hg, w_out, nw, mod, router, tm=512):
    b, l, d = x.shape
    tm = min(tm, l)
    rp = jnp.zeros((d, LANES), F32).at[:, :router.shape[1]].set(router)
    tile = lambda w: pl.BlockSpec((1, tm, w), lambda i, j: (i, j, 0))
    return pl.pallas_call(
        _router_kernel,
        grid=(b, l // tm),
        in_specs=_mixer_specs(tm, d, mod) + [pl.BlockSpec((d, LANES), lambda i, j: (0, 0))],
        out_specs=[tile(d), tile(d), tile(LANES), tile(LANES)],
        out_shape=[jax.ShapeDtypeStruct((b, l, d), F32),
                   jax.ShapeDtypeStruct((b, l, d), F32),
                   jax.ShapeDtypeStruct((b, l, LANES), jnp.int32),
                   jax.ShapeDtypeStruct((b, l, LANES), F32)],
        compiler_params=_cparams(("parallel", "parallel")),
        name="router",
    )(x, att, hg, w_out, nw.reshape(1, d), mod, rp)


MOE_TM = 512
MOE_TG = 1024
TOP_K = 2


def _route_plan(idx, n_experts, tm):
    t = idx.shape[0]
    chosen = idx[:, :, None] == jnp.arange(n_experts, dtype=jnp.int32)[None, None, :]
    sel = jnp.any(chosen, axis=1)
    rank = jnp.cumsum(sel.astype(jnp.int32), axis=0) - 1
    count = rank[-1] + 1
    tiles = (count + tm - 1) // tm
    tile_end = jnp.cumsum(tiles)
    start = (tile_end - tiles) * tm
    dest = jnp.sum(jnp.where(chosen, (start[None, :] + rank)[:, None, :], 0), axis=-1)
    n_tiles_max = (t * TOP_K) // tm + n_experts
    used = jnp.minimum(jnp.arange(n_tiles_max, dtype=jnp.int32), tile_end[-1] - 1)
    tile_expert = jnp.sum((used[:, None] >= tile_end[None, :]).astype(jnp.int32), axis=1)
    return dest.astype(jnp.int32), tile_expert.astype(jnp.int32), tile_end[-1:].astype(jnp.int32)


def _row_copies(dest_ref, src_of, dst_of, sem, n):
    def each(group, fn):
        base = pl.multiple_of(group * SUBLANES, SUBLANES)
        for j in range(SUBLANES):
            for s in range(TOP_K):
                r = base + j
                row = dest_ref[0, s, r]
                fn(pltpu.make_async_copy(src_of(s, r, row), dst_of(s, r, row), sem))

    def start(group, carry):
        each(group, lambda copy: copy.start())
        return carry

    def wait(group, carry):
        each(group, lambda copy: copy.wait())
        return carry

    lax.fori_loop(0, n // SUBLANES, start, 0)
    lax.fori_loop(0, n // SUBLANES, wait, 0)


def _dispatch_kernel(dest_ref, hp_ref, init_ref, xs_ref, sem):
    del init_ref
    _row_copies(dest_ref,
                lambda s, r, row: hp_ref.at[pl.ds(r, 1)],
                lambda s, r, row: xs_ref.at[pl.ds(row, 1)],
                sem, hp_ref.shape[0])


def _dispatch(hp, dest_tiles, n_rows):
    t, w = hp.shape
    return pl.pallas_call(
        _dispatch_kernel,
        grid=(t // MOE_TG,),
        in_specs=[
            pl.BlockSpec((1, TOP_K, MOE_TG), lambda i: (i, 0, 0), memory_space=pltpu.SMEM),
            pl.BlockSpec((MOE_TG, w), lambda i: (i, 0)),
            pl.BlockSpec(memory_space=pl.ANY),
        ],
        out_specs=pl.BlockSpec(memory_space=pl.ANY),
        out_shape=jax.ShapeDtypeStruct((n_rows, w), hp.dtype),
        scratch_shapes=[pltpu.SemaphoreType.DMA],
        input_output_aliases={2: 0},
        compiler_params=_cparams(("arbitrary",)),
        name="moe_dispatch",
    )(dest_tiles, hp, jnp.zeros((n_rows, w), hp.dtype))


def _expert_kernel(te_ref, nt_ref, xs_ref, w1_ref, w3_ref, w2_ref, y_ref):
    del te_ref
    i = pl.program_id(0)

    @pl.when(i < nt_ref[0])
    def _():
        h = xs_ref[...].astype(BF16)
        a = _dot(h, w1_ref[0])
        g = (a * jax.nn.sigmoid(a)) * _dot(h, w3_ref[0])
        y_ref[...] = _dot(g.astype(BF16), w2_ref[0])

    @pl.when(i >= nt_ref[0])
    def _():
        y_ref[...] = jnp.zeros_like(y_ref)


def _experts(xs, tile_expert, n_tiles, w1, w3, w2):
    n_rows, w = xs.shape
    _, d, ff = w1.shape
    return pl.pallas_call(
        _expert_kernel,
        grid_spec=pltpu.PrefetchScalarGridSpec(
            num_scalar_prefetch=2,
            grid=(n_rows // MOE_TM,),
            in_specs=[
                pl.BlockSpec((MOE_TM, w), lambda i, te, nt: (i, 0)),
                pl.BlockSpec((1, d, ff), lambda i, te, nt: (te[i], 0, 0)),
                pl.BlockSpec((1, d, ff), lambda i, te, nt: (te[i], 0, 0)),
                pl.BlockSpec((1, ff, d), lambda i, te, nt: (te[i], 0, 0)),
            ],
            out_specs=pl.BlockSpec((MOE_TM, d), lambda i, te, nt: (i, 0)),
        ),
        out_shape=jax.ShapeDtypeStruct((n_rows, d), F32),
        compiler_params=_cparams(("arbitrary",)),
        name="moe_experts",
    )(tile_expert, n_tiles, xs, w1, w3, w2)


def _combine_kernel(dest_ref, x_ref, mod_ref, wt_ref, y_ref, *refs):
    o_ref, buf_ref, sem = refs[-3:]
    _row_copies(dest_ref,
                lambda s, r, row: y_ref.at[pl.ds(row, 1)],
                lambda s, r, row: buf_ref.at[s, pl.ds(r, 1)],
                sem, x_ref.shape[1])
    wt = wt_ref[0]
    mixed = wt[:, 0:1] * buf_ref[0] + wt[:, 1:2] * buf_ref[1]
    o = x_ref[0] + mod_ref[0, 5:6, :] * mixed
    if len(refs) > 3:
        o = o * lax.rsqrt(jnp.mean(o * o, axis=-1, keepdims=True) + EPS) * refs[0][...]
    o_ref[0] = o


def _combine(x, mod, wts, dest_tiles, y, final_nw=None):
    b, l, d = x.shape
    per_b = l // MOE_TG
    in_specs = [
        pl.BlockSpec((1, TOP_K, MOE_TG), lambda i, j: (i * per_b + j, 0, 0), memory_space=pltpu.SMEM),
        pl.BlockSpec((1, MOE_TG, d), lambda i, j: (i, j, 0)),
        pl.BlockSpec((1, SUBLANES, d), _mod_index(mod)),
        pl.BlockSpec((1, MOE_TG, LANES), lambda i, j: (i, j, 0)),
        pl.BlockSpec(memory_space=pl.ANY),
    ]
    args = [dest_tiles, x, mod, wts, y]
    if final_nw is not None:
        in_specs.append(pl.BlockSpec((1, d), lambda i, j: (0, 0)))
        args.append(final_nw.reshape(1, d))
    return pl.pallas_call(
        _combine_kernel,
        grid=(b, per_b),
        in_specs=in_specs,
        out_specs=pl.BlockSpec((1, MOE_TG, d), lambda i, j: (i, j, 0)),
        out_shape=jax.ShapeDtypeStruct(x.shape, F32),
        scratch_shapes=[pltpu.VMEM((TOP_K, MOE_TG, d), F32), pltpu.SemaphoreType.DMA],
        compiler_params=_cparams(("arbitrary", "arbitrary")),
        name="moe_combine",
    )(*args)


def _moe(x, att, hg, w_out, nw, mod, router, w1, w3, w2, final_nw=None):
    b, l, d = x.shape
    t = b * l
    x, hp, idx, wts = _router(x, att, hg, w_out, nw, mod, router)
    dest, tile_expert, n_tiles = _route_plan(idx.reshape(t, LANES)[:, :TOP_K], w1.shape[0], MOE_TM)
    dest_tiles = dest.reshape(t // MOE_TG, MOE_TG, TOP_K).transpose(0, 2, 1)
    n_rows = t * TOP_K + w1.shape[0] * MOE_TM
    xs = _dispatch(hp.reshape(t, d), dest_tiles, n_rows)
    y = _experts(xs, tile_expert, n_tiles, w1, w3, w2)
    return _combine(x, mod, wts, dest_tiles, y, final_nw)


def _final_norm_kernel(x_ref, nw_ref, o_ref):
    x = x_ref[0]
    o_ref[0] = x * lax.rsqrt(jnp.mean(x * x, axis=-1, keepdims=True) + EPS) * nw_ref[...]


def _final_norm(x, nw, tm=512):
    b, l, d = x.shape
    tm = min(tm, l)
    return pl.pallas_call(
        _final_norm_kernel,
        grid=(b, l // tm),
        in_specs=[pl.BlockSpec((1, tm, d), lambda i, j: (i, j, 0)),
                  pl.BlockSpec((1, d), lambda i, j: (0, 0))],
        out_specs=pl.BlockSpec((1, tm, d), lambda i, j: (i, j, 0)),
        out_shape=jax.ShapeDtypeStruct(x.shape, F32),
        compiler_params=_cparams(("parallel", "parallel")),
        name="final_norm",
    )(x, nw.reshape(1, d))


def _pad_rows(a, rows):
    return jnp.zeros((rows,) + a.shape[1:], a.dtype).at[:a.shape[0]].set(a)


def kernel(x_prompt, x_sample, cache_k, cache_v, state_hgrn, c, c_ctx, norm1_w, norm2_w, w_ada, b_ada,
           w_in, rpb, hg_lower, hg_norm_w, w_out, ffn_w1, ffn_w3, ffn_w2, moe_router, moe_w1, moe_w3,
           moe_w2, final_norm_w):
    depth = w_in.shape[0]
    d = x_prompt.shape[-1]
    nb = x_prompt.shape[0]
    nd = x_sample.shape[0]
    rows = x_sample.shape[1] // GRID_W

    lbs = jnp.cumsum(jax.nn.softmax(hg_lower.astype(F32), axis=0), axis=0)
    lbs = lbs - lbs[0:1]
    lbc = _pad_rows(jnp.stack([lbs, jnp.log1p(-lbs), 1.0 - lbs], axis=1).transpose(1, 0, 2),
                    SUBLANES).transpose(1, 0, 2)

    cond = _pad_rows(jnp.concatenate([c, c_ctx[None, :]], axis=0), SUBLANES)
    mods = _modulation(cond, w_ada, b_ada)
    mods = _pad_rows(mods.reshape(depth, SUBLANES, 6, d).transpose(2, 0, 1, 3),
                     SUBLANES).transpose(1, 2, 0, 3)

    w_in_b = w_in.astype(BF16)
    w_out_b = w_out.astype(BF16)
    ffn_w1_b, ffn_w3_b, ffn_w2_b = ffn_w1.astype(BF16), ffn_w3.astype(BF16), ffn_w2.astype(BF16)
    moe_w1_b, moe_w3_b, moe_w2_b = moe_w1.astype(BF16), moe_w3.astype(BF16), moe_w2.astype(BF16)
    wkt_b = jnp.swapaxes(w_in[:, :, NA_W:2 * NA_W], 1, 2).astype(BF16)
    lc = cache_k.shape[2]
    ckt_b = jnp.swapaxes(cache_k.astype(BF16).reshape(nd, depth, lc, NA_W), 2, 3)
    cv_b = cache_v.astype(BF16).reshape(nd, depth, lc, NA_W)
    zero_state = jnp.zeros((nb, 2, HG_HEADS, HG_D, HG_D), F32)

    def layer(x, seq_shape, mod, l, attend, s0, wkt=None, caches=None):
        tok = lambda a: a.reshape(x.shape[:2] + a.shape[-1:])
        seq = lambda a: a.reshape(seq_shape + a.shape[-1:])
        outs = _inproj(x, norm1_w[l], mod, w_in_b[l], wkt, caches, l)
        att = attend(seq(outs[0]), outs[2])
        hg, s_fin = _hgrn(seq(outs[1]), lbc[l], hg_norm_w[l], s0)
        mixed = (x, tok(att), tok(hg), w_out_b[l], norm2_w[l], mod)
        i = l // 2
        if l % 2 == 0:
            x = _ffn(*mixed, ffn_w1_b[i], ffn_w3_b[i], ffn_w2_b[i])
        else:
            x = _moe(*mixed, moe_router[i], moe_w1_b[i], moe_w3_b[i], moe_w2_b[i],
                     final_norm_w if (l == depth - 1) else None)
        return x, outs[2:], s_fin

    seq = x_prompt.shape[1]
    xp, xs = x_prompt.reshape(1, nb * seq, d), x_sample
    caches = tuple(jnp.zeros((nb, depth, seq, NA_W), F32) for _ in range(2))
    ss_out = []
    for l in range(depth):
        mod_p = mods[l, nd:nd + 1]
        mod_s = mods[l, :nd]
        xp, caches, s_fin = layer(xp, (nb, seq), mod_p, l, lambda qkv, _: _ctx_attention(qkv),
                                  zero_state, caches=caches)
        ss_out.append(s_fin)
        bias = _na_bias_slabs(rpb[l], rows)
        attend_lat = lambda qkv, kt, l=l, bias=bias: _na_attention(qkv, kt, ckt_b[:, l], cv_b[:, l], bias)
        xs, _, _ = layer(xs, x_sample.shape[:2], mod_s, l, attend_lat,
                         state_hgrn[:, l].astype(F32), wkt=wkt_b[l])

    if depth % 2:
        xp, xs = _final_norm(xp, final_norm_w), _final_norm(xs, final_norm_w)
    y_prompt, y_sample = xp.reshape(x_prompt.shape), xs
    new_cache_k = caches[0].reshape(nb, depth, seq, NA_HEADS, NA_DH)
    new_cache_v = caches[1].reshape(nb, depth, seq, NA_HEADS, NA_DH)
    new_state = jnp.stack(ss_out, axis=1).astype(x_prompt.dtype)
    return (y_prompt, y_sample, new_cache_k, new_cache_v, new_state)
```

```python
import functools

import numpy as np
import jax
import jax.numpy as jnp
from jax import lax
from jax.experimental import pallas as pl
from jax.experimental.pallas import tpu as pltpu

F32 = jnp.float32
BF16 = jnp.bfloat16

EPS = 1e-6
NA_HEADS = 8
NA_DH = 64
NA_W = NA_HEADS * NA_DH
GRID_W = 64
WIN_R = 8
WIN_C = 16
HG_HEADS = 4
HG_D = 128
HG_W = HG_HEADS * HG_D
N_EXPERTS = 8
LANES = 128
SUBLANES = 8
VMEM_LIMIT = 56 * 1024 * 1024

HG_CHUNK = 128
HG_SUB = 16
HG_UNROLL = 4
HG_FLAT = 32
HG_FLAT_RANGE = 60.0


def _cparams(sem):
    return pltpu.CompilerParams(dimension_semantics=sem, vmem_limit_bytes=VMEM_LIMIT)


def _dot(a, b):
    return jnp.dot(a, b, preferred_element_type=F32)


def _dot_nt(a, b):
    return lax.dot_general(a, b, (((1,), (1,)), ((), ())), preferred_element_type=F32)


def _dot_tn(a, b):
    return lax.dot_general(a, b, (((0,), (0,)), ((), ())), preferred_element_type=F32)


def _norm_mod(x, nw, shift, scale):
    ms = jnp.mean(x * x, axis=-1, keepdims=True)
    y = x * lax.rsqrt(ms + EPS) * nw
    return y * (1.0 + scale) + shift


def _mod_kernel(cond_ref, w_ref, b_ref, o_ref):
    c = cond_ref[...]
    s = c * jax.nn.sigmoid(c)
    o_ref[0] = _dot(s.astype(BF16), w_ref[0].astype(BF16)) + b_ref[0]


def _modulation(cond, w_ada, b_ada, tn=1536):
    depth, d, n = w_ada.shape
    rows = cond.shape[0]
    return pl.pallas_call(
        _mod_kernel,
        grid=(depth, n // tn),
        in_specs=[
            pl.BlockSpec((rows, d), lambda l, j: (0, 0)),
            pl.BlockSpec((1, d, tn), lambda l, j: (l, 0, j)),
            pl.BlockSpec((1, 1, tn), lambda l, j: (l, 0, j)),
        ],
        out_specs=pl.BlockSpec((1, rows, tn), lambda l, j: (l, 0, j)),
        out_shape=jax.ShapeDtypeStruct((depth, rows, n), F32),
        compiler_params=_cparams(("parallel", "parallel")),
        name="modulation",
    )(cond, w_ada, b_ada.reshape(depth, 1, n))


def _mod_index(mod):
    if mod.shape[0] == 1:
        return lambda b, *_: (0, 0, 0)
    return lambda b, *_: (b, 0, 0)


def _inproj_kernel(x_ref, nw_ref, mod_ref, w_ref, *refs, want_kt):
    if want_kt:
        wkt_ref, qkv_ref, hg_ref, kt_ref = refs
    else:
        qkv_ref, hg_ref, kc_ref, vc_ref = refs[2:]
    h = _norm_mod(x_ref[0], nw_ref[...], mod_ref[0, 0:1, :], mod_ref[0, 1:2, :]).astype(BF16)
    p = _dot(h, w_ref[...])
    qkv_ref[0] = p[:, :3 * NA_W].astype(BF16)
    hg_ref[0] = p[:, 3 * NA_W:]
    if want_kt:
        kt_ref[0] = _dot_nt(wkt_ref[...], h).astype(BF16)
    else:
        kc_ref[...] = p[:, NA_W:2 * NA_W].reshape(kc_ref.shape)
        vc_ref[...] = p[:, 2 * NA_W:3 * NA_W].reshape(vc_ref.shape)


def _inproj(x, nw, mod, w, wkt=None, caches=None, layer=0, tm=512):
    b, l, d = x.shape
    n = w.shape[1]
    want_kt = wkt is not None
    out_shape = [jax.ShapeDtypeStruct((b, l, 3 * NA_W), BF16),
                 jax.ShapeDtypeStruct((b, l, n - 3 * NA_W), F32)]
    out_specs = [pl.BlockSpec((1, tm, 3 * NA_W), lambda i, j: (i, j, 0)),
                 pl.BlockSpec((1, tm, n - 3 * NA_W), lambda i, j: (i, j, 0))]
    in_specs = [pl.BlockSpec((1, tm, d), lambda i, j: (i, j, 0)),
                pl.BlockSpec((1, d), lambda i, j: (0, 0)),
                pl.BlockSpec((1, SUBLANES, d), _mod_index(mod)),
                pl.BlockSpec((d, n), lambda i, j: (0, 0))]
    args = [x, nw.reshape(1, d), mod, w]
    if want_kt:
        in_specs.append(pl.BlockSpec((NA_W, d), lambda i, j: (0, 0)))
        args.append(wkt)
        out_shape.append(jax.ShapeDtypeStruct((b, NA_W, l), BF16))
        out_specs.append(pl.BlockSpec((1, NA_W, tm), lambda i, j: (i, 0, j)))
        aliases = {}
    else:
        seq = caches[0].shape[2]
        per_tile = tm // seq
        assert b == 1 and per_tile * seq == tm
        for cache in caches:
            in_specs.append(pl.BlockSpec(memory_space=pl.ANY))
            args.append(cache)
            out_shape.append(jax.ShapeDtypeStruct(cache.shape, cache.dtype))
            out_specs.append(pl.BlockSpec((per_tile, 1, seq, NA_W), lambda i, j: (j, layer, 0, 0)))
        aliases = {4: 2, 5: 3}
    return pl.pallas_call(
        functools.partial(_inproj_kernel, want_kt=want_kt),
        grid=(b, l // tm),
        in_specs=in_specs,
        out_specs=out_specs,
        out_shape=out_shape,
        input_output_aliases=aliases,
        compiler_params=_cparams(("parallel", "parallel")),
        name="inproj",
    )(*args)


def _ctx_attn_kernel(q_ref, k_ref, v_ref, o_ref, s_ref):
    scale = NA_DH ** -0.5
    maxes = []
    for h in range(NA_HEADS):
        sl = slice(h * NA_DH, (h + 1) * NA_DH)
        q = q_ref[0, :, sl] * scale
        s = _dot_nt(q, k_ref[0, :, sl])
        s_ref[h] = s
        maxes.append(jnp.max(_lane_fold(jnp.maximum, s), axis=-1, keepdims=True))
    for h in range(NA_HEADS):
        sl = slice(h * NA_DH, (h + 1) * NA_DH)
        p = jnp.exp(s_ref[h] - maxes[h])
        den = jnp.sum(_lane_fold(jnp.add, p), axis=-1, keepdims=True)
        o = _dot(p.astype(BF16), v_ref[0, :, sl]) / den
        o_ref[0, :, sl] = o.astype(BF16)


def _ctx_attention(qkv):
    b, l, _ = qkv.shape
    spec = lambda c: pl.BlockSpec((1, l, NA_W), lambda i, c=c: (i, 0, c))
    return pl.pallas_call(
        _ctx_attn_kernel,
        grid=(b,),
        in_specs=[spec(0), spec(1), spec(2)],
        out_specs=pl.BlockSpec((1, l, NA_W), lambda i: (i, 0, 0)),
        out_shape=jax.ShapeDtypeStruct((b, l, NA_W), BF16),
        scratch_shapes=[pltpu.VMEM((NA_HEADS, l, l), F32)],
        compiler_params=_cparams(("parallel",)),
        name="ctx_attention",
    )(qkv, qkv, qkv)


NA_QROWS = 2
NA_KROWS = WIN_R + NA_QROWS


def _na_window_start(p, rows):
    return jnp.clip(NA_QROWS * p - WIN_R // 2, 0, rows - NA_KROWS)


def _lane_fold(op, *arrays):
    tiles = [a[:, i:i + LANES] for a in arrays for i in range(0, a.shape[1], LANES)]
    while len(tiles) > 1:
        tiles = [op(tiles[i], tiles[i + 1]) if i + 1 < len(tiles) else tiles[i]
                 for i in range(0, len(tiles), 2)]
    return tiles[0]


def _na_kernel(q_ref, kt_ref, v_ref, ckt_ref, cv_ref, bias_ref, o_ref, s_ref, *, rows):
    ws = _na_window_start(pl.program_id(1), rows)
    start = pl.multiple_of(ws * GRID_W, NA_QROWS * GRID_W)
    n_loc = NA_KROWS * GRID_W
    scale = NA_DH ** -0.5
    maxes = []
    for h in range(NA_HEADS):
        sl = slice(h * NA_DH, (h + 1) * NA_DH)
        q = q_ref[0, :, sl] * scale
        s_loc = _dot(q, kt_ref[0, sl, pl.ds(start, n_loc)]) + bias_ref[0, 0, h]
        s_ctx = _dot(q, ckt_ref[0, 0, sl, :])
        s_ref[h, :, :n_loc] = s_loc
        s_ref[h, :, n_loc:] = s_ctx
        maxes.append(jnp.max(_lane_fold(jnp.maximum, s_loc, s_ctx), axis=-1, keepdims=True))
    for h in range(NA_HEADS):
        sl = slice(h * NA_DH, (h + 1) * NA_DH)
        p_loc = jnp.exp(s_ref[h, :, :n_loc] - maxes[h])
        p_ctx = jnp.exp(s_ref[h, :, n_loc:] - maxes[h])
        den = jnp.sum(_lane_fold(jnp.add, p_loc, p_ctx), axis=-1, keepdims=True)
        o = (_dot(p_loc.astype(BF16), v_ref[0, pl.ds(start, n_loc), sl])
             + _dot(p_ctx.astype(BF16), cv_ref[0, 0, :, sl])) / den
        o_ref[0, :, sl] = o.astype(BF16)


NA_EDGE = 2


def _na_variant(p, n_pairs):
    return jnp.where(p < NA_EDGE, p, jnp.where(p >= n_pairs - NA_EDGE, p - (n_pairs - 2 * NA_EDGE - 1), NA_EDGE))


def _na_bias_slabs(rpb, rows):
    n_pairs = rows // NA_QROWS
    ps = np.array([0, 1, 2, n_pairs - 2, n_pairs - 1])
    r = NA_QROWS * ps[:, None] + np.arange(NA_QROWS)[None, :]
    rs = np.clip(r - WIN_R // 2, 0, rows - WIN_R)
    ws = np.clip(NA_QROWS * ps - WIN_R // 2, 0, rows - NA_KROWS)
    krow = ws[:, None] + np.arange(NA_KROWS)[None, :]
    row_ok = (krow[:, None, :] >= rs[:, :, None]) & (krow[:, None, :] < rs[:, :, None] + WIN_R)
    dr = np.clip(krow[:, None, :] - r[:, :, None] + (WIN_R - 1), 0, 2 * WIN_R - 2)
    j = np.arange(GRID_W)[:, None]
    kc = np.arange(GRID_W)[None, :]
    cs = np.clip(j - WIN_C // 2, 0, GRID_W - WIN_C)
    col_ok = (kc >= cs) & (kc < cs + WIN_C)
    dc = np.clip(kc - j + (WIN_C - 1), 0, 2 * WIN_C - 2)
    pick_r = jnp.asarray(dr[..., None] == np.arange(2 * WIN_R - 1), F32)
    pick_c = jnp.asarray(np.arange(2 * WIN_C - 1)[:, None, None] == dc[None], F32)
    exact = lax.Precision.HIGHEST
    planes = jnp.einsum('lhac,cjk->lhajk', rpb.astype(F32), pick_c, precision=exact)
    tab = jnp.einsum('vqia,lhajk->lvhqjik', pick_r, planes, precision=exact)
    ok = row_ok[:, None, :, None, :, None] & col_ok[None, None, None, :, None, :]
    tab = jnp.where(jnp.asarray(ok)[None], tab, -jnp.inf)
    return tab.reshape(rpb.shape[0], len(ps), NA_HEADS, NA_QROWS * GRID_W, NA_KROWS * GRID_W)


def _na_attention(qkv, kt, ckt, cv, bias, layer):
    b, l, _ = qkv.shape
    rows = l // GRID_W
    n_pairs = rows // NA_QROWS
    lc = cv.shape[2]
    tq = NA_QROWS * GRID_W
    n_loc = NA_KROWS * GRID_W
    return pl.pallas_call(
        functools.partial(_na_kernel, rows=rows),
        grid=(b, n_pairs),
        in_specs=[
            pl.BlockSpec((1, tq, NA_W), lambda i, p: (i, p, 0)),
            pl.BlockSpec((1, NA_W, l), lambda i, p: (i, 0, 0)),
            pl.BlockSpec((1, l, NA_W), lambda i, p: (i, 0, 2)),
            pl.BlockSpec((1, 1, NA_W, lc), lambda i, p: (i, layer, 0, 0)),
            pl.BlockSpec((1, 1, lc, NA_W), lambda i, p: (i, layer, 0, 0)),
            pl.BlockSpec((1, 1, NA_HEADS, tq, n_loc),
                         lambda i, p: (layer, _na_variant(p, n_pairs), 0, 0, 0)),
        ],
        out_specs=pl.BlockSpec((1, tq, NA_W), lambda i, p: (i, p, 0)),
        out_shape=jax.ShapeDtypeStruct((b, l, NA_W), BF16),
        scratch_shapes=[pltpu.VMEM((NA_HEADS, tq, n_loc + lc), F32)],
        compiler_params=_cparams(("parallel", "arbitrary")),
        name="na_attention",
    )(qkv, kt, qkv, ckt, cv, bias)


def _split2(x):
    hi = x.astype(BF16)
    lo = (x - hi.astype(F32)).astype(BF16)
    return hi, lo


def _block_ref(cum, block, ref_row):
    c = cum.shape[0]
    ref = cum.reshape(c // block, block, HG_D)[:, ref_row:ref_row + 1, :]
    return jnp.broadcast_to(ref, (c // block, block, HG_D)).reshape(c, HG_D)


def _hgrn_level(cum_f, cum_b, qs, k_f, k_b, m, ti, si):
    c = cum_f.shape[0]
    ref_f = _block_ref(cum_f, 2 * m, m - 1)
    ref_b = _block_ref(cum_b, 2 * m, m)
    row = lax.broadcasted_iota(jnp.int32, (c, 1), 0)
    upper = jnp.bitwise_and(row, m) != 0
    dec_f = jnp.exp(jnp.where(upper, cum_f - ref_f, ref_f - cum_f))
    dec_b = jnp.exp(jnp.where(upper, ref_b - cum_b, cum_b - ref_b))
    qm = jnp.concatenate([jnp.where(upper, qs * dec_f, 0.0), jnp.where(upper, 0.0, qs * dec_b)], axis=1)
    km = jnp.concatenate([jnp.where(upper, 0.0, k_f * dec_f), jnp.where(upper, k_b * dec_b, 0.0)], axis=1)
    am = _dot_nt(qm.astype(BF16), km.astype(BF16))
    if 2 * m < c:
        am = jnp.where(jnp.bitwise_xor(ti, si) < 2 * m, am, 0.0)
    return am


def _hgrn_pairwise(cum, qs, k, v, reverse):
    c = cum.shape[0]
    sub_row = lax.broadcasted_iota(jnp.int32, (SUBLANES, 1), 0)
    tiles_per_sub = HG_SUB // SUBLANES
    o_tiles = []
    for blk in range(c // HG_SUB):
        accs = [jnp.zeros((SUBLANES, HG_D), F32) for _ in range(tiles_per_sub)]
        for j in range(HG_SUB):
            s = blk * HG_SUB + j
            cs, ks, vs = cum[s:s + 1, :], k[s:s + 1, :], v[s:s + 1, :]
            for t in range(tiles_per_sub):
                lo_r, hi_r = t * SUBLANES, t * SUBLANES + SUBLANES - 1
                if (lo_r > j) if reverse else (hi_r < j):
                    continue
                base = blk * HG_SUB + lo_r
                x = qs[base:base + SUBLANES, :] * ks * jnp.exp(
                    jnp.minimum(cum[base:base + SUBLANES, :] - cs, 0.0))
                a = jnp.sum(x, axis=-1, keepdims=True)
                if reverse and hi_r > j:
                    a = jnp.where(sub_row + lo_r <= j, a, 0.0)
                elif (not reverse) and lo_r < j:
                    a = jnp.where(sub_row + lo_r >= j, a, 0.0)
                accs[t] = accs[t] + a * vs
        o_tiles.extend(accs)
    return jnp.concatenate(o_tiles, axis=0)


def _hgrn_gates(z, consts):
    lb, log1mlb, oml = consts
    e = jnp.exp(-jnp.abs(z))
    inv = 1.0 / (1.0 + e)
    pos = z >= 0.0
    f = lb + oml * (jnp.where(pos, 1.0, e) * inv)
    logf = jnp.where(f > 0.0, jnp.log(f), log1mlb + z)
    k = oml * (jnp.where(pos, e, 1.0) * inv)
    return logf, k


def _flat_offsets(cum_f, cum_b):
    return (cum_f - _block_ref(cum_f, HG_FLAT, HG_FLAT // 2 - 1),
            cum_b - _block_ref(cum_b, HG_FLAT, HG_FLAT // 2))


def _hgrn_kernel(q_ref, i_ref, zf_ref, zb_ref, g_ref, lbc_ref, nw_ref, s0_ref,
                 o_ref, s_ref, cum_ref, k_ref, qs_ref, acc_ref, qt_ref, u_ref, de_ref, *, nh):
    l = q_ref.shape[1]
    c = HG_CHUNK
    nc = l // c
    nu = nh * nc
    ti = lax.broadcasted_iota(jnp.int32, (c, c), 0)
    si = lax.broadcasted_iota(jnp.int32, (c, c), 1)

    def unit(u):
        if nh == 1:
            return pl.ds(pl.multiple_of(u * c, c), c), slice(None), slice(None)
        h = u // nc
        rows = pl.ds(pl.multiple_of((u - h * nc) * c, c), c)
        return (rows, pl.ds(pl.multiple_of(h * HG_D, HG_D), HG_D),
                pl.ds(pl.multiple_of(h * 2 * HG_D, 2 * HG_D), 2 * HG_D))

    def prepare(u, gmax):
        rows, lanes, _ = unit(u)
        consts = tuple(lbc_ref[r:r + 1, lanes] for r in range(3))
        q = q_ref[0, rows, lanes]
        qs_ref[rows, lanes] = q * jax.nn.sigmoid(q)
        cums = []
        for d, z_ref in enumerate((zf_ref, zb_ref)):
            logf, k = _hgrn_gates(z_ref[0, rows, lanes], consts)
            k_ref[d, rows, lanes] = k
            tri = jnp.where((si >= ti) if d else (si <= ti), 1.0, 0.0).astype(BF16)
            parts = _dot(tri, jnp.concatenate(_split2(logf), axis=1))
            cum = parts[:, :HG_D] + parts[:, HG_D:]
            cum_ref[d, rows, lanes] = cum
            cums.append(cum)
        dq_f, dq_b = _flat_offsets(*cums)
        g = jnp.maximum(jnp.abs(dq_f), jnp.abs(dq_b))
        return jnp.maximum(gmax, jnp.max(g.reshape(c // SUBLANES, SUBLANES, HG_D), axis=0))

    gmax = lax.fori_loop(0, nu, prepare, jnp.zeros((SUBLANES, HG_D), F32), unroll=HG_UNROLL)
    flat_ok = jnp.max(gmax) <= HG_FLAT_RANGE

    def intra(u, flat):
        rows, lanes, lanes2 = unit(u)
        cum_f, cum_b = cum_ref[0, rows, lanes], cum_ref[1, rows, lanes]
        k_f, k_b = k_ref[0, rows, lanes], k_ref[1, rows, lanes]
        qs = qs_ref[rows, lanes]
        v = i_ref[0, rows, lanes]
        vb = v.astype(BF16)
        a = jnp.zeros((c, c), F32)
        m = c // 2
        while m >= (HG_FLAT if flat else HG_SUB):
            a = a + _hgrn_level(cum_f, cum_b, qs, k_f, k_b, m, ti, si)
            m //= 2
        if flat:
            dq_f, dq_b = _flat_offsets(cum_f, cum_b)
            same = jnp.bitwise_xor(ti, si) < HG_FLAT
            a_f = _dot_nt((qs * jnp.exp(dq_f)).astype(BF16), (k_f * jnp.exp(-dq_f)).astype(BF16))
            a_b = _dot_nt((qs * jnp.exp(dq_b)).astype(BF16), (k_b * jnp.exp(-dq_b)).astype(BF16))
            a = a + jnp.where(same & (si <= ti), a_f, 0.0) + jnp.where(same & (si >= ti), a_b, 0.0)
            o = _dot(a.astype(BF16), vb)
        else:
            o = (_dot(a.astype(BF16), vb) + _hgrn_pairwise(cum_f, qs, k_f, v, False)
                 + _hgrn_pairwise(cum_b, qs, k_b, v, True))
        acc_ref[rows, lanes] = o
        edge_f, edge_b = cum_f[c - 1:c, :], cum_b[0:1, :]
        qt_ref[rows, lanes2] = jnp.concatenate([(qs * jnp.exp(cum_f)).astype(BF16),
                                                (qs * jnp.exp(cum_b)).astype(BF16)], axis=1)
        kd = jnp.concatenate([(k_f * jnp.exp(edge_f - cum_f)).astype(BF16),
                              (k_b * jnp.exp(edge_b - cum_b)).astype(BF16)], axis=1)
        u_ref[u] = _dot_tn(vb, kd)
        de_ref[u] = jnp.broadcast_to(jnp.concatenate([jnp.exp(edge_f), jnp.exp(edge_b)], axis=1),
                                     (SUBLANES, 2 * HG_D))

    def run_intra(flat):
        def body(u, carry):
            intra(u, flat)
            return carry
        lax.fori_loop(0, nu, body, 0, unroll=HG_UNROLL if flat else 1)

    pl.when(flat_ok)(lambda: run_intra(True))
    pl.when(jnp.logical_not(flat_ok))(lambda: run_intra(False))

    for h in range(nh):
        def sweep(i, carry, h=h):
            st_f, st_b = carry
            uf, ub = h * nc + i, h * nc + nc - 1 - i
            inc = u_ref[uf, :, :HG_D]
            u_ref[uf, :, :HG_D] = st_f
            st_f = st_f * de_ref[uf, 0:1, :HG_D] + inc
            inc = u_ref[ub, :, HG_D:]
            u_ref[ub, :, HG_D:] = st_b
            st_b = st_b * de_ref[ub, 0:1, HG_D:] + inc
            return st_f, st_b

        st_f, st_b = lax.fori_loop(0, nc, sweep, (s0_ref[0, 0, h].T, s0_ref[0, 1, h].T))
        s_ref[0, 0, h] = st_f.T
        s_ref[0, 1, h] = st_b.T

    def finish(u, carry):
        rows, lanes, lanes2 = unit(u)
        o = acc_ref[rows, lanes] + _dot_nt(qt_ref[rows, lanes2], u_ref[u].astype(BF16))
        o = o * lax.rsqrt(jnp.mean(o * o, axis=-1, keepdims=True) + EPS) * nw_ref[...]
        g = g_ref[0, rows, lanes]
        o_ref[0, rows, lanes] = (o * (g * jax.nn.sigmoid(g))).astype(BF16)
        return carry

    lax.fori_loop(0, nu, finish, 0, unroll=HG_UNROLL)


HG_VMEM_BUDGET = 44 * 1024 * 1024


def _hgrn_heads_per_step(l):
    per_head = l * HG_D * (5 * 2 * 4 + 6 * 4 + 2 * 2 + 8)
    nh = HG_HEADS
    while nh > 1 and nh * per_head > HG_VMEM_BUDGET:
        nh //= 2
    return nh


def _hgrn(hgp, lbc, norm_w, s0):
    b, l, _ = hgp.shape
    nh = _hgrn_heads_per_step(l)
    groups = HG_HEADS // nh
    w = nh * HG_D
    nu = nh * (l // HG_CHUNK)
    col = lambda sec: pl.BlockSpec((1, l, w), lambda i, h, sec=sec: (i, 0, sec * groups + h))
    state_spec = pl.BlockSpec((1, 2, nh, HG_D, HG_D), lambda i, h: (i, 0, h, 0, 0))
    return pl.pallas_call(
        functools.partial(_hgrn_kernel, nh=nh),
        grid=(b, groups),
        in_specs=[col(0), col(1), col(2), col(3), col(4),
                  pl.BlockSpec((SUBLANES, w), lambda i, h: (0, h)),
                  pl.BlockSpec((1, HG_D), lambda i, h: (0, 0)),
                  state_spec],
        out_specs=[pl.BlockSpec((1, l, w), lambda i, h: (i, 0, h)), state_spec],
        out_shape=[jax.ShapeDtypeStruct((b, l, HG_W), BF16),
                   jax.ShapeDtypeStruct((b, 2, HG_HEADS, HG_D, HG_D), F32)],
        scratch_shapes=[pltpu.VMEM((2, l, w), F32),
                        pltpu.VMEM((2, l, w), F32),
                        pltpu.VMEM((l, w), F32),
                        pltpu.VMEM((l, w), F32),
                        pltpu.VMEM((l, 2 * w), BF16),
                        pltpu.VMEM((nu, HG_D, 2 * HG_D), F32),
                        pltpu.VMEM((nu, SUBLANES, 2 * HG_D), F32)],
        compiler_params=_cparams(("parallel", "parallel")),
        name="hgrn",
    )(hgp, hgp, hgp, hgp, hgp, lbc, norm_w.reshape(1, HG_D), s0)


def _mixer_residual(x_ref, att_ref, hg_ref, wo_ref, mod_ref):
    y = _dot(att_ref[0], wo_ref[:NA_W, :]) + _dot(hg_ref[0], wo_ref[NA_W:, :])
    return x_ref[0] + mod_ref[0, 2:3, :] * y


def _mixer_specs(tm, d, mod):
    tile = lambda w: pl.BlockSpec((1, tm, w), lambda i, j, *_: (i, j, 0))
    const = lambda shape: pl.BlockSpec(shape, lambda *_: (0,) * len(shape))
    return [tile(d), tile(NA_W), tile(HG_W), const((NA_W + HG_W, d)), const((1, d)),
            pl.BlockSpec((1, SUBLANES, d), _mod_index(mod))]


def _ffn_kernel(x_ref, att_ref, hg_ref, wo_ref, nw_ref, mod_ref, w1_ref, w3_ref, w2_ref, o_ref,
                h_ref, acc_ref, xn_ref):
    f = pl.program_id(2)

    @pl.when(f == 0)
    def _():
        xn = _mixer_residual(x_ref, att_ref, hg_ref, wo_ref, mod_ref)
        xn_ref[...] = xn
        h = _norm_mod(xn, nw_ref[...], mod_ref[0, 3:4, :], mod_ref[0, 4:5, :])
        h_ref[...] = h.astype(BF16)
        acc_ref[...] = jnp.zeros_like(acc_ref)

    h = h_ref[...]
    a = _dot(h, w1_ref[...])
    g = (a * jax.nn.sigmoid(a)) * _dot(h, w3_ref[...])
    acc_ref[...] += _dot(g.astype(BF16), w2_ref[...])

    @pl.when(f == pl.num_programs(2) - 1)
    def _():
        o_ref[0] = xn_ref[...] + mod_ref[0, 5:6, :] * acc_ref[...]


def _ffn(x, att, hg, w_out, nw, mod, w1, w3, w2, tm=512, tf=1408):
    b, l, d = x.shape
    tm = min(tm, l)
    ff = w1.shape[1]
    tf = min(tf, ff)
    return pl.pallas_call(
        _ffn_kernel,
        grid=(b, l // tm, ff // tf),
        in_specs=_mixer_specs(tm, d, mod) + [
            pl.BlockSpec((d, tf), lambda i, j, f: (0, f)),
            pl.BlockSpec((d, tf), lambda i, j, f: (0, f)),
            pl.BlockSpec((tf, d), lambda i, j, f: (f, 0)),
        ],
        out_specs=pl.BlockSpec((1, tm, d), lambda i, j, f: (i, j, 0)),
        out_shape=jax.ShapeDtypeStruct(x.shape, F32),
        scratch_shapes=[pltpu.VMEM((tm, d), BF16), pltpu.VMEM((tm, d), F32), pltpu.VMEM((tm, d), F32)],
        compiler_params=_cparams(("parallel", "parallel", "arbitrary")),
        name="ffn",
    )(x, att, hg, w_out, nw.reshape(1, d), mod, w1, w3, w2)


def _router_kernel(x_ref, att_ref, hg_ref, wo_ref, nw_ref, mod_ref, r_ref, xn_ref, hp_ref, idx_ref, wt_ref):
    xn = _mixer_residual(x_ref, att_ref, hg_ref, wo_ref, mod_ref)
    xn_ref[0] = xn
    h = _norm_mod(xn, nw_ref[...], mod_ref[0, 3:4, :], mod_ref[0, 4:5, :])
    hh = h.astype(BF16)
    hl = (h - hh.astype(F32)).astype(BF16)
    r = r_ref[...]
    rh = r.astype(BF16)
    rl = (r - rh.astype(F32)).astype(BF16)
    logits = _dot(hh, rh) + _dot(hh, rl) + _dot(hl, rh)
    lane = lax.broadcasted_iota(jnp.int32, logits.shape, 1)
    logits = jnp.where(lane < N_EXPERTS, logits, -jnp.inf)
    m1 = jnp.max(logits, axis=-1, keepdims=True)
    i1 = jnp.min(jnp.where(logits == m1, lane, LANES), axis=-1, keepdims=True)
    rest = jnp.where(lane == i1, -jnp.inf, logits)
    m2 = jnp.max(rest, axis=-1, keepdims=True)
    i2 = jnp.min(jnp.where(rest == m2, lane, LANES), axis=-1, keepdims=True)
    e2 = jnp.exp(m2 - m1)
    den = 1.0 + e2
    idx_ref[0] = jnp.where(lane == 0, i1, jnp.where(lane == 1, i2, 0))
    wt_ref[0] = jnp.where(lane == 0, 1.0 / den, jnp.where(lane == 1, e2 / den, 0.0))
    hp_ref[0] = h


def _router(x, att, hg, w_out, nw, mod, router, tm=512):
    b, l, d = x.shape
    tm = min(tm, l)
    rp = jnp.zeros((d, LANES), F32).at[:, :router.shape[1]].set(router)
    tile = lambda w: pl.BlockSpec((1, tm, w), lambda i, j: (i, j, 0))
    return pl.pallas_call(
        _router_kernel,
        grid=(b, l // tm),
        in_specs=_mixer_specs(tm, d, mod) + [pl.BlockSpec((d, LANES), lambda i, j: (0, 0))],
        out_specs=[tile(d), tile(d), tile(LANES), tile(LANES)],
        out_shape=[jax.ShapeDtypeStruct((b, l, d), F32),
                   jax.ShapeDtypeStruct((b, l, d), F32),
                   jax.ShapeDtypeStruct((b, l, LANES), jnp.int32),
                   jax.ShapeDtypeStruct((b, l, LANES), F32)],
        compiler_params=_cparams(("parallel", "parallel")),
        name="router",
    )(x, att, hg, w_out, nw.reshape(1, d), mod, rp)


MOE_TM = 512
MOE_TG = 1024
TOP_K = 2


def _route_plan(idx, n_experts, tm):
    t = idx.shape[0]
    chosen = idx[:, :, None] == jnp.arange(n_experts, dtype=jnp.int32)[None, None, :]
    sel = jnp.any(chosen, axis=1)
    rank = jnp.cumsum(sel.astype(jnp.int32), axis=0) - 1
    count = rank[-1] + 1
    tiles = (count + tm - 1) // tm
    tile_end = jnp.cumsum(tiles)
    start = (tile_end - tiles) * tm
    dest = jnp.sum(jnp.where(chosen, (start[None, :] + rank)[:, None, :], 0), axis=-1)
    n_tiles_max = (t * TOP_K) // tm + n_experts + 1
    used = jnp.minimum(jnp.arange(n_tiles_max, dtype=jnp.int32), tile_end[-1] - 1)
    tile_expert = jnp.sum((used[:, None] >= tile_end[None, :]).astype(jnp.int32), axis=1)
    spare = jnp.arange(n_experts + 1, dtype=jnp.int32) * tm + tile_end[-1] * tm
    fills = jnp.concatenate([(start + count) // SUBLANES * SUBLANES,
                             jnp.minimum(spare, (n_tiles_max - 1) * tm - SUBLANES)])
    return (dest.astype(jnp.int32), tile_expert.astype(jnp.int32), tile_end[-1:].astype(jnp.int32),
            fills.astype(jnp.int32))


def _row_copies(dest_ref, src_of, dst_of, sem, n):
    def each(group, fn):
        base = pl.multiple_of(group * SUBLANES, SUBLANES)
        for j in range(SUBLANES):
            for s in range(TOP_K):
                r = base + j
                row = dest_ref[0, s, r]
                fn(pltpu.make_async_copy(src_of(s, r, row), dst_of(s, r, row), sem))

    def start(group, carry):
        each(group, lambda copy: copy.start())
        return carry

    def wait(group, carry):
        each(group, lambda copy: copy.wait())
        return carry

    lax.fori_loop(0, n // SUBLANES, start, 0)
    lax.fori_loop(0, n // SUBLANES, wait, 0)


def _dispatch_kernel(pad_ref, dest_ref, hp_ref, xs_ref, zero_ref, sem):
    @pl.when(pl.program_id(0) == 0)
    def _():
        zero_ref[...] = jnp.zeros_like(zero_ref)
        for e in range(pad_ref.shape[0]):
            first = pl.multiple_of(pad_ref[e], SUBLANES)
            copy = pltpu.make_async_copy(zero_ref, xs_ref.at[pl.ds(first, zero_ref.shape[0])], sem)
            copy.start()
            copy.wait()

    _row_copies(dest_ref,
                lambda s, r, row: hp_ref.at[pl.ds(r, 1)],
                lambda s, r, row: xs_ref.at[pl.ds(row, 1)],
                sem, hp_ref.shape[0])


def _dispatch(hp, dest_tiles, pad_start, n_rows):
    t, w = hp.shape
    return pl.pallas_call(
        _dispatch_kernel,
        grid_spec=pltpu.PrefetchScalarGridSpec(
            num_scalar_prefetch=1,
            grid=(t // MOE_TG,),
            in_specs=[
                pl.BlockSpec((1, TOP_K, MOE_TG), lambda i, pad: (i, 0, 0), memory_space=pltpu.SMEM),
                pl.BlockSpec((MOE_TG, w), lambda i, pad: (i, 0)),
            ],
            out_specs=pl.BlockSpec(memory_space=pl.ANY),
            scratch_shapes=[pltpu.VMEM((MOE_TM + SUBLANES, w), hp.dtype), pltpu.SemaphoreType.DMA],
        ),
        out_shape=jax.ShapeDtypeStruct((n_rows, w), hp.dtype),
        compiler_params=_cparams(("arbitrary",)),
        name="moe_dispatch",
    )(pad_start, dest_tiles, hp)


def _expert_kernel(te_ref, nt_ref, xs_ref, w1_ref, w3_ref, w2_ref, y_ref):
    del te_ref
    i = pl.program_id(0)

    @pl.when(i < nt_ref[0])
    def _():
        h = xs_ref[...].astype(BF16)
        a = _dot(h, w1_ref[0])
        g = (a * jax.nn.sigmoid(a)) * _dot(h, w3_ref[0])
        y_ref[...] = _dot(g.astype(BF16), w2_ref[0])

    @pl.when(i >= nt_ref[0])
    def _():
        y_ref[...] = jnp.zeros_like(y_ref)


def _experts(xs, tile_expert, n_tiles, w1, w3, w2):
    n_rows, w = xs.shape
    _, d, ff = w1.shape
    return pl.pallas_call(
        _expert_kernel,
        grid_spec=pltpu.PrefetchScalarGridSpec(
            num_scalar_prefetch=2,
            grid=(n_rows // MOE_TM,),
            in_specs=[
                pl.BlockSpec((MOE_TM, w), lambda i, te, nt: (i, 0)),
                pl.BlockSpec((1, d, ff), lambda i, te, nt: (te[i], 0, 0)),
                pl.BlockSpec((1, d, ff), lambda i, te, nt: (te[i], 0, 0)),
                pl.BlockSpec((1, ff, d), lambda i, te, nt: (te[i], 0, 0)),
            ],
            out_specs=pl.BlockSpec((MOE_TM, d), lambda i, te, nt: (i, 0)),
        ),
        out_shape=jax.ShapeDtypeStruct((n_rows, d), F32),
        compiler_params=_cparams(("arbitrary",)),
        name="moe_experts",
    )(tile_expert, n_tiles, xs, w1, w3, w2)


def _combine_kernel(dest_ref, x_ref, mod_ref, wt_ref, y_ref, *refs):
    o_ref, buf_ref, sem = refs[-3:]
    _row_copies(dest_ref,
                lambda s, r, row: y_ref.at[pl.ds(row, 1)],
                lambda s, r, row: buf_ref.at[s, pl.ds(r, 1)],
                sem, x_ref.shape[1])
    wt = wt_ref[0]
    mixed = wt[:, 0:1] * buf_ref[0] + wt[:, 1:2] * buf_ref[1]
    o = x_ref[0] + mod_ref[0, 5:6, :] * mixed
    if len(refs) > 3:
        o = o * lax.rsqrt(jnp.mean(o * o, axis=-1, keepdims=True) + EPS) * refs[0][...]
    o_ref[0] = o


def _combine(x, mod, wts, dest_tiles, y, final_nw=None):
    b, l, d = x.shape
    per_b = l // MOE_TG
    in_specs = [
        pl.BlockSpec((1, TOP_K, MOE_TG), lambda i, j: (i * per_b + j, 0, 0), memory_space=pltpu.SMEM),
        pl.BlockSpec((1, MOE_TG, d), lambda i, j: (i, j, 0)),
        pl.BlockSpec((1, SUBLANES, d), _mod_index(mod)),
        pl.BlockSpec((1, MOE_TG, LANES), lambda i, j: (i, j, 0)),
        pl.BlockSpec(memory_space=pl.ANY),
    ]
    args = [dest_tiles, x, mod, wts, y]
    if final_nw is not None:
        in_specs.append(pl.BlockSpec((1, d), lambda i, j: (0, 0)))
        args.append(final_nw.reshape(1, d))
    return pl.pallas_call(
        _combine_kernel,
        grid=(b, per_b),
        in_specs=in_specs,
        out_specs=pl.BlockSpec((1, MOE_TG, d), lambda i, j: (i, j, 0)),
        out_shape=jax.ShapeDtypeStruct(x.shape, F32),
        scratch_shapes=[pltpu.VMEM((TOP_K, MOE_TG, d), F32), pltpu.SemaphoreType.DMA],
        compiler_params=_cparams(("arbitrary", "arbitrary")),
        name="moe_combine",
    )(*args)


def _moe(x, att, hg, w_out, nw, mod, router, w1, w3, w2, final_nw=None):
    b, l, d = x.shape
    t = b * l
    x, hp, idx, wts = _router(x, att, hg, w_out, nw, mod, router)
    dest, tile_expert, n_tiles, pad_start = _route_plan(idx.reshape(t, LANES)[:, :TOP_K], w1.shape[0], MOE_TM)
    dest_tiles = dest.reshape(t // MOE_TG, MOE_TG, TOP_K).transpose(0, 2, 1)
    xs = _dispatch(hp.reshape(t, d), dest_tiles, pad_start, tile_expert.shape[0] * MOE_TM)
    y = _experts(xs, tile_expert, n_tiles, w1, w3, w2)
    return _combine(x, mod, wts, dest_tiles, y, final_nw)


def _final_norm_kernel(x_ref, nw_ref, o_ref):
    x = x_ref[0]
    o_ref[0] = x * lax.rsqrt(jnp.mean(x * x, axis=-1, keepdims=True) + EPS) * nw_ref[...]


def _final_norm(x, nw, tm=512):
    b, l, d = x.shape
    tm = min(tm, l)
    return pl.pallas_call(
        _final_norm_kernel,
        grid=(b, l // tm),
        in_specs=[pl.BlockSpec((1, tm, d), lambda i, j: (i, j, 0)),
                  pl.BlockSpec((1, d), lambda i, j: (0, 0))],
        out_specs=pl.BlockSpec((1, tm, d), lambda i, j: (i, j, 0)),
        out_shape=jax.ShapeDtypeStruct(x.shape, F32),
        compiler_params=_cparams(("parallel", "parallel")),
        name="final_norm",
    )(x, nw.reshape(1, d))


def _pad_rows(a, rows):
    return jnp.zeros((rows,) + a.shape[1:], a.dtype).at[:a.shape[0]].set(a)


def kernel(x_prompt, x_sample, cache_k, cache_v, state_hgrn, c, c_ctx, norm1_w, norm2_w, w_ada, b_ada,
           w_in, rpb, hg_lower, hg_norm_w, w_out, ffn_w1, ffn_w3, ffn_w2, moe_router, moe_w1, moe_w3,
           moe_w2, final_norm_w):
    depth = w_in.shape[0]
    d = x_prompt.shape[-1]
    nb = x_prompt.shape[0]
    nd = x_sample.shape[0]
    rows = x_sample.shape[1] // GRID_W

    lbs = jnp.cumsum(jax.nn.softmax(hg_lower.astype(F32), axis=0), axis=0)
    lbs = lbs - lbs[0:1]
    lbc = _pad_rows(jnp.stack([lbs, jnp.log1p(-lbs), 1.0 - lbs], axis=1).transpose(1, 0, 2),
                    SUBLANES).transpose(1, 0, 2)

    cond = _pad_rows(jnp.concatenate([c, c_ctx[None, :]], axis=0), SUBLANES)
    mods = _modulation(cond, w_ada, b_ada)
    mods = _pad_rows(mods.reshape(depth, SUBLANES, 6, d).transpose(2, 0, 1, 3),
                     SUBLANES).transpose(1, 2, 0, 3)

    def layer_weights(l):
        cast = lambda a: a.astype(BF16)
        i = l // 2
        mixer = ((ffn_w1[i], ffn_w3[i], ffn_w2[i]) if l % 2 == 0 else (moe_w1[i], moe_w3[i], moe_w2[i]))
        return dict(w_in=cast(w_in[l]), wkt=cast(w_in[l, :, NA_W:2 * NA_W].T), w_out=cast(w_out[l]),
                    mixer=tuple(cast(a) for a in mixer))

    lc = cache_k.shape[2]
    ckt_b = jnp.swapaxes(cache_k.astype(BF16).reshape(nd, depth, lc, NA_W), 2, 3)
    cv_b = cache_v.astype(BF16).reshape(nd, depth, lc, NA_W)
    zero_state = jnp.zeros((nb, 2, HG_HEADS, HG_D, HG_D), F32)

    def layer(x, seq_shape, mod, l, wts, attend, s0, caches=None):
        tok = lambda a: a.reshape(x.shape[:2] + a.shape[-1:])
        seq = lambda a: a.reshape(seq_shape + a.shape[-1:])
        outs = _inproj(x, norm1_w[l], mod, wts["w_in"], None if caches else wts["wkt"], caches, l)
        att = attend(seq(outs[0]), outs[2])
        hg, s_fin = _hgrn(seq(outs[1]), lbc[l], hg_norm_w[l], s0)
        mixed = (x, tok(att), tok(hg), wts["w_out"], norm2_w[l], mod)
        if l % 2 == 0:
            x = _ffn(*mixed, *wts["mixer"])
        else:
            x = _moe(*mixed, moe_router[l // 2], *wts["mixer"],
                     final_norm_w if (l == depth - 1) else None)
        return x, outs[2:], s_fin

    seq = x_prompt.shape[1]
    xp, xs = x_prompt.reshape(1, nb * seq, d), x_sample
    caches = tuple(jnp.zeros((nb, depth, seq, NA_W), F32) for _ in range(2))
    ss_out = []
    bias = _na_bias_slabs(rpb, rows)
    for l in range(depth):
        mod_p = mods[l, nd:nd + 1]
        mod_s = mods[l, :nd]
        wts = layer_weights(l)
        xp, caches, s_fin = layer(xp, (nb, seq), mod_p, l, wts, lambda qkv, _: _ctx_attention(qkv),
                                  zero_state, caches=caches)
        ss_out.append(s_fin)
        attend_lat = lambda qkv, kt, l=l: _na_attention(qkv, kt, ckt_b, cv_b, bias, l)
        xs, _, _ = layer(xs, x_sample.shape[:2], mod_s, l, wts, attend_lat,
                         state_hgrn[:, l].astype(F32))

    if depth % 2:
        xp, xs = _final_norm(xp, final_norm_w), _final_norm(xs, final_norm_w)
    y_prompt, y_sample = xp.reshape(x_prompt.shape), xs
    new_cache_k = caches[0].reshape(nb, depth, seq, NA_HEADS, NA_DH)
    new_cache_v = caches[1].reshape(nb, depth, seq, NA_HEADS, NA_DH)
    new_state = jnp.stack(ss_out, axis=1).astype(x_prompt.dtype)
    return (y_prompt, y_sample, new_cache_k, new_cache_v, new_state)
```

```python
import functools

import numpy as np
import jax
import jax.numpy as jnp
from jax import lax
from jax.experimental import pallas as pl
from jax.experimental.pallas import tpu as pltpu

F32 = jnp.float32
BF16 = jnp.bfloat16

EPS = 1e-6
NA_HEADS = 8
NA_DH = 64
NA_W = NA_HEADS * NA_DH
GRID_W = 64
WIN_R = 8
WIN_C = 16
HG_HEADS = 4
HG_D = 128
HG_W = HG_HEADS * HG_D
N_EXPERTS = 8
LANES = 128
SUBLANES = 8
VMEM_LIMIT = 56 * 1024 * 1024

HG_CHUNK = 128
HG_SUB = 16
HG_UNROLL = 8
HG_FLAT = 32
HG_FLAT_RANGE = 60.0


def _cparams(sem):
    return pltpu.CompilerParams(dimension_semantics=sem, vmem_limit_bytes=VMEM_LIMIT)


def _dot(a, b):
    return jnp.dot(a, b, preferred_element_type=F32)


def _dot_nt(a, b):
    return lax.dot_general(a, b, (((1,), (1,)), ((), ())), preferred_element_type=F32)


def _dot_tn(a, b):
    return lax.dot_general(a, b, (((0,), (0,)), ((), ())), preferred_element_type=F32)


def _norm_mod(x, nw, shift, scale):
    ms = jnp.mean(x * x, axis=-1, keepdims=True)
    y = x * lax.rsqrt(ms + EPS) * nw
    return y * (1.0 + scale) + shift


def _mod_kernel(cond_ref, w_ref, b_ref, o_ref):
    c = cond_ref[...]
    s = c * jax.nn.sigmoid(c)
    o_ref[0] = _dot(s.astype(BF16), w_ref[0].astype(BF16)) + b_ref[0]


def _modulation(cond, w_ada, b_ada, tn=1536):
    depth, d, n = w_ada.shape
    rows = cond.shape[0]
    return pl.pallas_call(
        _mod_kernel,
        grid=(depth, n // tn),
        in_specs=[
            pl.BlockSpec((rows, d), lambda l, j: (0, 0)),
            pl.BlockSpec((1, d, tn), lambda l, j: (l, 0, j)),
            pl.BlockSpec((1, 1, tn), lambda l, j: (l, 0, j)),
        ],
        out_specs=pl.BlockSpec((1, rows, tn), lambda l, j: (l, 0, j)),
        out_shape=jax.ShapeDtypeStruct((depth, rows, n), F32),
        compiler_params=_cparams(("parallel", "parallel")),
        name="modulation",
    )(cond, w_ada, b_ada.reshape(depth, 1, n))


def _mod_index(mod):
    if mod.shape[0] == 1:
        return lambda b, *_: (0, 0, 0)
    return lambda b, *_: (b, 0, 0)


def _inproj_kernel(x_ref, nw_ref, mod_ref, w_ref, *refs, want_kt):
    if want_kt:
        wkt_ref, qkv_ref, hg_ref, kt_ref = refs
    else:
        qkv_ref, hg_ref, kc_ref, vc_ref = refs[2:]
    h = _norm_mod(x_ref[0], nw_ref[...], mod_ref[0, 0:1, :], mod_ref[0, 1:2, :]).astype(BF16)
    p = _dot(h, w_ref[...])
    qkv_ref[0] = p[:, :3 * NA_W].astype(BF16)
    hg_ref[0] = p[:, 3 * NA_W:]
    if want_kt:
        kt_ref[0] = _dot_nt(wkt_ref[...], h).astype(BF16)
    else:
        kc_ref[...] = p[:, NA_W:2 * NA_W].reshape(kc_ref.shape)
        vc_ref[...] = p[:, 2 * NA_W:3 * NA_W].reshape(vc_ref.shape)


def _inproj(x, nw, mod, w, wkt=None, caches=None, layer=0, tm=512):
    b, l, d = x.shape
    n = w.shape[1]
    want_kt = wkt is not None
    out_shape = [jax.ShapeDtypeStruct((b, l, 3 * NA_W), BF16),
                 jax.ShapeDtypeStruct((b, l, n - 3 * NA_W), F32)]
    out_specs = [pl.BlockSpec((1, tm, 3 * NA_W), lambda i, j: (i, j, 0)),
                 pl.BlockSpec((1, tm, n - 3 * NA_W), lambda i, j: (i, j, 0))]
    in_specs = [pl.BlockSpec((1, tm, d), lambda i, j: (i, j, 0)),
                pl.BlockSpec((1, d), lambda i, j: (0, 0)),
                pl.BlockSpec((1, SUBLANES, d), _mod_index(mod)),
                pl.BlockSpec((d, n), lambda i, j: (0, 0))]
    args = [x, nw.reshape(1, d), mod, w]
    if want_kt:
        in_specs.append(pl.BlockSpec((NA_W, d), lambda i, j: (0, 0)))
        args.append(wkt)
        out_shape.append(jax.ShapeDtypeStruct((b, NA_W, l), BF16))
        out_specs.append(pl.BlockSpec((1, NA_W, tm), lambda i, j: (i, 0, j)))
        aliases = {}
    else:
        seq = caches[0].shape[2]
        per_tile = tm // seq
        assert b == 1 and per_tile * seq == tm
        for cache in caches:
            in_specs.append(pl.BlockSpec(memory_space=pl.ANY))
            args.append(cache)
            out_shape.append(jax.ShapeDtypeStruct(cache.shape, cache.dtype))
            out_specs.append(pl.BlockSpec((per_tile, 1, seq, NA_W), lambda i, j: (j, layer, 0, 0)))
        aliases = {4: 2, 5: 3}
    return pl.pallas_call(
        functools.partial(_inproj_kernel, want_kt=want_kt),
        grid=(b, l // tm),
        in_specs=in_specs,
        out_specs=out_specs,
        out_shape=out_shape,
        input_output_aliases=aliases,
        compiler_params=_cparams(("parallel", "parallel")),
        name="inproj",
    )(*args)


def _ctx_attn_kernel(q_ref, k_ref, v_ref, o_ref, s_ref):
    scale = NA_DH ** -0.5
    maxes = []
    for h in range(NA_HEADS):
        sl = slice(h * NA_DH, (h + 1) * NA_DH)
        q = q_ref[0, :, sl] * scale
        s = _dot_nt(q, k_ref[0, :, sl])
        s_ref[h] = s
        maxes.append(jnp.max(_lane_fold(jnp.maximum, s), axis=-1, keepdims=True))
    for h in range(NA_HEADS):
        sl = slice(h * NA_DH, (h + 1) * NA_DH)
        p = jnp.exp(s_ref[h] - maxes[h])
        den = jnp.sum(_lane_fold(jnp.add, p), axis=-1, keepdims=True)
        o = _dot(p.astype(BF16), v_ref[0, :, sl]) / den
        o_ref[0, :, sl] = o.astype(BF16)


def _ctx_attention(qkv):
    b, l, _ = qkv.shape
    spec = lambda c: pl.BlockSpec((1, l, NA_W), lambda i, c=c: (i, 0, c))
    return pl.pallas_call(
        _ctx_attn_kernel,
        grid=(b,),
        in_specs=[spec(0), spec(1), spec(2)],
        out_specs=pl.BlockSpec((1, l, NA_W), lambda i: (i, 0, 0)),
        out_shape=jax.ShapeDtypeStruct((b, l, NA_W), BF16),
        scratch_shapes=[pltpu.VMEM((NA_HEADS, l, l), F32)],
        compiler_params=_cparams(("parallel",)),
        name="ctx_attention",
    )(qkv, qkv, qkv)


NA_QROWS = 2
NA_KROWS = WIN_R + NA_QROWS


def _na_window_start(p, rows):
    return jnp.clip(NA_QROWS * p - WIN_R // 2, 0, rows - NA_KROWS)


def _lane_fold(op, *arrays):
    tiles = [a[:, i:i + LANES] for a in arrays for i in range(0, a.shape[1], LANES)]
    while len(tiles) > 1:
        tiles = [op(tiles[i], tiles[i + 1]) if i + 1 < len(tiles) else tiles[i]
                 for i in range(0, len(tiles), 2)]
    return tiles[0]


def _na_kernel(q_ref, kt_ref, v_ref, ckt_ref, cv_ref, bias_ref, o_ref, s_ref, *, rows):
    ws = _na_window_start(pl.program_id(1), rows)
    start = pl.multiple_of(ws * GRID_W, NA_QROWS * GRID_W)
    n_loc = NA_KROWS * GRID_W
    scale = NA_DH ** -0.5
    maxes = []
    for h in range(NA_HEADS):
        sl = slice(h * NA_DH, (h + 1) * NA_DH)
        q = q_ref[0, :, sl] * scale
        s_loc = _dot(q, kt_ref[0, sl, pl.ds(start, n_loc)]) + bias_ref[0, 0, h]
        s_ctx = _dot(q, ckt_ref[0, 0, sl, :])
        s_ref[h, :, :n_loc] = s_loc
        s_ref[h, :, n_loc:] = s_ctx
        maxes.append(jnp.max(_lane_fold(jnp.maximum, s_loc, s_ctx), axis=-1, keepdims=True))
    for h in range(NA_HEADS):
        sl = slice(h * NA_DH, (h + 1) * NA_DH)
        p_loc = jnp.exp(s_ref[h, :, :n_loc] - maxes[h])
        p_ctx = jnp.exp(s_ref[h, :, n_loc:] - maxes[h])
        den = jnp.sum(_lane_fold(jnp.add, p_loc, p_ctx), axis=-1, keepdims=True)
        o = (_dot(p_loc.astype(BF16), v_ref[0, pl.ds(start, n_loc), sl])
             + _dot(p_ctx.astype(BF16), cv_ref[0, 0, :, sl])) / den
        o_ref[0, :, sl] = o.astype(BF16)


NA_EDGE = 2


def _na_variant(p, n_pairs):
    return jnp.where(p < NA_EDGE, p, jnp.where(p >= n_pairs - NA_EDGE, p - (n_pairs - 2 * NA_EDGE - 1), NA_EDGE))


def _na_bias_slabs(rpb, rows):
    n_pairs = rows // NA_QROWS
    ps = np.array([0, 1, 2, n_pairs - 2, n_pairs - 1])
    r = NA_QROWS * ps[:, None] + np.arange(NA_QROWS)[None, :]
    rs = np.clip(r - WIN_R // 2, 0, rows - WIN_R)
    ws = np.clip(NA_QROWS * ps - WIN_R // 2, 0, rows - NA_KROWS)
    krow = ws[:, None] + np.arange(NA_KROWS)[None, :]
    row_ok = (krow[:, None, :] >= rs[:, :, None]) & (krow[:, None, :] < rs[:, :, None] + WIN_R)
    dr = np.clip(krow[:, None, :] - r[:, :, None] + (WIN_R - 1), 0, 2 * WIN_R - 2)
    j = np.arange(GRID_W)[:, None]
    kc = np.arange(GRID_W)[None, :]
    cs = np.clip(j - WIN_C // 2, 0, GRID_W - WIN_C)
    col_ok = (kc >= cs) & (kc < cs + WIN_C)
    dc = np.clip(kc - j + (WIN_C - 1), 0, 2 * WIN_C - 2)
    pick_r = jnp.asarray(dr[..., None] == np.arange(2 * WIN_R - 1), F32)
    pick_c = jnp.asarray(np.arange(2 * WIN_C - 1)[:, None, None] == dc[None], F32)
    exact = lax.Precision.HIGHEST
    planes = jnp.einsum('lhac,cjk->lhajk', rpb.astype(F32), pick_c, precision=exact)
    tab = jnp.einsum('vqia,lhajk->lvhqjik', pick_r, planes, precision=exact)
    ok = row_ok[:, None, :, None, :, None] & col_ok[None, None, None, :, None, :]
    tab = jnp.where(jnp.asarray(ok)[None], tab, -jnp.inf)
    return tab.reshape(rpb.shape[0], len(ps), NA_HEADS, NA_QROWS * GRID_W, NA_KROWS * GRID_W)


def _na_attention(qkv, kt, ckt, cv, bias, layer):
    b, l, _ = qkv.shape
    rows = l // GRID_W
    n_pairs = rows // NA_QROWS
    lc = cv.shape[2]
    tq = NA_QROWS * GRID_W
    n_loc = NA_KROWS * GRID_W
    return pl.pallas_call(
        functools.partial(_na_kernel, rows=rows),
        grid=(b, n_pairs),
        in_specs=[
            pl.BlockSpec((1, tq, NA_W), lambda i, p: (i, p, 0)),
            pl.BlockSpec((1, NA_W, l), lambda i, p: (i, 0, 0)),
            pl.BlockSpec((1, l, NA_W), lambda i, p: (i, 0, 2)),
            pl.BlockSpec((1, 1, NA_W, lc), lambda i, p: (i, layer, 0, 0)),
            pl.BlockSpec((1, 1, lc, NA_W), lambda i, p: (i, layer, 0, 0)),
            pl.BlockSpec((1, 1, NA_HEADS, tq, n_loc),
                         lambda i, p: (layer, _na_variant(p, n_pairs), 0, 0, 0)),
        ],
        out_specs=pl.BlockSpec((1, tq, NA_W), lambda i, p: (i, p, 0)),
        out_shape=jax.ShapeDtypeStruct((b, l, NA_W), BF16),
        scratch_shapes=[pltpu.VMEM((NA_HEADS, tq, n_loc + lc), F32)],
        compiler_params=_cparams(("parallel", "arbitrary")),
        name="na_attention",
    )(qkv, kt, qkv, ckt, cv, bias)


def _split2(x):
    hi = x.astype(BF16)
    lo = (x - hi.astype(F32)).astype(BF16)
    return hi, lo


def _block_ref(cum, block, ref_row):
    c = cum.shape[0]
    ref = cum.reshape(c // block, block, HG_D)[:, ref_row:ref_row + 1, :]
    return jnp.broadcast_to(ref, (c // block, block, HG_D)).reshape(c, HG_D)


def _hgrn_level(cum_f, cum_b, qs, k_f, k_b, m, ti, si):
    c = cum_f.shape[0]
    ref_f = _block_ref(cum_f, 2 * m, m - 1)
    ref_b = _block_ref(cum_b, 2 * m, m)
    row = lax.broadcasted_iota(jnp.int32, (c, 1), 0)
    upper = jnp.bitwise_and(row, m) != 0
    dec_f = jnp.exp(jnp.where(upper, cum_f - ref_f, ref_f - cum_f))
    dec_b = jnp.exp(jnp.where(upper, ref_b - cum_b, cum_b - ref_b))
    qm = jnp.concatenate([jnp.where(upper, qs * dec_f, 0.0), jnp.where(upper, 0.0, qs * dec_b)], axis=1)
    km = jnp.concatenate([jnp.where(upper, 0.0, k_f * dec_f), jnp.where(upper, k_b * dec_b, 0.0)], axis=1)
    am = _dot_nt(qm.astype(BF16), km.astype(BF16))
    if 2 * m < c:
        am = jnp.where(jnp.bitwise_xor(ti, si) < 2 * m, am, 0.0)
    return am


def _hgrn_pairwise(cum, qs, k, v, reverse):
    c = cum.shape[0]
    sub_row = lax.broadcasted_iota(jnp.int32, (SUBLANES, 1), 0)
    tiles_per_sub = HG_SUB // SUBLANES
    o_tiles = []
    for blk in range(c // HG_SUB):
        accs = [jnp.zeros((SUBLANES, HG_D), F32) for _ in range(tiles_per_sub)]
        for j in range(HG_SUB):
            s = blk * HG_SUB + j
            cs, ks, vs = cum[s:s + 1, :], k[s:s + 1, :], v[s:s + 1, :]
            for t in range(tiles_per_sub):
                lo_r, hi_r = t * SUBLANES, t * SUBLANES + SUBLANES - 1
                if (lo_r > j) if reverse else (hi_r < j):
                    continue
                base = blk * HG_SUB + lo_r
                x = qs[base:base + SUBLANES, :] * ks * jnp.exp(
                    jnp.minimum(cum[base:base + SUBLANES, :] - cs, 0.0))
                a = jnp.sum(x, axis=-1, keepdims=True)
                if reverse and hi_r > j:
                    a = jnp.where(sub_row + lo_r <= j, a, 0.0)
                elif (not reverse) and lo_r < j:
                    a = jnp.where(sub_row + lo_r >= j, a, 0.0)
                accs[t] = accs[t] + a * vs
        o_tiles.extend(accs)
    return jnp.concatenate(o_tiles, axis=0)


def _hgrn_gates(z, consts):
    lb, log1mlb, oml = consts
    e = jnp.exp(-jnp.abs(z))
    inv = 1.0 / (1.0 + e)
    pos = z >= 0.0
    f = lb + oml * (jnp.where(pos, 1.0, e) * inv)
    logf = jnp.where(f > 0.0, jnp.log(f), log1mlb + z)
    k = oml * (jnp.where(pos, e, 1.0) * inv)
    return logf, k


def _flat_offsets(cum_f, cum_b):
    return (cum_f - _block_ref(cum_f, HG_FLAT, HG_FLAT // 2 - 1),
            cum_b - _block_ref(cum_b, HG_FLAT, HG_FLAT // 2))


def _hgrn_kernel(q_ref, i_ref, zf_ref, zb_ref, g_ref, lbc_ref, nw_ref, s0_ref,
                 o_ref, s_ref, cum_ref, k_ref, qs_ref, acc_ref, qt_ref, u_ref, de_ref, *, nh):
    l = q_ref.shape[1]
    c = HG_CHUNK
    nc = l // c
    nu = nh * nc
    ti = lax.broadcasted_iota(jnp.int32, (c, c), 0)
    si = lax.broadcasted_iota(jnp.int32, (c, c), 1)

    def unit(u):
        if nh == 1:
            return pl.ds(pl.multiple_of(u * c, c), c), slice(None), slice(None)
        h = u // nc
        rows = pl.ds(pl.multiple_of((u - h * nc) * c, c), c)
        return (rows, pl.ds(pl.multiple_of(h * HG_D, HG_D), HG_D),
                pl.ds(pl.multiple_of(h * 2 * HG_D, 2 * HG_D), 2 * HG_D))

    def prepare(u, gmax):
        rows, lanes, _ = unit(u)
        consts = tuple(lbc_ref[r:r + 1, lanes] for r in range(3))
        q = q_ref[0, rows, lanes]
        qs_ref[rows, lanes] = q * jax.nn.sigmoid(q)
        cums = []
        for d, z_ref in enumerate((zf_ref, zb_ref)):
            logf, k = _hgrn_gates(z_ref[0, rows, lanes], consts)
            k_ref[d, rows, lanes] = k
            tri = jnp.where((si >= ti) if d else (si <= ti), 1.0, 0.0).astype(BF16)
            parts = _dot(tri, jnp.concatenate(_split2(logf), axis=1))
            cum = parts[:, :HG_D] + parts[:, HG_D:]
            cum_ref[d, rows, lanes] = cum
            cums.append(cum)
        dq_f, dq_b = _flat_offsets(*cums)
        g = jnp.maximum(jnp.abs(dq_f), jnp.abs(dq_b))
        return jnp.maximum(gmax, jnp.max(g.reshape(c // SUBLANES, SUBLANES, HG_D), axis=0))

    gmax = lax.fori_loop(0, nu, prepare, jnp.zeros((SUBLANES, HG_D), F32), unroll=HG_UNROLL)
    flat_ok = jnp.max(gmax) <= HG_FLAT_RANGE

    def intra(u, flat):
        rows, lanes, lanes2 = unit(u)
        cum_f, cum_b = cum_ref[0, rows, lanes], cum_ref[1, rows, lanes]
        k_f, k_b = k_ref[0, rows, lanes], k_ref[1, rows, lanes]
        qs = qs_ref[rows, lanes]
        v = i_ref[0, rows, lanes]
        vb = v.astype(BF16)
        a = jnp.zeros((c, c), F32)
        m = c // 2
        while m >= (HG_FLAT if flat else HG_SUB):
            a = a + _hgrn_level(cum_f, cum_b, qs, k_f, k_b, m, ti, si)
            m //= 2
        if flat:
            dq_f, dq_b = _flat_offsets(cum_f, cum_b)
            same = jnp.bitwise_xor(ti, si) < HG_FLAT
            a_f = _dot_nt((qs * jnp.exp(dq_f)).astype(BF16), (k_f * jnp.exp(-dq_f)).astype(BF16))
            a_b = _dot_nt((qs * jnp.exp(dq_b)).astype(BF16), (k_b * jnp.exp(-dq_b)).astype(BF16))
            a = a + jnp.where(same & (si <= ti), a_f, 0.0) + jnp.where(same & (si >= ti), a_b, 0.0)
            o = _dot(a.astype(BF16), vb)
        else:
            o = (_dot(a.astype(BF16), vb) + _hgrn_pairwise(cum_f, qs, k_f, v, False)
                 + _hgrn_pairwise(cum_b, qs, k_b, v, True))
        acc_ref[rows, lanes] = o
        edge_f, edge_b = cum_f[c - 1:c, :], cum_b[0:1, :]
        qt_ref[rows, lanes2] = jnp.concatenate([(qs * jnp.exp(cum_f)).astype(BF16),
                                                (qs * jnp.exp(cum_b)).astype(BF16)], axis=1)
        kd = jnp.concatenate([(k_f * jnp.exp(edge_f - cum_f)).astype(BF16),
                              (k_b * jnp.exp(edge_b - cum_b)).astype(BF16)], axis=1)
        u_ref[u] = _dot_tn(vb, kd)
        de_ref[u] = jnp.broadcast_to(jnp.concatenate([jnp.exp(edge_f), jnp.exp(edge_b)], axis=1),
                                     (SUBLANES, 2 * HG_D))

    def run_intra(flat):
        def body(u, carry):
            intra(u, flat)
            return carry
        lax.fori_loop(0, nu, body, 0, unroll=HG_UNROLL if flat else 1)

    pl.when(flat_ok)(lambda: run_intra(True))
    pl.when(jnp.logical_not(flat_ok))(lambda: run_intra(False))

    for h in range(nh):
        def sweep(i, carry, h=h):
            st_f, st_b = carry
            uf, ub = h * nc + i, h * nc + nc - 1 - i
            inc = u_ref[uf, :, :HG_D]
            u_ref[uf, :, :HG_D] = st_f
            st_f = st_f * de_ref[uf, 0:1, :HG_D] + inc
            inc = u_ref[ub, :, HG_D:]
            u_ref[ub, :, HG_D:] = st_b
            st_b = st_b * de_ref[ub, 0:1, HG_D:] + inc
            return st_f, st_b

        st_f, st_b = lax.fori_loop(0, nc, sweep, (s0_ref[0, 0, h].T, s0_ref[0, 1, h].T))
        s_ref[0, 0, h] = st_f.T
        s_ref[0, 1, h] = st_b.T

    def finish(u, carry):
        rows, lanes, lanes2 = unit(u)
        o = acc_ref[rows, lanes] + _dot_nt(qt_ref[rows, lanes2], u_ref[u].astype(BF16))
        o = o * lax.rsqrt(jnp.mean(o * o, axis=-1, keepdims=True) + EPS) * nw_ref[...]
        g = g_ref[0, rows, lanes]
        o_ref[0, rows, lanes] = (o * (g * jax.nn.sigmoid(g))).astype(BF16)
        return carry

    lax.fori_loop(0, nu, finish, 0, unroll=HG_UNROLL)


HG_VMEM_BUDGET = 44 * 1024 * 1024


def _hgrn_heads_per_step(l):
    per_head = l * HG_D * (5 * 2 * 4 + 6 * 4 + 2 * 2 + 8)
    nh = HG_HEADS
    while nh > 1 and nh * per_head > HG_VMEM_BUDGET:
        nh //= 2
    return nh


def _hgrn(hgp, lbc, norm_w, s0):
    b, l, _ = hgp.shape
    nh = _hgrn_heads_per_step(l)
    groups = HG_HEADS // nh
    w = nh * HG_D
    nu = nh * (l // HG_CHUNK)
    col = lambda sec: pl.BlockSpec((1, l, w), lambda i, h, sec=sec: (i, 0, sec * groups + h))
    state_spec = pl.BlockSpec((1, 2, nh, HG_D, HG_D), lambda i, h: (i, 0, h, 0, 0))
    return pl.pallas_call(
        functools.partial(_hgrn_kernel, nh=nh),
        grid=(b, groups),
        in_specs=[col(0), col(1), col(2), col(3), col(4),
                  pl.BlockSpec((SUBLANES, w), lambda i, h: (0, h)),
                  pl.BlockSpec((1, HG_D), lambda i, h: (0, 0)),
                  state_spec],
        out_specs=[pl.BlockSpec((1, l, w), lambda i, h: (i, 0, h)), state_spec],
        out_shape=[jax.ShapeDtypeStruct((b, l, HG_W), BF16),
                   jax.ShapeDtypeStruct((b, 2, HG_HEADS, HG_D, HG_D), F32)],
        scratch_shapes=[pltpu.VMEM((2, l, w), F32),
                        pltpu.VMEM((2, l, w), F32),
                        pltpu.VMEM((l, w), F32),
                        pltpu.VMEM((l, w), F32),
                        pltpu.VMEM((l, 2 * w), BF16),
                        pltpu.VMEM((nu, HG_D, 2 * HG_D), F32),
                        pltpu.VMEM((nu, SUBLANES, 2 * HG_D), F32)],
        compiler_params=_cparams(("parallel", "parallel")),
        name="hgrn",
    )(hgp, hgp, hgp, hgp, hgp, lbc, norm_w.reshape(1, HG_D), s0)


def _mixer_residual(x_ref, att_ref, hg_ref, wo_ref, mod_ref):
    y = _dot(att_ref[0], wo_ref[:NA_W, :]) + _dot(hg_ref[0], wo_ref[NA_W:, :])
    return x_ref[0] + mod_ref[0, 2:3, :] * y


def _mixer_specs(tm, d, mod):
    tile = lambda w: pl.BlockSpec((1, tm, w), lambda i, j, *_: (i, j, 0))
    const = lambda shape: pl.BlockSpec(shape, lambda *_: (0,) * len(shape))
    return [tile(d), tile(NA_W), tile(HG_W), const((NA_W + HG_W, d)), const((1, d)),
            pl.BlockSpec((1, SUBLANES, d), _mod_index(mod))]


def _ffn_kernel(x_ref, att_ref, hg_ref, wo_ref, nw_ref, mod_ref, w1_ref, w3_ref, w2_ref, o_ref,
                h_ref, acc_ref, xn_ref):
    f = pl.program_id(2)

    @pl.when(f == 0)
    def _():
        xn = _mixer_residual(x_ref, att_ref, hg_ref, wo_ref, mod_ref)
        xn_ref[...] = xn
        h = _norm_mod(xn, nw_ref[...], mod_ref[0, 3:4, :], mod_ref[0, 4:5, :])
        h_ref[...] = h.astype(BF16)
        acc_ref[...] = jnp.zeros_like(acc_ref)

    h = h_ref[...]
    a = _dot(h, w1_ref[...])
    g = (a * jax.nn.sigmoid(a)) * _dot(h, w3_ref[...])
    acc_ref[...] += _dot(g.astype(BF16), w2_ref[...])

    @pl.when(f == pl.num_programs(2) - 1)
    def _():
        o_ref[0] = xn_ref[...] + mod_ref[0, 5:6, :] * acc_ref[...]


def _ffn(x, att, hg, w_out, nw, mod, w1, w3, w2, tm=512, tf=1408):
    b, l, d = x.shape
    tm = min(tm, l)
    ff = w1.shape[1]
    tf = min(tf, ff)
    return pl.pallas_call(
        _ffn_kernel,
        grid=(b, l // tm, ff // tf),
        in_specs=_mixer_specs(tm, d, mod) + [
            pl.BlockSpec((d, tf), lambda i, j, f: (0, f)),
            pl.BlockSpec((d, tf), lambda i, j, f: (0, f)),
            pl.BlockSpec((tf, d), lambda i, j, f: (f, 0)),
        ],
        out_specs=pl.BlockSpec((1, tm, d), lambda i, j, f: (i, j, 0)),
        out_shape=jax.ShapeDtypeStruct(x.shape, F32),
        scratch_shapes=[pltpu.VMEM((tm, d), BF16), pltpu.VMEM((tm, d), F32), pltpu.VMEM((tm, d), F32)],
        compiler_params=_cparams(("parallel", "parallel", "arbitrary")),
        name="ffn",
    )(x, att, hg, w_out, nw.reshape(1, d), mod, w1, w3, w2)


def _router_kernel(x_ref, att_ref, hg_ref, wo_ref, nw_ref, mod_ref, r_ref, xn_ref, hp_ref, idx_ref, wt_ref):
    xn = _mixer_residual(x_ref, att_ref, hg_ref, wo_ref, mod_ref)
    xn_ref[0] = xn
    h = _norm_mod(xn, nw_ref[...], mod_ref[0, 3:4, :], mod_ref[0, 4:5, :])
    hh = h.astype(BF16)
    hl = (h - hh.astype(F32)).astype(BF16)
    r = r_ref[...]
    rh = r.astype(BF16)
    rl = (r - rh.astype(F32)).astype(BF16)
    logits = _dot(hh, rh) + _dot(hh, rl) + _dot(hl, rh)
    lane = lax.broadcasted_iota(jnp.int32, logits.shape, 1)
    logits = jnp.where(lane < N_EXPERTS, logits, -jnp.inf)
    m1 = jnp.max(logits, axis=-1, keepdims=True)
    i1 = jnp.min(jnp.where(logits == m1, lane, LANES), axis=-1, keepdims=True)
    rest = jnp.where(lane == i1, -jnp.inf, logits)
    m2 = jnp.max(rest, axis=-1, keepdims=True)
    i2 = jnp.min(jnp.where(rest == m2, lane, LANES), axis=-1, keepdims=True)
    e2 = jnp.exp(m2 - m1)
    den = 1.0 + e2
    idx_ref[0] = jnp.where(lane == 0, i1, jnp.where(lane == 1, i2, 0))
    wt_ref[0] = jnp.where(lane == 0, 1.0 / den, jnp.where(lane == 1, e2 / den, 0.0))
    hp_ref[0] = h


def _router(x, att, hg, w_out, nw, mod, router, tm=512):
    b, l, d = x.shape
    tm = min(tm, l)
    rp = jnp.zeros((d, LANES), F32).at[:, :router.shape[1]].set(router)
    tile = lambda w: pl.BlockSpec((1, tm, w), lambda i, j: (i, j, 0))
    return pl.pallas_call(
        _router_kernel,
        grid=(b, l // tm),
        in_specs=_mixer_specs(tm, d, mod) + [pl.BlockSpec((d, LANES), lambda i, j: (0, 0))],
        out_specs=[tile(d), tile(d), tile(LANES), tile(LANES)],
        out_shape=[jax.ShapeDtypeStruct((b, l, d), F32),
                   jax.ShapeDtypeStruct((b, l, d), F32),
                   jax.ShapeDtypeStruct((b, l, LANES), jnp.int32),
                   jax.ShapeDtypeStruct((b, l, LANES), F32)],
        compiler_params=_cparams(("parallel", "parallel")),
        name="router",
    )(x, att, hg, w_out, nw.reshape(1, d), mod, rp)


MOE_TM = 512
MOE_TG = 1024
TOP_K = 2


def _route_plan(idx, n_experts, tm):
    t = idx.shape[0]
    chosen = idx[:, :, None] == jnp.arange(n_experts, dtype=jnp.int32)[None, None, :]
    sel = jnp.any(chosen, axis=1)
    rank = jnp.cumsum(sel.astype(jnp.int32), axis=0) - 1
    count = rank[-1] + 1
    tiles = (count + tm - 1) // tm
    tile_end = jnp.cumsum(tiles)
    start = (tile_end - tiles) * tm
    dest = jnp.sum(jnp.where(chosen, (start[None, :] + rank)[:, None, :], 0), axis=-1)
    n_tiles_max = (t * TOP_K) // tm + n_experts + 1
    used = jnp.minimum(jnp.arange(n_tiles_max, dtype=jnp.int32), tile_end[-1] - 1)
    tile_expert = jnp.sum((used[:, None] >= tile_end[None, :]).astype(jnp.int32), axis=1)
    spare = jnp.arange(n_experts + 1, dtype=jnp.int32) * tm + tile_end[-1] * tm
    fills = jnp.concatenate([(start + count) // SUBLANES * SUBLANES,
                             jnp.minimum(spare, (n_tiles_max - 1) * tm - SUBLANES)])
    return (dest.astype(jnp.int32), tile_expert.astype(jnp.int32), tile_end[-1:].astype(jnp.int32),
            fills.astype(jnp.int32))


def _row_copies(dest_ref, src_of, dst_of, sem, n):
    def each(group, fn):
        base = pl.multiple_of(group * SUBLANES, SUBLANES)
        for j in range(SUBLANES):
            for s in range(TOP_K):
                r = base + j
                row = dest_ref[0, s, r]
                fn(pltpu.make_async_copy(src_of(s, r, row), dst_of(s, r, row), sem))

    def start(group, carry):
        each(group, lambda copy: copy.start())
        return carry

    def wait(group, carry):
        each(group, lambda copy: copy.wait())
        return carry

    lax.fori_loop(0, n // SUBLANES, start, 0)
    lax.fori_loop(0, n // SUBLANES, wait, 0)


def _dispatch_kernel(pad_ref, dest_ref, hp_ref, xs_ref, zero_ref, sem):
    @pl.when(pl.program_id(0) == 0)
    def _():
        zero_ref[...] = jnp.zeros_like(zero_ref)
        for e in range(pad_ref.shape[0]):
            first = pl.multiple_of(pad_ref[e], SUBLANES)
            copy = pltpu.make_async_copy(zero_ref, xs_ref.at[pl.ds(first, zero_ref.shape[0])], sem)
            copy.start()
            copy.wait()

    _row_copies(dest_ref,
                lambda s, r, row: hp_ref.at[pl.ds(r, 1)],
                lambda s, r, row: xs_ref.at[pl.ds(row, 1)],
                sem, hp_ref.shape[0])


def _dispatch(hp, dest_tiles, pad_start, n_rows):
    t, w = hp.shape
    return pl.pallas_call(
        _dispatch_kernel,
        grid_spec=pltpu.PrefetchScalarGridSpec(
            num_scalar_prefetch=1,
            grid=(t // MOE_TG,),
            in_specs=[
                pl.BlockSpec((1, TOP_K, MOE_TG), lambda i, pad: (i, 0, 0), memory_space=pltpu.SMEM),
                pl.BlockSpec((MOE_TG, w), lambda i, pad: (i, 0)),
            ],
            out_specs=pl.BlockSpec(memory_space=pl.ANY),
            scratch_shapes=[pltpu.VMEM((MOE_TM + SUBLANES, w), hp.dtype), pltpu.SemaphoreType.DMA],
        ),
        out_shape=jax.ShapeDtypeStruct((n_rows, w), hp.dtype),
        compiler_params=_cparams(("arbitrary",)),
        name="moe_dispatch",
    )(pad_start, dest_tiles, hp)


def _expert_kernel(te_ref, nt_ref, xs_ref, w1_ref, w3_ref, w2_ref, y_ref):
    del te_ref
    i = pl.program_id(0)

    @pl.when(i < nt_ref[0])
    def _():
        h = xs_ref[...].astype(BF16)
        a = _dot(h, w1_ref[0])
        g = (a * jax.nn.sigmoid(a)) * _dot(h, w3_ref[0])
        y_ref[...] = _dot(g.astype(BF16), w2_ref[0])

    @pl.when(i >= nt_ref[0])
    def _():
        y_ref[...] = jnp.zeros_like(y_ref)


def _experts(xs, tile_expert, n_tiles, w1, w3, w2):
    n_rows, w = xs.shape
    _, d, ff = w1.shape
    return pl.pallas_call(
        _expert_kernel,
        grid_spec=pltpu.PrefetchScalarGridSpec(
            num_scalar_prefetch=2,
            grid=(n_rows // MOE_TM,),
            in_specs=[
                pl.BlockSpec((MOE_TM, w), lambda i, te, nt: (i, 0)),
                pl.BlockSpec((1, d, ff), lambda i, te, nt: (te[i], 0, 0)),
                pl.BlockSpec((1, d, ff), lambda i, te, nt: (te[i], 0, 0)),
                pl.BlockSpec((1, ff, d), lambda i, te, nt: (te[i], 0, 0)),
            ],
            out_specs=pl.BlockSpec((MOE_TM, d), lambda i, te, nt: (i, 0)),
        ),
        out_shape=jax.ShapeDtypeStruct((n_rows, d), F32),
        compiler_params=_cparams(("arbitrary",)),
        name="moe_experts",
    )(tile_expert, n_tiles, xs, w1, w3, w2)


def _combine_kernel(dest_ref, x_ref, mod_ref, wt_ref, y_ref, *refs):
    o_ref, buf_ref, sem = refs[-3:]
    _row_copies(dest_ref,
                lambda s, r, row: y_ref.at[pl.ds(row, 1)],
                lambda s, r, row: buf_ref.at[s, pl.ds(r, 1)],
                sem, x_ref.shape[1])
    wt = wt_ref[0]
    mixed = wt[:, 0:1] * buf_ref[0] + wt[:, 1:2] * buf_ref[1]
    o = x_ref[0] + mod_ref[0, 5:6, :] * mixed
    if len(refs) > 3:
        o = o * lax.rsqrt(jnp.mean(o * o, axis=-1, keepdims=True) + EPS) * refs[0][...]
    o_ref[0] = o


def _combine(x, mod, wts, dest_tiles, y, final_nw=None):
    b, l, d = x.shape
    per_b = l // MOE_TG
    in_specs = [
        pl.BlockSpec((1, TOP_K, MOE_TG), lambda i, j: (i * per_b + j, 0, 0), memory_space=pltpu.SMEM),
        pl.BlockSpec((1, MOE_TG, d), lambda i, j: (i, j, 0)),
        pl.BlockSpec((1, SUBLANES, d), _mod_index(mod)),
        pl.BlockSpec((1, MOE_TG, LANES), lambda i, j: (i, j, 0)),
        pl.BlockSpec(memory_space=pl.ANY),
    ]
    args = [dest_tiles, x, mod, wts, y]
    if final_nw is not None:
        in_specs.append(pl.BlockSpec((1, d), lambda i, j: (0, 0)))
        args.append(final_nw.reshape(1, d))
    return pl.pallas_call(
        _combine_kernel,
        grid=(b, per_b),
        in_specs=in_specs,
        out_specs=pl.BlockSpec((1, MOE_TG, d), lambda i, j: (i, j, 0)),
        out_shape=jax.ShapeDtypeStruct(x.shape, F32),
        scratch_shapes=[pltpu.VMEM((TOP_K, MOE_TG, d), F32), pltpu.SemaphoreType.DMA],
        compiler_params=_cparams(("arbitrary", "arbitrary")),
        name="moe_combine",
    )(*args)


def _moe(x, att, hg, w_out, nw, mod, router, w1, w3, w2, final_nw=None):
    b, l, d = x.shape
    t = b * l
    x, hp, idx, wts = _router(x, att, hg, w_out, nw, mod, router)
    dest, tile_expert, n_tiles, pad_start = _route_plan(idx.reshape(t, LANES)[:, :TOP_K], w1.shape[0], MOE_TM)
    dest_tiles = dest.reshape(t // MOE_TG, MOE_TG, TOP_K).transpose(0, 2, 1)
    xs = _dispatch(hp.reshape(t, d), dest_tiles, pad_start, tile_expert.shape[0] * MOE_TM)
    y = _experts(xs, tile_expert, n_tiles, w1, w3, w2)
    return _combine(x, mod, wts, dest_tiles, y, final_nw)


def _final_norm_kernel(x_ref, nw_ref, o_ref):
    x = x_ref[0]
    o_ref[0] = x * lax.rsqrt(jnp.mean(x * x, axis=-1, keepdims=True) + EPS) * nw_ref[...]


def _final_norm(x, nw, tm=512):
    b, l, d = x.shape
    tm = min(tm, l)
    return pl.pallas_call(
        _final_norm_kernel,
        grid=(b, l // tm),
        in_specs=[pl.BlockSpec((1, tm, d), lambda i, j: (i, j, 0)),
                  pl.BlockSpec((1, d), lambda i, j: (0, 0))],
        out_specs=pl.BlockSpec((1, tm, d), lambda i, j: (i, j, 0)),
        out_shape=jax.ShapeDtypeStruct(x.shape, F32),
        compiler_params=_cparams(("parallel", "parallel")),
        name="final_norm",
    )(x, nw.reshape(1, d))


def _pad_rows(a, rows):
    return jnp.zeros((rows,) + a.shape[1:], a.dtype).at[:a.shape[0]].set(a)


def kernel(x_prompt, x_sample, cache_k, cache_v, state_hgrn, c, c_ctx, norm1_w, norm2_w, w_ada, b_ada,
           w_in, rpb, hg_lower, hg_norm_w, w_out, ffn_w1, ffn_w3, ffn_w2, moe_router, moe_w1, moe_w3,
           moe_w2, final_norm_w):
    depth = w_in.shape[0]
    d = x_prompt.shape[-1]
    nb = x_prompt.shape[0]
    nd = x_sample.shape[0]
    rows = x_sample.shape[1] // GRID_W

    lbs = jnp.cumsum(jax.nn.softmax(hg_lower.astype(F32), axis=0), axis=0)
    lbs = lbs - lbs[0:1]
    lbc = _pad_rows(jnp.stack([lbs, jnp.log1p(-lbs), 1.0 - lbs], axis=1).transpose(1, 0, 2),
                    SUBLANES).transpose(1, 0, 2)

    cond = _pad_rows(jnp.concatenate([c, c_ctx[None, :]], axis=0), SUBLANES)
    mods = _modulation(cond, w_ada, b_ada)
    mods = _pad_rows(mods.reshape(depth, SUBLANES, 6, d).transpose(2, 0, 1, 3),
                     SUBLANES).transpose(1, 2, 0, 3)

    def layer_weights(l):
        cast = lambda a: a.astype(BF16)
        i = l // 2
        mixer = ((ffn_w1[i], ffn_w3[i], ffn_w2[i]) if l % 2 == 0 else (moe_w1[i], moe_w3[i], moe_w2[i]))
        return dict(w_in=cast(w_in[l]), wkt=cast(w_in[l, :, NA_W:2 * NA_W].T), w_out=cast(w_out[l]),
                    mixer=tuple(cast(a) for a in mixer))

    lc = cache_k.shape[2]
    ckt_b = jnp.swapaxes(cache_k.astype(BF16).reshape(nd, depth, lc, NA_W), 2, 3)
    cv_b = cache_v.astype(BF16).reshape(nd, depth, lc, NA_W)
    zero_state = jnp.zeros((nb, 2, HG_HEADS, HG_D, HG_D), F32)

    def layer(x, seq_shape, mod, l, wts, attend, s0, caches=None):
        tok = lambda a: a.reshape(x.shape[:2] + a.shape[-1:])
        seq = lambda a: a.reshape(seq_shape + a.shape[-1:])
        outs = _inproj(x, norm1_w[l], mod, wts["w_in"], None if caches else wts["wkt"], caches, l)
        att = attend(seq(outs[0]), outs[2])
        hg, s_fin = _hgrn(seq(outs[1]), lbc[l], hg_norm_w[l], s0)
        mixed = (x, tok(att), tok(hg), wts["w_out"], norm2_w[l], mod)
        if l % 2 == 0:
            x = _ffn(*mixed, *wts["mixer"])
        else:
            x = _moe(*mixed, moe_router[l // 2], *wts["mixer"],
                     final_norm_w if (l == depth - 1) else None)
        return x, outs[2:], s_fin

    seq = x_prompt.shape[1]
    xp, xs = x_prompt.reshape(1, nb * seq, d), x_sample
    caches = tuple(jnp.zeros((nb, depth, seq, NA_W), F32) for _ in range(2))
    ss_out = []
    bias = _na_bias_slabs(rpb, rows)
    for l in range(depth):
        mod_p = mods[l, nd:nd + 1]
        mod_s = mods[l, :nd]
        wts = layer_weights(l)
        xp, caches, s_fin = layer(xp, (nb, seq), mod_p, l, wts, lambda qkv, _: _ctx_attention(qkv),
                                  zero_state, caches=caches)
        ss_out.append(s_fin)
        attend_lat = lambda qkv, kt, l=l: _na_attention(qkv, kt, ckt_b, cv_b, bias, l)
        xs, _, _ = layer(xs, x_sample.shape[:2], mod_s, l, wts, attend_lat,
                         state_hgrn[:, l].astype(F32))

    if depth % 2:
        xp, xs = _final_norm(xp, final_norm_w), _final_norm(xs, final_norm_w)
    y_prompt, y_sample = xp.reshape(x_prompt.shape), xs
    new_cache_k = caches[0].reshape(nb, depth, seq, NA_HEADS, NA_DH)
    new_cache_v = caches[1].reshape(nb, depth, seq, NA_HEADS, NA_DH)
    new_state = jnp.stack(ss_out, axis=1).astype(x_prompt.dtype)
    return (y_prompt, y_sample, new_cache_k, new_cache_v, new_state)
```
